```python
import jax, jax.numpy as jnp
from jax import lax
import numpy as np

D_MODEL = 2048
BATCH = 4
SEQ = 4096
DEPTH = 2

RW_HEAD = 64
RW_WIDTH = D_MODEL
RW_HEADS = RW_WIDTH // RW_HEAD
RW_DECAY_LORA = 96
RW_AAA_LORA = 96
RW_MV_LORA = 64
RW_GATE_LORA = 256
RW_GN_EPS = RW_HEAD * 1e-5
GM_WIDTH = D_MODEL
GM_CHUNK = 128
GM_GROUP = 128
GM_GROUPS = GM_WIDTH // GM_GROUP
NSA_HEADS = 16
NSA_KV_GROUPS = 4
NSA_HPG = NSA_HEADS // NSA_KV_GROUPS
NSA_DK = 192
NSA_DV = 128
NSA_WIDTH = NSA_HEADS * NSA_DV
CMP_BLK = 32
CMP_STRIDE = 16
SEL_BLK = 64
N_SEL = 16
WIN = 512
Q_BLK = 32
D_FF = 5632
N_BRANCH = 3
BRANCH_WIDTH = D_MODEL
ALPHA = (2 * DEPTH) ** 0.25
BETA = (8 * DEPTH) ** -0.25
LN_EPS = 1e-5
NEG = -1e30
FORCED = 1e6

RW_COLS = 3 * RW_WIDTH + RW_DECAY_LORA + RW_AAA_LORA + RW_GATE_LORA
GM_COLS = 2 * GM_WIDTH
NSA_Q_COLS = NSA_HEADS * NSA_DK
NSA_GK = NSA_KV_GROUPS * NSA_DK
NSA_GV = NSA_KV_GROUPS * NSA_DV
NSA_KV_COLS = 3 * (NSA_GK + NSA_GV)
NSA_G_COLS = 3 * NSA_HEADS
NSA_COLS = NSA_Q_COLS + NSA_KV_COLS + NSA_G_COLS
GATE_COLS = N_BRANCH * D_MODEL
OFF_GM = RW_COLS
OFF_NSA = OFF_GM + GM_COLS
OFF_GATE = OFF_NSA + NSA_COLS
C_IN = OFF_GATE + GATE_COLS

kernel_name = 'hybrid_rwkv7_gmlp_nsa_macaron_deepnorm'


def layer_norm(x, g, b):
    xf = x.astype(jnp.float32)
    mu = jnp.mean(xf, -1, keepdims=True)
    var = jnp.mean(jnp.square(xf - mu), -1, keepdims=True)
    return ((xf - mu) * lax.rsqrt(var + LN_EPS) * g + b).astype(x.dtype)


def masked_softmax(s, mask):
    s = jnp.where(mask, s.astype(jnp.float32), NEG)
    e = jnp.where(mask, jnp.exp(s - jnp.max(s, -1, keepdims=True)), 0.0)
    return e / jnp.maximum(jnp.sum(e, -1, keepdims=True), 1e-30)


def swiglu(x, wg, wu, wd):
    return (jax.nn.silu(x @ wg) * (x @ wu)) @ wd


def wkv7_scan(r, w, k, v, a, b):
    B, T, H, N = r.shape

    def step(S, inp):
        r_t, w_t, k_t, v_t, a_t, b_t = inp
        sa = jnp.einsum('bhij,bhj->bhi', S, a_t)
        S = S * w_t[:, :, None, :] + sa[..., None] * b_t[:, :, None, :] + v_t[..., None] * k_t[:, :, None, :]
        return S, jnp.einsum('bhij,bhj->bhi', S, r_t)

    xs = [jnp.moveaxis(z, 1, 0) for z in (r, w, k, v, a, b)]
    _, y = lax.scan(step, jnp.zeros((B, H, N, N), jnp.float32), xs)
    return jnp.moveaxis(y, 0, 1)


def rwkv7_mix(p, v_first, vres, mu, w0, w2, a0, a2, g2, k_k, k_a, r_k, gn_g, gn_b):
    B, T, _ = p.shape
    prev = jnp.pad(p, ((0, 0), (1, 0), (0, 0)))[:, :-1]
    p = p + (prev - p) * mu
    cut = [RW_WIDTH, 2 * RW_WIDTH, 3 * RW_WIDTH, 3 * RW_WIDTH + RW_DECAY_LORA,
           3 * RW_WIDTH + RW_DECAY_LORA + RW_AAA_LORA]
    r, k, v, wl, al, gl = jnp.split(p, cut, axis=-1)
    w = -jax.nn.softplus(-(w0 + jnp.tanh(wl) @ w2)) - 0.5
    a = jax.nn.sigmoid(a0 + al @ a2)
    g = jax.nn.sigmoid(gl) @ g2
    if vres is None:
        v_first = v
    else:
        v0, v1, v2 = vres
        v = v + (v_first - v) * jax.nn.sigmoid(v0 + (v @ v1) @ v2)

    def hs(z):
        return z.reshape(B, T, RW_HEADS, RW_HEAD).astype(jnp.float32)

    kk = hs(k * k_k)
    kk = kk * lax.rsqrt(jnp.maximum(jnp.sum(jnp.square(kk), -1, keepdims=True), 1e-24))
    k = k * (1.0 + (a - 1.0) * k_a)
    rh, kh, vh, ah = hs(r), hs(k), hs(v), hs(a)
    decay = jnp.exp(-jnp.exp(hs(w)))
    y = wkv7_scan(rh, decay, kh, vh, -kk, kk * ah)
    m = jnp.mean(y, -1, keepdims=True)
    var = jnp.mean(jnp.square(y - m), -1, keepdims=True)
    y = ((y - m) * lax.rsqrt(var + RW_GN_EPS)).reshape(B, T, RW_WIDTH) * gn_g + gn_b
    bonus = jnp.sum(rh * kh * r_k, -1, keepdims=True) * vh
    y = (y + bonus.reshape(B, T, RW_WIDTH)) * g
    return y.astype(p.dtype), v_first


def gmlp_mix(p, ln_g, ln_b, ws, bs):
    B, T, _ = p.shape
    u, v = jnp.split(jax.nn.gelu(p), 2, axis=-1)
    v = layer_norm(v, ln_g, ln_b).reshape(B, T // GM_CHUNK, GM_CHUNK, GM_GROUPS, GM_GROUP)
    causal = jnp.tril(jnp.ones((GM_CHUNK, GM_CHUNK), ws.dtype))
    s = jnp.einsum('gts,bcsgd->bctgd', ws * causal, v) + bs.T[None, None, :, :, None]
    return u * s.reshape(B, T, GM_WIDTH)


def compress(z, pos, w1, w2):
    B, T, G, d = z.shape
    n_c = (T - CMP_BLK) // CMP_STRIDE + 1
    idx = np.arange(n_c)[:, None] * CMP_STRIDE + np.arange(CMP_BLK)[None, :]
    blocks = z[:, idx] + pos[None, None, :, None, :]
    flat = blocks.transpose(0, 3, 1, 2, 4).reshape(B, G, n_c, CMP_BLK * d)
    return jax.nn.gelu(flat @ w1) @ w2


def nsa_mix(p, pos_k, pos_v, phi_k1, phi_k2, phi_v1, phi_v2):
    B, T, _ = p.shape
    G, h = NSA_KV_GROUPS, NSA_HPG
    q = p[..., :NSA_Q_COLS].reshape(B, T, G, h, NSA_DK).transpose(0, 2, 3, 1, 4)
    kv = p[..., NSA_Q_COLS:NSA_Q_COLS + NSA_KV_COLS]
    cut = [NSA_GK, NSA_GK + NSA_GV, 2 * NSA_GK + NSA_GV, 2 * NSA_GK + 2 * NSA_GV, 3 * NSA_GK + 2 * NSA_GV]
    kc, vc, ks, vs, kw, vw = jnp.split(kv, cut, axis=-1)
    gates = jax.nn.sigmoid(p[..., NSA_Q_COLS + NSA_KV_COLS:]).reshape(B, T, G, h, 3).transpose(0, 2, 3, 1, 4)

    k_cmp = compress(kc.reshape(B, T, G, NSA_DK), pos_k, phi_k1, phi_k2)
    v_cmp = compress(vc.reshape(B, T, G, NSA_DV), pos_v, phi_v1, phi_v2)
    n_c = k_cmp.shape[2]
    n_s = T // SEL_BLK
    k_sel = min(N_SEL, n_s)
    ks_blk = ks.reshape(B, T, G, NSA_DK).transpose(0, 2, 1, 3).reshape(B, G, n_s, SEL_BLK, NSA_DK)
    vs_blk = vs.reshape(B, T, G, NSA_DV).transpose(0, 2, 1, 3).reshape(B, G, n_s, SEL_BLK, NSA_DV)
    pad = ((0, 0), (0, 0), (WIN, 0), (0, 0))
    kw_pad = jnp.pad(kw.reshape(B, T, G, NSA_DK).transpose(0, 2, 1, 3), pad)
    vw_pad = jnp.pad(vw.reshape(B, T, G, NSA_DV).transpose(0, 2, 1, 3), pad)

    cmp_start = jnp.arange(n_c) * CMP_STRIDE
    cmp_end = cmp_start + CMP_BLK - 1
    blk = jnp.arange(n_s)
    sel_start = blk * SEL_BLK
    overlap = ((cmp_start[:, None] < sel_start[None, :] + SEL_BLK)
               & (cmp_end[:, None] >= sel_start[None, :])).astype(jnp.float32)
    bi = jnp.arange(B)[:, None, None, None]
    gi = jnp.arange(G)[None, :, None, None]
    scale = NSA_DK ** -0.5

    def block(i):
        t0 = i * Q_BLK
        tq = t0 + jnp.arange(Q_BLK)
        qb = lax.dynamic_slice_in_dim(q, t0, Q_BLK, axis=3)
        gb = lax.dynamic_slice_in_dim(gates, t0, Q_BLK, axis=3)
        s_c = jnp.einsum('bghqd,bgnd->bghqn', qb, k_cmp) * scale
        p_c = masked_softmax(s_c, cmp_end[None, :] <= tq[:, None])
        o_c = jnp.einsum('bghqn,bgnd->bghqd', p_c, v_cmp)
        imp = jnp.einsum('bghqn,ns->bgqs', p_c, overlap)
        valid = sel_start[None, :] <= tq[:, None]
        cur = (tq // SEL_BLK)[:, None]
        forced = valid & ((blk[None, :] == 0) | (blk[None, :] == cur) | (blk[None, :] == cur - 1))
        score = jnp.where(forced, FORCED, jnp.where(valid, imp, NEG))
        top_val, top_idx = lax.top_k(score, k_sel)
        sel_ok = top_val > 0.5 * NEG
        kg = ks_blk[bi, gi, top_idx]
        vg = vs_blk[bi, gi, top_idx]
        s_s = jnp.einsum('bghqd,bgqkld->bghqkl', qb, kg) * scale
        kpos = top_idx[..., None] * SEL_BLK + jnp.arange(SEL_BLK)
        m_s = sel_ok[..., None] & (kpos <= tq[:, None, None])
        p_s = masked_softmax(s_s.reshape(B, G, h, Q_BLK, k_sel * SEL_BLK),
                             m_s.reshape(B, G, 1, Q_BLK, k_sel * SEL_BLK)).reshape(s_s.shape)
        o_s = jnp.einsum('bghqkl,bgqkld->bghqd', p_s, vg)
        kwb = lax.dynamic_slice_in_dim(kw_pad, t0, Q_BLK + WIN, axis=2)
        vwb = lax.dynamic_slice_in_dim(vw_pad, t0, Q_BLK + WIN, axis=2)
        kpos_w = t0 - WIN + jnp.arange(Q_BLK + WIN)
        m_w = ((kpos_w[None, :] <= tq[:, None]) & (kpos_w[None, :] > tq[:, None] - WIN)
               & (kpos_w[None, :] >= 0))
        s_w = jnp.einsum('bghqd,bgkd->bghqk', qb, kwb) * scale
        o_w = jnp.einsum('bghqk,bgkd->bghqd', masked_softmax(s_w, m_w), vwb)
        return gb[..., 0:1] * o_c + gb[..., 1:2] * o_s + gb[..., 2:3] * o_w

    out = lax.map(block, jnp.arange(T // Q_BLK))
    return out.transpose(1, 0, 4, 2, 3, 5).reshape(B, T, NSA_WIDTH).astype(p.dtype)


def setup_inputs(seed: int = 0) -> dict:
    key = jax.random.key(seed)
    ks = iter(jax.random.split(key, 48))
    L, D = DEPTH, D_MODEL

    def nrm(shape, scale):
        return jax.random.normal(next(ks), shape, jnp.float32) * scale

    ramp = (jnp.arange(RW_WIDTH, dtype=jnp.float32) / (RW_WIDTH - 1)) ** 0.85
    return {
        'x': nrm((BATCH, SEQ, D), 1.0),
        'w_in': nrm((L, D, C_IN), D ** -0.5),
        'rw_mu': jax.random.uniform(next(ks), (L, RW_COLS), jnp.float32),
        'rw_w0': -6.0 + 5.0 * ramp + nrm((L, RW_WIDTH), 0.1),
        'rw_w2': nrm((L, RW_DECAY_LORA, RW_WIDTH), 0.5 * RW_DECAY_LORA ** -0.5),
        'rw_a0': nrm((L, RW_WIDTH), 0.1),
        'rw_a2': nrm((L, RW_AAA_LORA, RW_WIDTH), RW_AAA_LORA ** -0.5),
        'rw_g2': nrm((L, RW_GATE_LORA, RW_WIDTH), RW_GATE_LORA ** -0.5),
        'rw_v0': nrm((L - 1, RW_WIDTH), 0.1),
        'rw_v1': nrm((L - 1, RW_WIDTH, RW_MV_LORA), RW_WIDTH ** -0.5),
        'rw_v2': nrm((L - 1, RW_MV_LORA, RW_WIDTH), RW_MV_LORA ** -0.5),
        'rw_k_k': 0.85 + nrm((L, RW_WIDTH), 0.05),
        'rw_k_a': 1.0 + nrm((L, RW_WIDTH), 0.05),
        'rw_r_k': nrm((L, RW_HEADS, RW_HEAD), 0.1),
        'rw_gn_g': 1.0 + nrm((L, RW_WIDTH), 0.02),
        'rw_gn_b': nrm((L, RW_WIDTH), 0.02),
        'gm_ln_g': 1.0 + nrm((L, GM_WIDTH), 0.02),
        'gm_ln_b': nrm((L, GM_WIDTH), 0.02),
        'gm_ws': nrm((L, GM_GROUPS, GM_CHUNK, GM_CHUNK), GM_CHUNK ** -0.5),
        'gm_bs': 1.0 + nrm((L, GM_GROUPS, GM_CHUNK), 0.02),
        'nsa_pos_k': nrm((L, CMP_BLK, NSA_DK), 0.02),
        'nsa_pos_v': nrm((L, CMP_BLK, NSA_DV), 0.02),
        'nsa_phi_k1': nrm((L, CMP_BLK * NSA_DK, NSA_DK), (CMP_BLK * NSA_DK) ** -0.5),
        'nsa_phi_k2': nrm((L, NSA_DK, NSA_DK), NSA_DK ** -0.5),
        'nsa_phi_v1': nrm((L, CMP_BLK * NSA_DV, NSA_DV), (CMP_BLK * NSA_DV) ** -0.5),
        'nsa_phi_v2': nrm((L, NSA_DV, NSA_DV), NSA_DV ** -0.5),
        'w_br': nrm((L, N_BRANCH, BRANCH_WIDTH, D), BRANCH_WIDTH ** -0.5),
        'w_o': nrm((L, D, D), BETA * D ** -0.5),
        'ffn1_wg': nrm((L, D, D_FF), D ** -0.5),
        'ffn1_wu': nrm((L, D, D_FF), D ** -0.5),
        'ffn1_wd': nrm((L, D_FF, D), BETA * D_FF ** -0.5),
        'ffn2_wg': nrm((L, D, D_FF), D ** -0.5),
        'ffn2_wu': nrm((L, D, D_FF), D ** -0.5),
        'ffn2_wd': nrm((L, D_FF, D), BETA * D_FF ** -0.5),
        'ln_g': 1.0 + nrm((L, 3, D), 0.02),
        'ln_b': nrm((L, 3, D), 0.02),
    }


def reference(x, w_in, rw_mu, rw_w0, rw_w2, rw_a0, rw_a2, rw_g2, rw_v0, rw_v1, rw_v2,
              rw_k_k, rw_k_a, rw_r_k, rw_gn_g, rw_gn_b, gm_ln_g, gm_ln_b, gm_ws, gm_bs,
              nsa_pos_k, nsa_pos_v, nsa_phi_k1, nsa_phi_k2, nsa_phi_v1, nsa_phi_v2,
              w_br, w_o, ffn1_wg, ffn1_wu, ffn1_wd, ffn2_wg, ffn2_wu, ffn2_wd, ln_g, ln_b):
    B, T, D = x.shape
    v_first = None
    for l in range(DEPTH):
        x = layer_norm(ALPHA * x + 0.5 * swiglu(x, ffn1_wg[l], ffn1_wu[l], ffn1_wd[l]), ln_g[l, 0], ln_b[l, 0])
        wl = w_in[l]
        vres = None if l == 0 else (rw_v0[l - 1], rw_v1[l - 1], rw_v2[l - 1])
        y_rw, v_first = rwkv7_mix(x @ wl[:, :OFF_GM], v_first, vres, rw_mu[l], rw_w0[l], rw_w2[l],
                                  rw_a0[l], rw_a2[l], rw_g2[l], rw_k_k[l], rw_k_a[l], rw_r_k[l],
                                  rw_gn_g[l], rw_gn_b[l])
        y_gm = gmlp_mix(x @ wl[:, OFF_GM:OFF_NSA], gm_ln_g[l], gm_ln_b[l], gm_ws[l], gm_bs[l])
        y_ns = nsa_mix(x @ wl[:, OFF_NSA:OFF_GATE], nsa_pos_k[l], nsa_pos_v[l], nsa_phi_k1[l],
                       nsa_phi_k2[l], nsa_phi_v1[l], nsa_phi_v2[l])
        gate = jax.nn.sigmoid(x @ wl[:, OFF_GATE:]).reshape(B, T, N_BRANCH, D)
        merged = (gate[:, :, 0] * (y_rw @ w_br[l, 0]) + gate[:, :, 1] * (y_gm @ w_br[l, 1])
                  + gate[:, :, 2] * (y_ns @ w_br[l, 2]))
        x = layer_norm(ALPHA * x + merged @ w_o[l], ln_g[l, 1], ln_b[l, 1])
        x = layer_norm(ALPHA * x + 0.5 * swiglu(x, ffn2_wg[l], ffn2_wu[l], ffn2_wd[l]), ln_g[l, 2], ln_b[l, 2])
    return x
```

```python
import functools

import jax
import jax.numpy as jnp
from jax import lax
from jax.experimental import pallas as pl
from jax.experimental.pallas import tpu as pltpu

F32 = jnp.float32
BF16 = jnp.bfloat16

D_MODEL = 2048
DEPTH = 2
RW_HEAD = 64
RW_WIDTH = D_MODEL
RW_DECAY_LORA = 96
RW_AAA_LORA = 96
RW_MV_LORA = 64
RW_GATE_LORA = 256
RW_LORA_COLS = RW_DECAY_LORA + RW_AAA_LORA + RW_GATE_LORA
RW_GN_EPS = RW_HEAD * 1e-5
GM_WIDTH = D_MODEL
GM_CHUNK = 128
GM_GROUP = 128
GM_GROUPS = GM_WIDTH // GM_GROUP
NSA_HEADS = 16
NSA_KV_GROUPS = 4
NSA_HPG = NSA_HEADS // NSA_KV_GROUPS
NSA_DK = 192
NSA_DV = 128
NSA_WIDTH = NSA_HEADS * NSA_DV
CMP_BLK = 32
CMP_STRIDE = 16
SEL_BLK = 64
N_SEL = 16
WIN = 512
D_FF = 5632
N_BRANCH = 3
ALPHA = (2 * DEPTH) ** 0.25
LN_EPS = 1e-5
NEG = -1e30
FORCED = 1e6

RW_COLS = 3 * RW_WIDTH + RW_LORA_COLS
GM_COLS = 2 * GM_WIDTH
NSA_Q_COLS = NSA_HEADS * NSA_DK
NSA_GK = NSA_KV_GROUPS * NSA_DK
NSA_GV = NSA_KV_GROUPS * NSA_DV
NSA_KV_COLS = 3 * (NSA_GK + NSA_GV)
NSA_G_COLS = 3 * NSA_HEADS
NSA_COLS = NSA_Q_COLS + NSA_KV_COLS + NSA_G_COLS
OFF_GM = RW_COLS
OFF_NSA = OFF_GM + GM_COLS
OFF_GATE = OFF_NSA + NSA_COLS

V7X_VMEM_BYTES = 64 * 1024 * 1024
VMEM_LIMIT = 56 * 1024 * 1024

FFN_TM = 512
FFN_TF = 512
MM_TM = 1024
MM_TN = 512
MERGE_TM = 512
MERGE_TN = 512
WO_TM = 512
PREP_TM = 128
WKV_L = 64
WKV_TC = 256
GM_TM = 256
NSA_TQ = 128
NSA_TK = 512

NN = (((1,), (0,)), ((), ()))
NT = (((1,), (1,)), ((), ()))
TN = (((0,), (0,)), ((), ()))


def _cparams(sem):
    return pltpu.CompilerParams(dimension_semantics=sem, vmem_limit_bytes=VMEM_LIMIT)


def _dg(a, b, dims):
    return lax.dot_general(a, b, dims, preferred_element_type=F32)


def _split2(x):
    hi = x.astype(BF16)
    lo = (x - hi.astype(F32)).astype(BF16)
    return hi, lo


def _split3(x):
    x1 = x.astype(BF16)
    r1 = x - x1.astype(F32)
    x2 = r1.astype(BF16)
    x3 = (r1 - x2.astype(F32)).astype(BF16)
    return x1, x2, x3


def _dot3(a, b, dims=NN):
    ah, al = _split2(a)
    bh, bl = _split2(b)
    return _dg(ah, bh, dims) + (_dg(ah, bl, dims) + _dg(al, bh, dims))


def _dot_exact_rhs(a, m01, dims=NN):
    a1, a2, a3 = _split3(a)
    return _dg(a1, m01, dims) + (_dg(a2, m01, dims) + _dg(a3, m01, dims))


def _dot_exact_lhs(m01, b, dims=NN):
    b1, b2, b3 = _split3(b)
    return _dg(m01, b1, dims) + (_dg(m01, b2, dims) + _dg(m01, b3, dims))


def _layer_norm(y, g, b):
    mu = jnp.mean(y, axis=-1, keepdims=True)
    d = y - mu
    var = jnp.mean(d * d, axis=-1, keepdims=True)
    return d * lax.rsqrt(var + LN_EPS) * g + b


def _gelu_tanh(x):
    return 0.5 * x * (1.0 + jnp.tanh(0.7978845608028654 * (x + 0.044715 * (x * x * x))))


def _sigmoid(x):
    return 1.0 / (1.0 + jnp.exp(-x))


def _ffn_kernel(x_ref, wg_ref, wu_ref, wd_ref, g_ref, b_ref, o_ref, acc_ref, xb_ref):
    j = pl.program_id(1)

    @pl.when(j == 0)
    def _():
        acc_ref[...] = jnp.zeros_like(acc_ref)
        xb_ref[...] = x_ref[...].astype(BF16)

    xb = xb_ref[...]
    hg = _dg(xb, wg_ref[...], NN)
    hu = _dg(xb, wu_ref[...], NN)
    h = (hg * _sigmoid(hg) * hu).astype(BF16)
    acc_ref[...] += _dg(h, wd_ref[...], NN)

    @pl.when(j == pl.num_programs(1) - 1)
    def _():
        y = ALPHA * x_ref[...] + 0.5 * acc_ref[...]
        o_ref[...] = _layer_norm(y, g_ref[...], b_ref[...])


def _ffn(x, wg, wu, wd, g, b):
    m, d = x.shape
    ff = wg.shape[1]
    tm = min(FFN_TM, m)
    return pl.pallas_call(
        _ffn_kernel,
        grid=(m // tm, ff // FFN_TF),
        in_specs=[
            pl.BlockSpec((tm, d), lambda i, j: (i, 0)),
            pl.BlockSpec((d, FFN_TF), lambda i, j: (0, j)),
            pl.BlockSpec((d, FFN_TF), lambda i, j: (0, j)),
            pl.BlockSpec((FFN_TF, d), lambda i, j: (j, 0)),
            pl.BlockSpec((1, d), lambda i, j: (0, 0)),
            pl.BlockSpec((1, d), lambda i, j: (0, 0)),
        ],
        out_specs=pl.BlockSpec((tm, d), lambda i, j: (i, 0)),
        out_shape=jax.ShapeDtypeStruct((m, d), F32),
        scratch_shapes=[pltpu.VMEM((tm, d), F32), pltpu.VMEM((tm, d), BF16)],
        compiler_params=_cparams(("parallel", "arbitrary")),
        name="ffn_swiglu_ln",
    )(x, wg, wu, wd, g.reshape(1, d), b.reshape(1, d))


def _mm_kernel(x_ref, w_ref, o_ref):
    o_ref[...] = _dg(x_ref[...], w_ref[...], NN).astype(o_ref.dtype)


def _mm(x, w, out_dtype, tn=MM_TN):
    m, k = x.shape
    n = w.shape[1]
    tm = min(MM_TM, m)
    tn = min(tn, n)
    return pl.pallas_call(
        _mm_kernel,
        grid=(m // tm, n // tn),
        in_specs=[
            pl.BlockSpec((tm, k), lambda i, j: (i, 0)),
            pl.BlockSpec((k, tn), lambda i, j: (0, j)),
        ],
        out_specs=pl.BlockSpec((tm, tn), lambda i, j: (i, j)),
        out_shape=jax.ShapeDtypeStruct((m, n), out_dtype),
        compiler_params=_cparams(("parallel", "arbitrary")),
        name="proj_mm",
    )(x, w)


def _mm_grouped(x, w3, out_dtype):
    m, k = x.shape
    nj, _, n = w3.shape
    tm = min(MM_TM, m)
    return pl.pallas_call(
        _mm_kernel,
        grid=(m // tm, nj),
        in_specs=[
            pl.BlockSpec((tm, k), lambda i, j: (i, 0)),
            pl.BlockSpec((None, k, n), lambda i, j: (j, 0, 0)),
        ],
        out_specs=pl.BlockSpec((None, tm, n), lambda i, j: (j, i, 0)),
        out_shape=jax.ShapeDtypeStruct((nj, m, n), out_dtype),
        compiler_params=_cparams(("parallel", "arbitrary")),
        name="proj_mm_grouped",
    )(x, w3)


def _token_shift(x, prev_row, mu, first):
    rows = lax.broadcasted_iota(jnp.int32, x.shape, 0)
    prev_row = jnp.where(first, jnp.zeros_like(prev_row), prev_row)
    shifted = jnp.where(rows == 0, prev_row, pltpu.roll(x, 1, 0))
    return x + (shifted - x) * mu


def _rw_prep_kernel(seq_len, has_vres, *refs):
    if has_vres:
        (p_ref, pp_ref, q_ref, qp_ref, mu_ref, mul_ref, w0_ref, a0_ref, w2_ref, a2_ref, g2_ref,
         vf_ref, v0_ref, v1_ref, v2_ref, r_out, k_out, v_out, w_out, a_out, g_out) = refs
    else:
        (p_ref, pp_ref, q_ref, qp_ref, mu_ref, mul_ref, w0_ref, a0_ref, w2_ref, a2_ref, g2_ref,
         r_out, k_out, v_out, w_out, a_out, g_out) = refs
    tm = p_ref.shape[0]
    first = (pl.program_id(0) * tm) % seq_len == 0
    wd = RW_WIDTH

    def shifted(c0, c1):
        return _token_shift(p_ref[:, c0:c1], pp_ref[7:8, c0:c1], mu_ref[:, c0:c1], first)

    r_out[...] = shifted(0, wd)
    k_out[...] = shifted(wd, 2 * wd)
    v = shifted(2 * wd, 3 * wd)
    lo = _token_shift(q_ref[...], qp_ref[7:8, :], mul_ref[...], first)
    wl = lo[:, :RW_DECAY_LORA]
    al = lo[:, RW_DECAY_LORA:RW_DECAY_LORA + RW_AAA_LORA]
    gl = lo[:, RW_DECAY_LORA + RW_AAA_LORA:]
    z = w0_ref[...] + _dot3(jnp.tanh(wl), w2_ref[...])
    w_out[...] = -(jnp.maximum(-z, 0.0) + jnp.log1p(jnp.exp(-jnp.abs(z)))) - 0.5
    a_out[...] = _sigmoid(a0_ref[...] + _dot3(al, a2_ref[...]))
    g_out[...] = _dot3(_sigmoid(gl), g2_ref[...])
    if has_vres:
        mix = _sigmoid(v0_ref[...] + _dot3(_dot3(v, v1_ref[...]), v2_ref[...]))
        v = v + (vf_ref[...] - v) * mix
    v_out[...] = v


def _rw_prep(p_rkv, p_lora, mu, w0, a0, w2, a2, g2, vres, seq_len):
    m = p_rkv.shape[0]
    tm = PREP_TM
    wd = RW_WIDTH
    c3 = 3 * wd
    row = lambda i: (i, 0)
    prev = lambda i: (jnp.maximum(i * (tm // 8) - 1, 0), 0)
    fix = lambda i: (0, 0)
    in_specs = [
        pl.BlockSpec((tm, c3), row), pl.BlockSpec((8, c3), prev),
        pl.BlockSpec((tm, RW_LORA_COLS), row), pl.BlockSpec((8, RW_LORA_COLS), prev),
        pl.BlockSpec((1, c3), fix), pl.BlockSpec((1, RW_LORA_COLS), fix),
        pl.BlockSpec((1, wd), fix), pl.BlockSpec((1, wd), fix),
        pl.BlockSpec((RW_DECAY_LORA, wd), fix), pl.BlockSpec((RW_AAA_LORA, wd), fix),
        pl.BlockSpec((RW_GATE_LORA, wd), fix),
    ]
    args = [p_rkv, p_rkv, p_lora, p_lora, mu[:c3].reshape(1, c3), mu[c3:].reshape(1, RW_LORA_COLS),
            w0.reshape(1, wd), a0.reshape(1, wd), w2, a2, g2]
    if vres is not None:
        v_first, v0, v1, v2 = vres
        in_specs += [pl.BlockSpec((tm, wd), row), pl.BlockSpec((1, wd), fix),
                     pl.BlockSpec((wd, RW_MV_LORA), fix), pl.BlockSpec((RW_MV_LORA, wd), fix)]
        args += [v_first, v0.reshape(1, wd), v1, v2]
    out = jax.ShapeDtypeStruct((m, wd), F32)
    return pl.pallas_call(
        functools.partial(_rw_prep_kernel, seq_len, vres is not None),
        grid=(m // tm,),
        in_specs=in_specs,
        out_specs=[pl.BlockSpec((tm, wd), row)] * 6,
        out_shape=[out] * 6,
        compiler_params=_cparams(("parallel",)),
        name="rwkv_prep",
    )(*args)


def _wkv_kernel(r_ref, k_ref, v_ref, w_ref, a_ref, g_ref, kk_ref, ka_ref, rk_ref, gg_ref, gb_ref,
                o_ref, s_ref):
    L = WKV_L
    n_chunks = r_ref.shape[0] // L

    @pl.when(pl.program_id(2) == 0)
    def _():
        s_ref[...] = jnp.zeros_like(s_ref)

    ri = lax.broadcasted_iota(jnp.int32, (2 * L, 2 * L), 0)
    ci = lax.broadcasted_iota(jnp.int32, (2 * L, 2 * L), 1)
    bd_f = ((ri // L) == (ci // L)).astype(F32)
    bd_b = bd_f.astype(BF16)
    rl = lax.broadcasted_iota(jnp.int32, (L, L), 0)
    cl = lax.broadcasted_iota(jnp.int32, (L, L), 1)
    tri_b = (cl <= rl).astype(BF16)
    rt_i = lax.broadcasted_iota(jnp.int32, (L, 2 * L), 0)
    cs_i = lax.broadcasted_iota(jnp.int32, (L, 2 * L), 1) % L
    strict = cs_i < rt_i
    incl = cs_i <= rt_i

    def sm(z):
        return jnp.concatenate([z, z], axis=0) * bd_f

    k_k = kk_ref[...]
    k_a = ka_ref[...]
    r_k = rk_ref[...]
    gn_g = gg_ref[...]
    gn_b = gb_ref[...]
    inv_n = 1.0 / RW_HEAD

    state = s_ref[...]
    for c in range(n_chunks):
        rows = slice(c * L, (c + 1) * L)
        r = r_ref[rows, :]
        k_raw = k_ref[rows, :]
        v = v_ref[rows, :]
        a = a_ref[rows, :]
        kk = k_raw * k_k
        ssq = _dot_exact_rhs(kk * kk, bd_b)
        kk = kk * lax.rsqrt(jnp.maximum(ssq, 1e-24))
        k = k_raw * (1.0 + (a - 1.0) * k_a)
        lw = -jnp.exp(w_ref[rows, :])
        bv = kk * a
        cum = _dot_exact_lhs(tri_b, lw)
        tot = cum[L - 1:L, :]
        e_neg = jnp.exp(-cum)
        rt = r * jnp.exp(cum)
        at = -kk * jnp.exp(cum - lw)
        kt = k * e_neg
        bt = bv * e_neg
        e_tot = jnp.exp(tot - cum)
        kh = k * e_tot
        bh = bv * e_tot

        lhs2 = jnp.concatenate([at, rt], axis=0)
        ak = _dot3(lhs2, sm(kt), NT)
        ab = _dot3(lhs2, sm(bt), NT)
        a_ak = jnp.where(strict, ak[:L], 0.0)
        a_rk = jnp.where(incl, ak[L:], 0.0)
        a_ab = jnp.where(strict, ab[:L], 0.0)
        a_rb = jnp.where(incl, ab[L:], 0.0)
        smv = sm(v)
        u = _dot3(at, state, NT) + _dot3(a_ak, smv)
        p = a_ab
        n_lvl = L.bit_length() - 1
        for lvl in range(n_lvl):
            u = u + _dot3(p, sm(u))
            if lvl + 1 < n_lvl:
                p = _dot3(p, sm(p))
        y = _dot3(rt, state, NT) + _dot3(a_rk, smv) + _dot3(a_rb, sm(u))
        state = state * jnp.exp(tot) + bd_f * (_dot3(v, kh, TN) + _dot3(u, bh, TN))

        mean = _dot_exact_rhs(y, bd_b) * inv_n
        d = y - mean
        var = _dot_exact_rhs(d * d, bd_b) * inv_n
        yn = d * lax.rsqrt(var + RW_GN_EPS) * gn_g + gn_b
        bonus = _dot_exact_rhs(r * k * r_k, bd_b) * v
        o_ref[rows, :] = ((yn + bonus) * g_ref[rows, :]).astype(o_ref.dtype)
    s_ref[...] = state


def _wkv(r, k, v, w, a, g, k_k, k_a, r_k, gn_g, gn_b, batch, seq_len):
    m, wd = r.shape
    tc = min(WKV_TC, seq_len)
    nt = seq_len // tc
    lanes = 2 * RW_HEAD
    act = pl.BlockSpec((tc, lanes), lambda b, h, t: (b * nt + t, h))
    par = pl.BlockSpec((1, lanes), lambda b, h, t: (0, h))
    params = [z.reshape(1, wd) for z in (k_k, k_a, r_k, gn_g, gn_b)]
    return pl.pallas_call(
        _wkv_kernel,
        grid=(batch, wd // lanes, nt),
        in_specs=[act] * 6 + [par] * 5,
        out_specs=act,
        out_shape=jax.ShapeDtypeStruct((m, wd), BF16),
        scratch_shapes=[pltpu.VMEM((lanes, lanes), F32)],
        compiler_params=_cparams(("parallel", "parallel", "arbitrary")),
        name="rwkv_wkv",
    )(r, k, v, w, a, g, *params)


def _gmlp_kernel(p_ref, lg_ref, lb_ref, ws_ref, bst_ref, o_ref):
    tm = p_ref.shape[0]
    u = _gelu_tanh(p_ref[:, :GM_WIDTH])
    v = _layer_norm(_gelu_tanh(p_ref[:, GM_WIDTH:]), lg_ref[...], lb_ref[...])
    rl = lax.broadcasted_iota(jnp.int32, (GM_CHUNK, GM_CHUNK), 0)
    cl = lax.broadcasted_iota(jnp.int32, (GM_CHUNK, GM_CHUNK), 1)
    causal = cl <= rl
    for g in range(GM_GROUPS):
        cols = slice(g * GM_GROUP, (g + 1) * GM_GROUP)
        wsg = jnp.where(causal, ws_ref[g], 0.0)
        bias = bst_ref[:, g:g + 1]
        for c in range(tm // GM_CHUNK):
            rows = slice(c * GM_CHUNK, (c + 1) * GM_CHUNK)
            s = _dot3(wsg, v[rows, cols]) + bias
            o_ref[rows, cols] = (u[rows, cols] * s).astype(o_ref.dtype)


def _gmlp(p_gm, ln_g, ln_b, ws, bs):
    m = p_gm.shape[0]
    tm = GM_TM
    fix2 = lambda i: (0, 0)
    return pl.pallas_call(
        _gmlp_kernel,
        grid=(m // tm,),
        in_specs=[
            pl.BlockSpec((tm, GM_COLS), lambda i: (i, 0)),
            pl.BlockSpec((1, GM_WIDTH), fix2),
            pl.BlockSpec((1, GM_WIDTH), fix2),
            pl.BlockSpec((GM_GROUPS, GM_CHUNK, GM_CHUNK), lambda i: (0, 0, 0)),
            pl.BlockSpec((GM_CHUNK, GM_GROUPS), fix2),
        ],
        out_specs=pl.BlockSpec((tm, GM_WIDTH), lambda i: (i, 0)),
        out_shape=jax.ShapeDtypeStruct((m, GM_WIDTH), BF16),
        compiler_params=_cparams(("parallel",)),
        name="gmlp_mix",
    )(p_gm, ln_g.reshape(1, -1), ln_b.reshape(1, -1), ws, bs.T)


def _compress_kernel(c_ref, pos_ref, w1_ref, w2_ref, o_ref):
    c = c_ref[...]
    n = c.shape[0]
    r1 = _dot3(c, w1_ref[0])
    r2 = _dot3(c, w1_ref[1])
    pos = jnp.broadcast_to(pos_ref[...], (8, pos_ref.shape[1]))
    half = w1_ref.shape[1]
    pterm = (_dot3(pos[:, :half], w1_ref[0]) + _dot3(pos[:, half:], w1_ref[1]))[0:1, :]
    h = r1 + pltpu.roll(r2, n - 1, 0) + pterm
    o_ref[...] = _dot3(_gelu_tanh(h), w2_ref[...])


def _compress(z, pos, w1, w2):
    gb, t, d = z.shape
    nb = t // CMP_STRIDE
    c = z.reshape(gb, nb, CMP_STRIDE * d)
    half = CMP_STRIDE * d
    return pl.pallas_call(
        _compress_kernel,
        grid=(gb,),
        in_specs=[
            pl.BlockSpec((None, nb, half), lambda i: (i, 0, 0)),
            pl.BlockSpec((1, 2 * half), lambda i: (0, 0)),
            pl.BlockSpec((2, half, d), lambda i: (0, 0, 0)),
            pl.BlockSpec((d, d), lambda i: (0, 0)),
        ],
        out_specs=pl.BlockSpec((None, nb, d), lambda i: (i, 0, 0)),
        out_shape=jax.ShapeDtypeStruct((gb, nb, d), F32),
        compiler_params=_cparams(("parallel",)),
        name="nsa_compress",
    )(c, pos.reshape(1, 2 * half), w1.reshape(2, half, d), w2)


def _nsa_kernel(seq_len, q_ref, gt_ref, kc_ref, vc_ref, ks_ref, vs_ref, kw_ref, vw_ref,
                o_ref, selm_ref):
    tq = NSA_TQ
    hp = NSA_HPG
    tk = min(NSA_TK, seq_len)
    n_s = seq_len // SEL_BLK
    k_sel = min(N_SEL, n_s)
    n_cb = kc_ref.shape[0]
    i = pl.program_id(2)
    t0 = i * tq
    scale = NSA_DK ** -0.5

    q = q_ref[...]
    qa = jnp.concatenate(
        [(q[:, h * NSA_DK:(h + 1) * NSA_DK] * scale).astype(BF16) for h in range(hp)], axis=0)
    t_q = t0 + lax.broadcasted_iota(jnp.int32, (tq, 1), 0)
    t_row = t0 + (lax.broadcasted_iota(jnp.int32, (hp * tq, 1), 0) & (tq - 1))

    s_c = _dg(qa, kc_ref[...].astype(BF16), NT)
    n_end = lax.broadcasted_iota(jnp.int32, (1, n_cb), 1) * CMP_STRIDE + (CMP_BLK - 1)
    m_c = (n_end <= t_row) & (n_end < seq_len)
    s_c = jnp.where(m_c, s_c, NEG)
    e_c = jnp.where(m_c, jnp.exp(s_c - jnp.max(s_c, axis=-1, keepdims=True)), 0.0)
    p_c = e_c / jnp.maximum(jnp.sum(e_c, axis=-1, keepdims=True), 1e-30)
    o_c = _dg(p_c.astype(BF16), vc_ref[...].astype(BF16), NN)

    p_sum = p_c[0:tq]
    for h in range(1, hp):
        p_sum = p_sum + p_c[h * tq:(h + 1) * tq]
    cs = lax.broadcasted_iota(jnp.int32, (n_cb, n_s), 0) * CMP_STRIDE
    ss = lax.broadcasted_iota(jnp.int32, (n_cb, n_s), 1) * SEL_BLK
    overlap = ((cs < ss + SEL_BLK) & (cs + (CMP_BLK - 1) >= ss)
               & (cs + (CMP_BLK - 1) < seq_len)).astype(BF16)
    imp = _dot_exact_rhs(p_sum, overlap)
    blk = lax.broadcasted_iota(jnp.int32, (1, n_s), 1)
    cur = t_q // SEL_BLK
    valid = blk * SEL_BLK <= t_q
    forced = valid & ((blk == 0) | (blk == cur) | (blk == cur - 1))
    score = jnp.where(forced, FORCED, jnp.where(valid, imp, NEG))
    rank = jnp.zeros((tq, n_s), jnp.int32)
    for s in range(n_s):
        col = score[:, s:s + 1]
        beats = (col > score) | ((col == score) & (blk > s))
        rank = rank + beats.astype(jnp.int32)
    sel = ((rank < k_sel) & (score > 0.5 * NEG)).astype(BF16)
    for kt in range(seq_len // tk):
        er = lax.broadcasted_iota(jnp.int32, (n_s, tk), 0)
        ec = (lax.broadcasted_iota(jnp.int32, (n_s, tk), 1) + kt * tk) // SEL_BLK
        selm_ref[kt] = _dg(sel, (er == ec).astype(BF16), NN)

    def online(s, msk, v_tile, carry):
        m_run, l_run, acc = carry
        s = jnp.where(msk, s, NEG)
        m_new = jnp.maximum(m_run, jnp.max(s, axis=-1, keepdims=True))
        alpha = jnp.exp(m_run - m_new)
        p = jnp.where(msk, jnp.exp(s - m_new), 0.0)
        l_new = alpha * l_run + jnp.sum(p, axis=-1, keepdims=True)
        acc = alpha * acc + _dg(p.astype(BF16), v_tile, NN)
        return m_new, l_new, acc

    init = (jnp.full((hp * tq, 1), NEG, F32), jnp.zeros((hp * tq, 1), F32),
            jnp.zeros((hp * tq, NSA_DV), F32))

    def sel_body(kt, carry):
        k0 = pl.multiple_of(kt * tk, tk)
        s = _dg(qa, ks_ref[pl.ds(k0, tk), :], NT)
        kpos = k0 + lax.broadcasted_iota(jnp.int32, (1, tk), 1)
        chosen = selm_ref[kt] > 0.5
        msk = jnp.concatenate([chosen] * hp, axis=0) & (kpos <= t_row)
        return online(s, msk, vs_ref[pl.ds(k0, tk), :], carry)

    n_kt = (t0 + tq + tk - 1) // tk
    _, l_s, acc_s = lax.fori_loop(0, n_kt, sel_body, init)
    o_s = acc_s / jnp.maximum(l_s, 1e-30)

    def win_body(kt, carry):
        k0 = pl.multiple_of(kt * tq, tq)
        s = _dg(qa, kw_ref[pl.ds(k0, tq), :], NT)
        kpos = k0 + lax.broadcasted_iota(jnp.int32, (1, tq), 1)
        msk = (kpos <= t_row) & (kpos > t_row - WIN)
        return online(s, msk, vw_ref[pl.ds(k0, tq), :], carry)

    _, l_w, acc_w = lax.fori_loop(jnp.maximum(i - WIN // tq, 0), i + 1, win_body, init)
    o_w = acc_w / jnp.maximum(l_w, 1e-30)

    gates = _sigmoid(gt_ref[...])
    for h in range(hp):
        rows = slice(h * tq, (h + 1) * tq)
        out = (gates[:, 3 * h:3 * h + 1] * o_c[rows] + gates[:, 3 * h + 1:3 * h + 2] * o_s[rows]
               + gates[:, 3 * h + 2:3 * h + 3] * o_w[rows])
        o_ref[:, h * NSA_DV:(h + 1) * NSA_DV] = out.astype(o_ref.dtype)


def _nsa_attention(p_q, p_g, k_cmp, v_cmp, ks, vs, kw, vw, batch, seq_len):
    m = p_q.shape[0]
    tq = NSA_TQ
    nq = seq_len // tq
    n_cb = k_cmp.shape[1]
    tk = min(NSA_TK, seq_len)
    gq = NSA_HPG * NSA_DK
    kv = lambda b, g, i: (g * batch + b, 0, 0)
    return pl.pallas_call(
        functools.partial(_nsa_kernel, seq_len),
        grid=(batch, NSA_KV_GROUPS, nq),
        in_specs=[
            pl.BlockSpec((tq, gq), lambda b, g, i: (b * nq + i, g)),
            pl.BlockSpec((None, tq, 3 * NSA_HPG), lambda b, g, i: (g, b * nq + i, 0)),
            pl.BlockSpec((None, n_cb, NSA_DK), kv),
            pl.BlockSpec((None, n_cb, NSA_DV), kv),
            pl.BlockSpec((None, seq_len, NSA_DK), kv),
            pl.BlockSpec((None, seq_len, NSA_DV), kv),
            pl.BlockSpec((None, seq_len, NSA_DK), kv),
            pl.BlockSpec((None, seq_len, NSA_DV), kv),
        ],
        out_specs=pl.BlockSpec((tq, NSA_HPG * NSA_DV), lambda b, g, i: (b * nq + i, g)),
        out_shape=jax.ShapeDtypeStruct((m, NSA_WIDTH), BF16),
        scratch_shapes=[pltpu.VMEM((seq_len // tk, tq, tk), F32)],
        compiler_params=_cparams(("parallel", "parallel", "arbitrary")),
        name="nsa_attention",
    )(p_q, p_g, k_cmp, v_cmp, ks, vs, kw, vw)


def _merge_kernel(y0_ref, y1_ref, y2_ref, g0_ref, g1_ref, g2_ref, w_ref, o_ref):
    acc = _sigmoid(g0_ref[...]) * _dg(y0_ref[...], w_ref[0], NN)
    acc = acc + _sigmoid(g1_ref[...]) * _dg(y1_ref[...], w_ref[1], NN)
    acc = acc + _sigmoid(g2_ref[...]) * _dg(y2_ref[...], w_ref[2], NN)
    o_ref[...] = acc.astype(o_ref.dtype)


def _merge(y_rw, y_gm, y_ns, p_gate, w_br):
    m, d = y_rw.shape
    tm = min(MERGE_TM, m)
    tn = MERGE_TN
    nj = d // tn
    ys = pl.BlockSpec((tm, d), lambda i, j: (i, 0))
    gspec = lambda br: pl.BlockSpec((tm, tn), lambda i, j: (i, br * nj + j))
    return pl.pallas_call(
        _merge_kernel,
        grid=(m // tm, nj),
        in_specs=[ys, ys, ys, gspec(0), gspec(1), gspec(2),
                  pl.BlockSpec((N_BRANCH, d, tn), lambda i, j: (0, 0, j))],
        out_specs=pl.BlockSpec((tm, tn), lambda i, j: (i, j)),
        out_shape=jax.ShapeDtypeStruct((m, d), BF16),
        compiler_params=_cparams(("parallel", "arbitrary")),
        name="branch_merge",
    )(y_rw, y_gm, y_ns, p_gate, p_gate, p_gate, w_br)


def _out_ln_kernel(y_ref, w_ref, x_ref, g_ref, b_ref, o_ref):
    y = ALPHA * x_ref[...] + _dg(y_ref[...], w_ref[...], NN)
    o_ref[...] = _layer_norm(y, g_ref[...], b_ref[...])


def _out_ln(merged, w_o, x, g, b):
    m, d = x.shape
    tm = min(WO_TM, m)
    row = lambda i: (i, 0)
    fix = lambda i: (0, 0)
    return pl.pallas_call(
        _out_ln_kernel,
        grid=(m // tm,),
        in_specs=[pl.BlockSpec((tm, d), row), pl.BlockSpec((d, d), fix), pl.BlockSpec((tm, d), row),
                  pl.BlockSpec((1, d), fix), pl.BlockSpec((1, d), fix)],
        out_specs=pl.BlockSpec((tm, d), row),
        out_shape=jax.ShapeDtypeStruct((m, d), F32),
        compiler_params=_cparams(("parallel",)),
        name="out_proj_ln",
    )(merged, w_o, x, g.reshape(1, d), b.reshape(1, d))


def _rwkv_branch(xb, wl, mu, w0, w2, a0, a2, g2, k_k, k_a, r_k, gn_g, gn_b, vres, batch, seq_len):
    p_rkv = _mm(xb, wl[:, :3 * RW_WIDTH].astype(BF16), F32)
    p_lora = _mm(xb, wl[:, 3 * RW_WIDTH:RW_COLS].astype(BF16), F32, tn=RW_LORA_COLS)
    r, k, v, w, a, g = _rw_prep(p_rkv, p_lora, mu, w0, a0, w2, a2, g2, vres, seq_len)
    y = _wkv(r, k, v, w, a, g, k_k, k_a, r_k.reshape(-1), gn_g, gn_b, batch, seq_len)
    return y, v


def _nsa_branch(xb, wl, pos_k, pos_v, phi_k1, phi_k2, phi_v1, phi_v2, batch, seq_len):
    m = xb.shape[0]
    G = NSA_KV_GROUPS
    o = OFF_NSA
    p_q = _mm(xb, wl[:, o:o + NSA_Q_COLS].astype(BF16), F32)
    o += NSA_Q_COLS

    def grouped(width, d, dtype):
        nonlocal o
        w3 = wl[:, o:o + width].astype(BF16).reshape(D_MODEL, G, d).transpose(1, 0, 2)
        o += width
        return _mm_grouped(xb, w3, dtype).reshape(G * batch, seq_len, d)

    kc = grouped(NSA_GK, NSA_DK, F32)
    vc = grouped(NSA_GV, NSA_DV, F32)
    ks = grouped(NSA_GK, NSA_DK, BF16)
    vs = grouped(NSA_GV, NSA_DV, BF16)
    kw = grouped(NSA_GK, NSA_DK, BF16)
    vw = grouped(NSA_GV, NSA_DV, BF16)
    wg3 = wl[:, o:o + NSA_G_COLS].astype(BF16).reshape(D_MODEL, G, 3 * NSA_HPG).transpose(1, 0, 2)
    p_g = _mm_grouped(xb, wg3, F32)
    k_cmp = _compress(kc, pos_k, phi_k1, phi_k2)
    v_cmp = _compress(vc, pos_v, phi_v1, phi_v2)
    return _nsa_attention(p_q, p_g, k_cmp, v_cmp, ks, vs, kw, vw, batch, seq_len)


def kernel(x, w_in, rw_mu, rw_w0, rw_w2, rw_a0, rw_a2, rw_g2, rw_v0, rw_v1, rw_v2, rw_k_k, rw_k_a, rw_r_k, rw_gn_g, rw_gn_b, gm_ln_g, gm_ln_b, gm_ws, gm_bs, nsa_pos_k, nsa_pos_v, nsa_phi_k1, nsa_phi_k2, nsa_phi_v1, nsa_phi_v2, w_br, w_o, ffn1_wg, ffn1_wu, ffn1_wd, ffn2_wg, ffn2_wu, ffn2_wd, ln_g, ln_b):
    batch, seq_len, d = x.shape
    m = batch * seq_len
    h = x.reshape(m, d)
    v_first = None
    for l in range(DEPTH):
        h = _ffn(h, ffn1_wg[l].astype(BF16), ffn1_wu[l].astype(BF16), ffn1_wd[l].astype(BF16),
                 ln_g[l, 0], ln_b[l, 0])
        hb = h.astype(BF16)
        wl = w_in[l]
        vres = None if l == 0 else (v_first, rw_v0[l - 1], rw_v1[l - 1], rw_v2[l - 1])
        y_rw, v_out = _rwkv_branch(hb, wl, rw_mu[l], rw_w0[l], rw_w2[l], rw_a0[l], rw_a2[l], rw_g2[l],
                                   rw_k_k[l], rw_k_a[l], rw_r_k[l], rw_gn_g[l], rw_gn_b[l], vres,
                                   batch, seq_len)
        if l == 0:
            v_first = v_out
        p_gm = _mm(hb, wl[:, OFF_GM:OFF_NSA].astype(BF16), F32)
        y_gm = _gmlp(p_gm, gm_ln_g[l], gm_ln_b[l], gm_ws[l], gm_bs[l])
        y_ns = _nsa_branch(hb, wl, nsa_pos_k[l], nsa_pos_v[l], nsa_phi_k1[l], nsa_phi_k2[l],
                           nsa_phi_v1[l], nsa_phi_v2[l], batch, seq_len)
        p_gate = _mm(hb, wl[:, OFF_GATE:].astype(BF16), F32)
        merged = _merge(y_rw, y_gm, y_ns, p_gate, w_br[l].astype(BF16))
        h = _out_ln(merged, w_o[l].astype(BF16), h, ln_g[l, 1], ln_b[l, 1])
        h = _ffn(h, ffn2_wg[l].astype(BF16), ffn2_wu[l].astype(BF16), ffn2_wd[l].astype(BF16),
                 ln_g[l, 2], ln_b[l, 2])
    return h.reshape(batch, seq_len, d)
```

```python
import functools

import jax
import jax.numpy as jnp
from jax import lax
from jax.experimental import pallas as pl
from jax.experimental.pallas import tpu as pltpu

F32 = jnp.float32
BF16 = jnp.bfloat16

D_MODEL = 2048
DEPTH = 2
RW_HEAD = 64
RW_WIDTH = D_MODEL
RW_DECAY_LORA = 96
RW_AAA_LORA = 96
RW_MV_LORA = 64
RW_GATE_LORA = 256
RW_LORA_COLS = RW_DECAY_LORA + RW_AAA_LORA + RW_GATE_LORA
RW_GN_EPS = RW_HEAD * 1e-5
GM_WIDTH = D_MODEL
GM_CHUNK = 128
GM_GROUP = 128
GM_GROUPS = GM_WIDTH // GM_GROUP
NSA_HEADS = 16
NSA_KV_GROUPS = 4
NSA_HPG = NSA_HEADS // NSA_KV_GROUPS
NSA_DK = 192
NSA_DV = 128
NSA_WIDTH = NSA_HEADS * NSA_DV
CMP_BLK = 32
CMP_STRIDE = 16
SEL_BLK = 64
N_SEL = 16
WIN = 512
D_FF = 5632
N_BRANCH = 3
ALPHA = (2 * DEPTH) ** 0.25
LN_EPS = 1e-5
NEG = -1e30
FORCED = 1e6

RW_COLS = 3 * RW_WIDTH + RW_LORA_COLS
GM_COLS = 2 * GM_WIDTH
NSA_Q_COLS = NSA_HEADS * NSA_DK
NSA_GK = NSA_KV_GROUPS * NSA_DK
NSA_GV = NSA_KV_GROUPS * NSA_DV
NSA_KV_COLS = 3 * (NSA_GK + NSA_GV)
NSA_G_COLS = 3 * NSA_HEADS
NSA_COLS = NSA_Q_COLS + NSA_KV_COLS + NSA_G_COLS
OFF_GM = RW_COLS
OFF_NSA = OFF_GM + GM_COLS
OFF_GATE = OFF_NSA + NSA_COLS

V7X_VMEM_BYTES = 64 * 1024 * 1024
VMEM_LIMIT = 56 * 1024 * 1024

FFN_TM = 512
FFN_TF = 512
MM_TM = 1024
MM_TN = 512
MERGE_TM = 512
MERGE_TN = 512
WO_TM = 512
PREP_TM = 128
WKV_L = 64
WKV_TC = 512
GM_TM = 256
NSA_TQ = 128
NSA_TK = 512

NN = (((1,), (0,)), ((), ()))
NT = (((1,), (1,)), ((), ()))
TN = (((0,), (0,)), ((), ()))


def _cparams(sem):
    return pltpu.CompilerParams(dimension_semantics=sem, vmem_limit_bytes=VMEM_LIMIT)


def _dg(a, b, dims):
    return lax.dot_general(a, b, dims, preferred_element_type=F32)


def _split2(x):
    hi = x.astype(BF16)
    lo = (x - hi.astype(F32)).astype(BF16)
    return hi, lo


def _split3(x):
    x1 = x.astype(BF16)
    r1 = x - x1.astype(F32)
    x2 = r1.astype(BF16)
    x3 = (r1 - x2.astype(F32)).astype(BF16)
    return x1, x2, x3


def _dot3(a, b, dims=NN):
    ah, al = _split2(a)
    bh, bl = _split2(b)
    return _dg(ah, bh, dims) + (_dg(ah, bl, dims) + _dg(al, bh, dims))


def _dot_exact_rhs(a, m01, dims=NN):
    a1, a2, a3 = _split3(a)
    return _dg(a1, m01, dims) + (_dg(a2, m01, dims) + _dg(a3, m01, dims))


def _dot_exact_lhs(m01, b, dims=NN):
    b1, b2, b3 = _split3(b)
    return _dg(m01, b1, dims) + (_dg(m01, b2, dims) + _dg(m01, b3, dims))


def _layer_norm(y, g, b):
    mu = jnp.mean(y, axis=-1, keepdims=True)
    d = y - mu
    var = jnp.mean(d * d, axis=-1, keepdims=True)
    return d * lax.rsqrt(var + LN_EPS) * g + b


def _gelu_tanh(x):
    return 0.5 * x * (1.0 + jnp.tanh(0.7978845608028654 * (x + 0.044715 * (x * x * x))))


def _sigmoid(x):
    return 1.0 / (1.0 + jnp.exp(-x))


def _ffn_kernel(x_ref, wg_ref, wu_ref, wd_ref, g_ref, b_ref, o_ref, acc_ref, xb_ref):
    j = pl.program_id(1)

    @pl.when(j == 0)
    def _():
        acc_ref[...] = jnp.zeros_like(acc_ref)
        xb_ref[...] = x_ref[...].astype(BF16)

    xb = xb_ref[...]
    hg = _dg(xb, wg_ref[...], NN)
    hu = _dg(xb, wu_ref[...], NN)
    h = (hg * _sigmoid(hg) * hu).astype(BF16)
    acc_ref[...] += _dg(h, wd_ref[...], NN)

    @pl.when(j == pl.num_programs(1) - 1)
    def _():
        y = ALPHA * x_ref[...] + 0.5 * acc_ref[...]
        o_ref[...] = _layer_norm(y, g_ref[...], b_ref[...])


def _ffn(x, wg, wu, wd, g, b):
    m, d = x.shape
    ff = wg.shape[1]
    tm = min(FFN_TM, m)
    return pl.pallas_call(
        _ffn_kernel,
        grid=(m // tm, ff // FFN_TF),
        in_specs=[
            pl.BlockSpec((tm, d), lambda i, j: (i, 0)),
            pl.BlockSpec((d, FFN_TF), lambda i, j: (0, j)),
            pl.BlockSpec((d, FFN_TF), lambda i, j: (0, j)),
            pl.BlockSpec((FFN_TF, d), lambda i, j: (j, 0)),
            pl.BlockSpec((1, d), lambda i, j: (0, 0)),
            pl.BlockSpec((1, d), lambda i, j: (0, 0)),
        ],
        out_specs=pl.BlockSpec((tm, d), lambda i, j: (i, 0)),
        out_shape=jax.ShapeDtypeStruct((m, d), F32),
        scratch_shapes=[pltpu.VMEM((tm, d), F32), pltpu.VMEM((tm, d), BF16)],
        compiler_params=_cparams(("parallel", "arbitrary")),
        name="ffn_swiglu_ln",
    )(x, wg, wu, wd, g.reshape(1, d), b.reshape(1, d))


def _mm_kernel(x_ref, w_ref, o_ref):
    o_ref[...] = _dg(x_ref[...], w_ref[...], NN).astype(o_ref.dtype)


def _mm(x, w, out_dtype, tn=MM_TN):
    m, k = x.shape
    n = w.shape[1]
    tm = min(MM_TM, m)
    tn = min(tn, n)
    return pl.pallas_call(
        _mm_kernel,
        grid=(m // tm, n // tn),
        in_specs=[
            pl.BlockSpec((tm, k), lambda i, j: (i, 0)),
            pl.BlockSpec((k, tn), lambda i, j: (0, j)),
        ],
        out_specs=pl.BlockSpec((tm, tn), lambda i, j: (i, j)),
        out_shape=jax.ShapeDtypeStruct((m, n), out_dtype),
        compiler_params=_cparams(("parallel", "arbitrary")),
        name="proj_mm",
    )(x, w)


def _token_shift(x, prev_row, mu, first):
    rows = lax.broadcasted_iota(jnp.int32, x.shape, 0)
    prev_row = jnp.where(first, jnp.zeros_like(prev_row), prev_row)
    shifted = jnp.where(rows == 0, prev_row, pltpu.roll(x, 1, 0))
    return x + (shifted - x) * mu


def _rw_prep_kernel(seq_len, has_vres, *refs):
    if has_vres:
        (p_ref, pp_ref, q_ref, qp_ref, mu_ref, mul_ref, w0_ref, a0_ref, w2_ref, a2_ref, g2_ref,
         vf_ref, v0_ref, v1_ref, v2_ref, r_out, k_out, v_out, w_out, a_out, g_out) = refs
    else:
        (p_ref, pp_ref, q_ref, qp_ref, mu_ref, mul_ref, w0_ref, a0_ref, w2_ref, a2_ref, g2_ref,
         r_out, k_out, v_out, w_out, a_out, g_out) = refs
    tm = p_ref.shape[0]
    first = (pl.program_id(0) * tm) % seq_len == 0
    wd = RW_WIDTH

    def shifted(c0, c1):
        return _token_shift(p_ref[:, c0:c1], pp_ref[7:8, c0:c1], mu_ref[:, c0:c1], first)

    r_out[...] = shifted(0, wd)
    k_out[...] = shifted(wd, 2 * wd)
    v = shifted(2 * wd, 3 * wd)
    lo = _token_shift(q_ref[...], qp_ref[7:8, :], mul_ref[...], first)
    wl = lo[:, :RW_DECAY_LORA]
    al = lo[:, RW_DECAY_LORA:RW_DECAY_LORA + RW_AAA_LORA]
    gl = lo[:, RW_DECAY_LORA + RW_AAA_LORA:]
    z = w0_ref[...] + _dot3(jnp.tanh(wl), w2_ref[...])
    w_out[...] = -(jnp.maximum(-z, 0.0) + jnp.log1p(jnp.exp(-jnp.abs(z)))) - 0.5
    a_out[...] = _sigmoid(a0_ref[...] + _dot3(al, a2_ref[...]))
    g_out[...] = _dot3(_sigmoid(gl), g2_ref[...])
    if has_vres:
        mix = _sigmoid(v0_ref[...] + _dot3(_dot3(v, v1_ref[...]), v2_ref[...]))
        v = v + (vf_ref[...] - v) * mix
    v_out[...] = v


def _rw_prep(p_rkv, p_lora, mu, w0, a0, w2, a2, g2, vres, seq_len):
    m = p_rkv.shape[0]
    tm = PREP_TM
    wd = RW_WIDTH
    c3 = 3 * wd
    row = lambda i: (i, 0)
    prev = lambda i: (jnp.maximum(i * (tm // 8) - 1, 0), 0)
    fix = lambda i: (0, 0)
    in_specs = [
        pl.BlockSpec((tm, c3), row), pl.BlockSpec((8, c3), prev),
        pl.BlockSpec((tm, RW_LORA_COLS), row), pl.BlockSpec((8, RW_LORA_COLS), prev),
        pl.BlockSpec((1, c3), fix), pl.BlockSpec((1, RW_LORA_COLS), fix),
        pl.BlockSpec((1, wd), fix), pl.BlockSpec((1, wd), fix),
        pl.BlockSpec((RW_DECAY_LORA, wd), fix), pl.BlockSpec((RW_AAA_LORA, wd), fix),
        pl.BlockSpec((RW_GATE_LORA, wd), fix),
    ]
    args = [p_rkv, p_rkv, p_lora, p_lora, mu[:c3].reshape(1, c3), mu[c3:].reshape(1, RW_LORA_COLS),
            w0.reshape(1, wd), a0.reshape(1, wd), w2, a2, g2]
    if vres is not None:
        v_first, v0, v1, v2 = vres
        in_specs += [pl.BlockSpec((tm, wd), row), pl.BlockSpec((1, wd), fix),
                     pl.BlockSpec((wd, RW_MV_LORA), fix), pl.BlockSpec((RW_MV_LORA, wd), fix)]
        args += [v_first, v0.reshape(1, wd), v1, v2]
    out = jax.ShapeDtypeStruct((m, wd), F32)
    return pl.pallas_call(
        functools.partial(_rw_prep_kernel, seq_len, vres is not None),
        grid=(m // tm,),
        in_specs=in_specs,
        out_specs=[pl.BlockSpec((tm, wd), row)] * 6,
        out_shape=[out] * 6,
        compiler_params=_cparams(("parallel",)),
        name="rwkv_prep",
    )(*args)


def _sp(x):
    return _split2(x)


def _d3s(a, b, dims=NN):
    return _dg(a[0], b[0], dims) + (_dg(a[0], b[1], dims) + _dg(a[1], b[0], dims))


def _d2m(a, m01, dims=NN):
    return _dg(a[0], m01, dims) + _dg(a[1], m01, dims)


def _wkv_kernel(r_ref, k_ref, v_ref, w_ref, a_ref, g_ref, kk_ref, ka_ref, rk_ref, gg_ref, gb_ref,
                o_ref, s_ref):
    L = WKV_L
    n_chunks = r_ref.shape[0] // L

    @pl.when(pl.program_id(2) == 0)
    def _():
        s_ref[...] = jnp.zeros_like(s_ref)

    ri = lax.broadcasted_iota(jnp.int32, (2 * L, 2 * L), 0)
    ci = lax.broadcasted_iota(jnp.int32, (2 * L, 2 * L), 1)
    bd_f = ((ri // L) == (ci // L)).astype(F32)
    bd_b = bd_f.astype(BF16)
    rl = lax.broadcasted_iota(jnp.int32, (L, L), 0)
    cl = lax.broadcasted_iota(jnp.int32, (L, L), 1)
    tri_b = (cl <= rl).astype(BF16)
    rt_i = lax.broadcasted_iota(jnp.int32, (L, 2 * L), 0)
    cs_i = lax.broadcasted_iota(jnp.int32, (L, 2 * L), 1) % L
    strict = cs_i < rt_i
    incl = cs_i <= rt_i

    bd2_b = jnp.concatenate([bd_b, bd_b], axis=1)
    eye_f = (ri == ci).astype(F32)

    def sm(zs, mask=bd_b):
        return tuple(jnp.concatenate([p, p], axis=0) * mask for p in zs)

    k_k = kk_ref[...]
    k_a = ka_ref[...]
    r_k = rk_ref[...]
    gn_g = gg_ref[...]
    gn_b = gb_ref[...]
    inv_n = 1.0 / RW_HEAD
    n_lvl = L.bit_length() - 1

    C = range(n_chunks)
    rows = [slice(c * L, (c + 1) * L) for c in C]
    r = [r_ref[rows[c], :] for c in C]
    k_raw = [k_ref[rows[c], :] for c in C]
    v = [v_ref[rows[c], :] for c in C]
    a = [a_ref[rows[c], :] for c in C]
    kk = [k_raw[c] * k_k for c in C]
    ssq = [_d2m(_sp(kk[c] * kk[c]), bd_b) for c in C]
    lw = [-jnp.exp(w_ref[rows[c], :]) for c in C]
    cum = [_dot_exact_lhs(tri_b, lw[c]) for c in C]
    kk = [kk[c] * lax.rsqrt(jnp.maximum(ssq[c], 1e-24)) for c in C]
    k = [k_raw[c] * (1.0 + (a[c] - 1.0) * k_a) for c in C]
    bv = [kk[c] * a[c] for c in C]
    tot = [cum[c][L - 1:L, :] for c in C]
    e_neg = [jnp.exp(-cum[c]) for c in C]
    rt = [r[c] * jnp.exp(cum[c]) for c in C]
    at = [-kk[c] * jnp.exp(cum[c] - lw[c]) for c in C]
    e_tot = [jnp.exp(tot[c] - cum[c]) for c in C]
    bh_s = [_sp(bv[c] * e_tot[c]) for c in C]
    kh_s = [_sp(k[c] * e_tot[c]) for c in C]
    v_s = [_sp(v[c]) for c in C]
    smv = [sm(v_s[c]) for c in C]
    lhs2 = [_sp(jnp.concatenate([at[c], rt[c]], axis=0)) for c in C]
    smk = [sm(_sp(k[c] * e_neg[c])) for c in C]
    smb = [sm(_sp(bv[c] * e_neg[c])) for c in C]
    ak = [_d3s(lhs2[c], smk[c], NT) for c in C]
    ab = [_d3s(lhs2[c], smb[c], NT) for c in C]
    a_ak = [_sp(jnp.where(strict, ak[c][:L], 0.0)) for c in C]
    a_rk = [jnp.where(incl, ak[c][L:], 0.0).astype(BF16) for c in C]
    p = [jnp.where(strict, ab[c][:L], 0.0) for c in C]
    a_rb = [jnp.where(incl, ab[c][L:], 0.0).astype(BF16) for c in C]
    akv = [_d3s(a_ak[c], smv[c]) for c in C]
    z = [jnp.concatenate([at[c], akv[c]], axis=1) for c in C]
    for lvl in range(n_lvl):
        p_s = [_sp(p[c]) for c in C]
        smz = [sm(_sp(z[c]), bd2_b) for c in C]
        z = [z[c] + _d3s(p_s[c], smz[c]) for c in C]
        if lvl + 1 < n_lvl:
            smp = [sm(p_s[c]) for c in C]
            p = [_d3s(p_s[c], smp[c]) for c in C]
    z_s = [_sp(z[c]) for c in C]
    at2_s = [(z_s[c][0][:, :2 * L], z_s[c][1][:, :2 * L]) for c in C]
    u0_s = [(z_s[c][0][:, 2 * L:], z_s[c][1][:, 2 * L:]) for c in C]
    q = [(rt[c] + _dg(a_rb[c], sm(at2_s[c])[0], NN)).astype(BF16) for c in C]
    y0 = [_dg(a_rk[c], smv[c][0], NN) + _dg(a_rb[c], sm(u0_s[c])[0], NN) for c in C]
    gam_s = [_sp(bd_f * _d3s(at2_s[c], bh_s[c], TN) + eye_f * jnp.exp(tot[c])) for c in C]
    cc = [bd_f * (_d3s(v_s[c], kh_s[c], TN) + _d3s(u0_s[c], bh_s[c], TN)) for c in C]
    bonus = [_d2m(_sp(r[c] * k[c] * r_k), bd_b) * v[c] for c in C]

    state = s_ref[...]
    y = []
    for c in C:
        st_s = _sp(state)
        y.append(_dg(q[c], st_s[0], NT) + y0[c])
        state = _d3s(st_s, gam_s[c]) + cc[c]
    s_ref[...] = state
    mean = [_d2m(_sp(y[c]), bd_b) * inv_n for c in C]
    d = [y[c] - mean[c] for c in C]
    var = [_d2m(_sp(d[c] * d[c]), bd_b) * inv_n for c in C]
    for c in C:
        yn = d[c] * lax.rsqrt(var[c] + RW_GN_EPS) * gn_g + gn_b
        o_ref[rows[c], :] = ((yn + bonus[c]) * g_ref[rows[c], :]).astype(o_ref.dtype)


def _wkv(r, k, v, w, a, g, k_k, k_a, r_k, gn_g, gn_b, batch, seq_len):
    m, wd = r.shape
    tc = min(WKV_TC, seq_len)
    nt = seq_len // tc
    lanes = 2 * RW_HEAD
    act = pl.BlockSpec((tc, lanes), lambda b, h, t: (b * nt + t, h))
    par = pl.BlockSpec((1, lanes), lambda b, h, t: (0, h))
    params = [z.reshape(1, wd) for z in (k_k, k_a, r_k, gn_g, gn_b)]
    return pl.pallas_call(
        _wkv_kernel,
        grid=(batch, wd // lanes, nt),
        in_specs=[act] * 6 + [par] * 5,
        out_specs=act,
        out_shape=jax.ShapeDtypeStruct((m, wd), BF16),
        scratch_shapes=[pltpu.VMEM((lanes, lanes), F32)],
        compiler_params=_cparams(("parallel", "parallel", "arbitrary")),
        name="rwkv_wkv",
    )(r, k, v, w, a, g, *params)


def _gmlp_kernel(p_ref, lg_ref, lb_ref, ws_ref, bst_ref, o_ref):
    tm = p_ref.shape[0]
    u = _gelu_tanh(p_ref[:, :GM_WIDTH])
    v = _layer_norm(_gelu_tanh(p_ref[:, GM_WIDTH:]), lg_ref[...], lb_ref[...])
    rl = lax.broadcasted_iota(jnp.int32, (GM_CHUNK, GM_CHUNK), 0)
    cl = lax.broadcasted_iota(jnp.int32, (GM_CHUNK, GM_CHUNK), 1)
    causal = cl <= rl
    for g in range(GM_GROUPS):
        cols = slice(g * GM_GROUP, (g + 1) * GM_GROUP)
        wsg = jnp.where(causal, ws_ref[g], 0.0)
        bias = bst_ref[:, g:g + 1]
        for c in range(tm // GM_CHUNK):
            rows = slice(c * GM_CHUNK, (c + 1) * GM_CHUNK)
            s = _dot3(wsg, v[rows, cols]) + bias
            o_ref[rows, cols] = (u[rows, cols] * s).astype(o_ref.dtype)


def _gmlp(p_gm, ln_g, ln_b, ws, bs):
    m = p_gm.shape[0]
    tm = GM_TM
    fix2 = lambda i: (0, 0)
    return pl.pallas_call(
        _gmlp_kernel,
        grid=(m // tm,),
        in_specs=[
            pl.BlockSpec((tm, GM_COLS), lambda i: (i, 0)),
            pl.BlockSpec((1, GM_WIDTH), fix2),
            pl.BlockSpec((1, GM_WIDTH), fix2),
            pl.BlockSpec((GM_GROUPS, GM_CHUNK, GM_CHUNK), lambda i: (0, 0, 0)),
            pl.BlockSpec((GM_CHUNK, GM_GROUPS), fix2),
        ],
        out_specs=pl.BlockSpec((tm, GM_WIDTH), lambda i: (i, 0)),
        out_shape=jax.ShapeDtypeStruct((m, GM_WIDTH), BF16),
        compiler_params=_cparams(("parallel",)),
        name="gmlp_mix",
    )(p_gm, ln_g.reshape(1, -1), ln_b.reshape(1, -1), ws, bs.T)


def _compress_kernel(c_ref, pos_ref, w1_ref, w2_ref, o_ref):
    c = c_ref[...]
    n = c.shape[0]
    r1 = _dot3(c, w1_ref[0])
    r2 = _dot3(c, w1_ref[1])
    pos = jnp.broadcast_to(pos_ref[...], (8, pos_ref.shape[1]))
    half = w1_ref.shape[1]
    pterm = (_dot3(pos[:, :half], w1_ref[0]) + _dot3(pos[:, half:], w1_ref[1]))[0:1, :]
    h = r1 + pltpu.roll(r2, n - 1, 0) + pterm
    o_ref[...] = _dot3(_gelu_tanh(h), w2_ref[...])


def _compress(z, pos, w1, w2):
    gb, t, d = z.shape
    nb = t // CMP_STRIDE
    c = z.reshape(gb, nb, CMP_STRIDE * d)
    half = CMP_STRIDE * d
    return pl.pallas_call(
        _compress_kernel,
        grid=(gb,),
        in_specs=[
            pl.BlockSpec((None, nb, half), lambda i: (i, 0, 0)),
            pl.BlockSpec((1, 2 * half), lambda i: (0, 0)),
            pl.BlockSpec((2, half, d), lambda i: (0, 0, 0)),
            pl.BlockSpec((d, d), lambda i: (0, 0)),
        ],
        out_specs=pl.BlockSpec((None, nb, d), lambda i: (i, 0, 0)),
        out_shape=jax.ShapeDtypeStruct((gb, nb, d), F32),
        compiler_params=_cparams(("parallel",)),
        name="nsa_compress",
    )(c, pos.reshape(1, 2 * half), w1.reshape(2, half, d), w2)


def _nsa_proj_kernel(x_ref, wq_ref, wkc_ref, wvc_ref, wks_ref, wvs_ref, wkw_ref, wvw_ref, wg_ref,
                     q_ref, kc_ref, vc_ref, ks_ref, vs_ref, kw_ref, vw_ref, g_ref):
    x = x_ref[...]
    scale = NSA_DK ** -0.5
    for h in range(NSA_HPG):
        q_ref[h] = (_dg(x, wq_ref[h], NN) * scale).astype(q_ref.dtype)
    for w_ref, o_ref in ((wkc_ref, kc_ref), (wvc_ref, vc_ref), (wks_ref, ks_ref), (wvs_ref, vs_ref),
                         (wkw_ref, kw_ref), (wvw_ref, vw_ref), (wg_ref, g_ref)):
        o_ref[...] = _dg(x, w_ref[...], NN).astype(o_ref.dtype)


def _nsa_proj(xb, wl):
    m, kdim = xb.shape
    G, hp = NSA_KV_GROUPS, NSA_HPG
    tm = min(MM_TM, m)
    o = OFF_NSA

    def take(width, d):
        nonlocal o
        w3 = wl[:, o:o + width].astype(BF16).reshape(kdim, width // d, d).transpose(1, 0, 2)
        o += width
        return w3

    wq = take(NSA_Q_COLS, NSA_DK)
    ws = [take(NSA_GK, NSA_DK), take(NSA_GV, NSA_DV), take(NSA_GK, NSA_DK), take(NSA_GV, NSA_DV),
          take(NSA_GK, NSA_DK), take(NSA_GV, NSA_DV), take(NSA_G_COLS, 3 * hp)]
    dts = [F32, F32, BF16, BF16, BF16, BF16, F32]
    wspec = lambda d: pl.BlockSpec((None, kdim, d), lambda i, g: (g, 0, 0))
    ospec = lambda d: pl.BlockSpec((None, tm, d), lambda i, g: (g, i, 0))
    return pl.pallas_call(
        _nsa_proj_kernel,
        grid=(m // tm, G),
        in_specs=[pl.BlockSpec((tm, kdim), lambda i, g: (i, 0)),
                  pl.BlockSpec((hp, kdim, NSA_DK), lambda i, g: (g, 0, 0))]
                 + [wspec(w.shape[2]) for w in ws],
        out_specs=[pl.BlockSpec((hp, tm, NSA_DK), lambda i, g: (g, i, 0))]
                  + [ospec(w.shape[2]) for w in ws],
        out_shape=[jax.ShapeDtypeStruct((G * hp, m, NSA_DK), BF16)]
                  + [jax.ShapeDtypeStruct((G, m, w.shape[2]), dt) for w, dt in zip(ws, dts)],
        compiler_params=_cparams(("parallel", "arbitrary")),
        name="nsa_proj",
    )(xb, wq, *ws)


def _nsa_kernel(seq_len, q_ref, gt_ref, kc_ref, vc_ref, ks_ref, vs_ref, kw_ref, vw_ref,
                o_ref, bias_ref):
    tq = NSA_TQ
    hp = NSA_HPG
    tk = min(NSA_TK, seq_len)
    wk = WIN + tq
    n_s = seq_len // SEL_BLK
    k_sel = min(N_SEL, n_s)
    n_cb = kc_ref.shape[0]
    i = pl.program_id(2)
    t0 = i * tq

    qa = q_ref[...].reshape(hp * tq, NSA_DK)
    t_q = t0 + lax.broadcasted_iota(jnp.int32, (tq, 1), 0)
    t_lane = t0 + lax.broadcasted_iota(jnp.int32, (1, tq), 1)
    t_row = t0 + (lax.broadcasted_iota(jnp.int32, (hp * tq, 1), 0) & (tq - 1))

    w0 = pl.multiple_of(jnp.maximum(t0 - WIN, 0), tq)
    kpos_w = w0 + lax.broadcasted_iota(jnp.int32, (1, wk), 1)
    bias_w = jnp.where((kpos_w <= t_q) & (kpos_w > t_q - WIN), 0.0, NEG)
    kw_tile = kw_ref[pl.ds(w0, wk), :]
    vw_tile = vw_ref[pl.ds(w0, wk), :]
    s_w = _dg(qa, kw_tile, NT)
    s_w = (s_w.reshape(hp, tq, wk) + bias_w[None]).reshape(hp * tq, wk)
    p_w = jnp.exp(s_w - jnp.max(s_w, axis=-1, keepdims=True))
    l_w = jnp.sum(p_w, axis=-1, keepdims=True)
    o_w = _dg(p_w.astype(BF16), vw_tile, NN) * (1.0 / jnp.maximum(l_w, 1e-30))

    s_c = _dg(qa, kc_ref[...].astype(BF16), NT)
    n_end = lax.broadcasted_iota(jnp.int32, (1, n_cb), 1) * CMP_STRIDE + (CMP_BLK - 1)
    m_c = (n_end <= t_row) & (n_end < seq_len)
    s_c = jnp.where(m_c, s_c, NEG)
    e_c = jnp.where(m_c, jnp.exp(s_c - jnp.max(s_c, axis=-1, keepdims=True)), 0.0)
    p_c = e_c * (1.0 / jnp.maximum(jnp.sum(e_c, axis=-1, keepdims=True), 1e-30))
    o_c = _dg(p_c.astype(BF16), vc_ref[...].astype(BF16), NN)

    p_sum = p_c[0:tq]
    for h in range(1, hp):
        p_sum = p_sum + p_c[h * tq:(h + 1) * tq]
    ss = lax.broadcasted_iota(jnp.int32, (n_s, n_cb), 0) * SEL_BLK
    cs = lax.broadcasted_iota(jnp.int32, (n_s, n_cb), 1) * CMP_STRIDE
    overlap_t = ((cs < ss + SEL_BLK) & (cs + (CMP_BLK - 1) >= ss)
                 & (cs + (CMP_BLK - 1) < seq_len)).astype(BF16)
    imp = _dot_exact_lhs(overlap_t, p_sum, NT)
    blk = lax.broadcasted_iota(jnp.int32, (n_s, 1), 0)
    cur = t_lane // SEL_BLK
    valid = blk * SEL_BLK <= t_lane
    forced = valid & ((blk == 0) | (blk == cur) | (blk == cur - 1))
    score = jnp.where(forced, FORCED, jnp.where(valid, imp, NEG))
    rank = jnp.zeros((n_s, tq), jnp.int32)
    for s in range(n_s):
        row = score[s:s + 1, :]
        beats = (row > score) | ((row == score) & (blk > s))
        rank = rank + beats.astype(jnp.int32)
    sel_t = ((rank < k_sel) & (score > 0.5 * NEG)).astype(BF16)
    n_kt = (t0 + tq + tk - 1) // tk
    n_all = seq_len // tk
    half = max(n_all // 2, 1)

    def expand(kt0, kt1):
        width = (kt1 - kt0) * tk
        er = lax.broadcasted_iota(jnp.int32, (n_s, width), 0)
        ec = (lax.broadcasted_iota(jnp.int32, (n_s, width), 1) + kt0 * tk) // SEL_BLK
        chosen = _dg(sel_t, (er == ec).astype(BF16), TN)
        kpos = kt0 * tk + lax.broadcasted_iota(jnp.int32, (1, width), 1)
        bias = jnp.where((chosen > 0.5) & (kpos <= t_q), 0.0, NEG)
        for kt in range(kt0, kt1):
            bias_ref[kt] = bias[:, (kt - kt0) * tk:(kt - kt0 + 1) * tk]

    expand(0, half)
    if n_all > half:
        @pl.when(n_kt > half)
        def _():
            expand(half, n_all)

    def sel_body(kt, carry):
        m_run, l_run, acc = carry
        k0 = pl.multiple_of(kt * tk, tk)
        s = _dg(qa, ks_ref[pl.ds(k0, tk), :], NT)
        s = (s.reshape(hp, tq, tk) + bias_ref[kt][None]).reshape(hp * tq, tk)
        m_new = jnp.maximum(m_run, jnp.max(s, axis=-1, keepdims=True))
        alpha = jnp.exp(m_run - m_new)
        p = jnp.exp(s - m_new)
        l_new = alpha * l_run + jnp.sum(p, axis=-1, keepdims=True)
        acc = alpha * acc + _dg(p.astype(BF16), vs_ref[pl.ds(k0, tk), :], NN)
        return m_new, l_new, acc

    init = (jnp.full((hp * tq, 1), NEG, F32), jnp.zeros((hp * tq, 1), F32),
            jnp.zeros((hp * tq, NSA_DV), F32))
    _, l_s, acc_s = lax.fori_loop(0, n_kt, sel_body, init)
    o_s = acc_s * (1.0 / jnp.maximum(l_s, 1e-30))

    gates = _sigmoid(gt_ref[...])
    for h in range(hp):
        rows = slice(h * tq, (h + 1) * tq)
        out = (gates[:, 3 * h:3 * h + 1] * o_c[rows] + gates[:, 3 * h + 1:3 * h + 2] * o_s[rows]
               + gates[:, 3 * h + 2:3 * h + 3] * o_w[rows])
        o_ref[:, h * NSA_DV:(h + 1) * NSA_DV] = out.astype(o_ref.dtype)


def _nsa_attention(q, p_g, k_cmp, v_cmp, ks, vs, kw, vw, batch, seq_len):
    m = q.shape[1]
    tq = NSA_TQ
    assert seq_len >= WIN + tq and seq_len % NSA_TK == 0
    nq = seq_len // tq
    n_cb = k_cmp.shape[1]
    tk = min(NSA_TK, seq_len)
    kv = lambda b, g, i: (g * batch + b, 0, 0)
    return pl.pallas_call(
        functools.partial(_nsa_kernel, seq_len),
        grid=(batch, NSA_KV_GROUPS, nq),
        in_specs=[
            pl.BlockSpec((NSA_HPG, tq, NSA_DK), lambda b, g, i: (g, b * nq + i, 0)),
            pl.BlockSpec((None, tq, 3 * NSA_HPG), lambda b, g, i: (g, b * nq + i, 0)),
            pl.BlockSpec((None, n_cb, NSA_DK), kv),
            pl.BlockSpec((None, n_cb, NSA_DV), kv),
            pl.BlockSpec((None, seq_len, NSA_DK), kv),
            pl.BlockSpec((None, seq_len, NSA_DV), kv),
            pl.BlockSpec((None, seq_len, NSA_DK), kv),
            pl.BlockSpec((None, seq_len, NSA_DV), kv),
        ],
        out_specs=pl.BlockSpec((tq, NSA_HPG * NSA_DV), lambda b, g, i: (b * nq + i, g)),
        out_shape=jax.ShapeDtypeStruct((m, NSA_WIDTH), BF16),
        scratch_shapes=[pltpu.VMEM((seq_len // tk, tq, tk), F32)],
        compiler_params=_cparams(("parallel", "parallel", "arbitrary")),
        name="nsa_attention",
    )(q, p_g, k_cmp, v_cmp, ks, vs, kw, vw)


def _merge_kernel(y0_ref, y1_ref, y2_ref, g0_ref, g1_ref, g2_ref, w_ref, o_ref):
    acc = _sigmoid(g0_ref[...]) * _dg(y0_ref[...], w_ref[0], NN)
    acc = acc + _sigmoid(g1_ref[...]) * _dg(y1_ref[...], w_ref[1], NN)
    acc = acc + _sigmoid(g2_ref[...]) * _dg(y2_ref[...], w_ref[2], NN)
    o_ref[...] = acc.astype(o_ref.dtype)


def _merge(y_rw, y_gm, y_ns, p_gate, w_br):
    m, d = y_rw.shape
    tm = min(MERGE_TM, m)
    tn = MERGE_TN
    nj = d // tn
    ys = pl.BlockSpec((tm, d), lambda i, j: (i, 0))
    gspec = lambda br: pl.BlockSpec((tm, tn), lambda i, j: (i, br * nj + j))
    return pl.pallas_call(
        _merge_kernel,
        grid=(m // tm, nj),
        in_specs=[ys, ys, ys, gspec(0), gspec(1), gspec(2),
                  pl.BlockSpec((N_BRANCH, d, tn), lambda i, j: (0, 0, j))],
        out_specs=pl.BlockSpec((tm, tn), lambda i, j: (i, j)),
        out_shape=jax.ShapeDtypeStruct((m, d), BF16),
        compiler_params=_cparams(("parallel", "arbitrary")),
        name="branch_merge",
    )(y_rw, y_gm, y_ns, p_gate, p_gate, p_gate, w_br)


def _out_ln_kernel(y_ref, w_ref, x_ref, g_ref, b_ref, o_ref):
    y = ALPHA * x_ref[...] + _dg(y_ref[...], w_ref[...], NN)
    o_ref[...] = _layer_norm(y, g_ref[...], b_ref[...])


def _out_ln(merged, w_o, x, g, b):
    m, d = x.shape
    tm = min(WO_TM, m)
    row = lambda i: (i, 0)
    fix = lambda i: (0, 0)
    return pl.pallas_call(
        _out_ln_kernel,
        grid=(m // tm,),
        in_specs=[pl.BlockSpec((tm, d), row), pl.BlockSpec((d, d), fix), pl.BlockSpec((tm, d), row),
                  pl.BlockSpec((1, d), fix), pl.BlockSpec((1, d), fix)],
        out_specs=pl.BlockSpec((tm, d), row),
        out_shape=jax.ShapeDtypeStruct((m, d), F32),
        compiler_params=_cparams(("parallel",)),
        name="out_proj_ln",
    )(merged, w_o, x, g.reshape(1, d), b.reshape(1, d))


def _rwkv_branch(xb, wl, mu, w0, w2, a0, a2, g2, k_k, k_a, r_k, gn_g, gn_b, vres, batch, seq_len):
    p_rkv = _mm(xb, wl[:, :3 * RW_WIDTH].astype(BF16), F32)
    p_lora = _mm(xb, wl[:, 3 * RW_WIDTH:RW_COLS].astype(BF16), F32, tn=RW_LORA_COLS)
    r, k, v, w, a, g = _rw_prep(p_rkv, p_lora, mu, w0, a0, w2, a2, g2, vres, seq_len)
    y = _wkv(r, k, v, w, a, g, k_k, k_a, r_k.reshape(-1), gn_g, gn_b, batch, seq_len)
    return y, v


def _nsa_branch(xb, wl, pos_k, pos_v, phi_k1, phi_k2, phi_v1, phi_v2, batch, seq_len):
    q, kc, vc, ks, vs, kw, vw, p_g = _nsa_proj(xb, wl)
    gb = NSA_KV_GROUPS * batch
    per_seq = lambda z: z.reshape(gb, seq_len, z.shape[-1])
    k_cmp = _compress(per_seq(kc), pos_k, phi_k1, phi_k2)
    v_cmp = _compress(per_seq(vc), pos_v, phi_v1, phi_v2)
    return _nsa_attention(q, p_g, k_cmp, v_cmp, per_seq(ks), per_seq(vs), per_seq(kw), per_seq(vw),
                          batch, seq_len)


def kernel(x, w_in, rw_mu, rw_w0, rw_w2, rw_a0, rw_a2, rw_g2, rw_v0, rw_v1, rw_v2, rw_k_k, rw_k_a, rw_r_k, rw_gn_g, rw_gn_b, gm_ln_g, gm_ln_b, gm_ws, gm_bs, nsa_pos_k, nsa_pos_v, nsa_phi_k1, nsa_phi_k2, nsa_phi_v1, nsa_phi_v2, w_br, w_o, ffn1_wg, ffn1_wu, ffn1_wd, ffn2_wg, ffn2_wu, ffn2_wd, ln_g, ln_b):
    batch, seq_len, d = x.shape
    m = batch * seq_len
    h = x.reshape(m, d)
    v_first = None
    for l in range(DEPTH):
        h = _ffn(h, ffn1_wg[l].astype(BF16), ffn1_wu[l].astype(BF16), ffn1_wd[l].astype(BF16),
                 ln_g[l, 0], ln_b[l, 0])
        hb = h.astype(BF16)
        wl = w_in[l]
        vres = None if l == 0 else (v_first, rw_v0[l - 1], rw_v1[l - 1], rw_v2[l - 1])
        y_rw, v_out = _rwkv_branch(hb, wl, rw_mu[l], rw_w0[l], rw_w2[l], rw_a0[l], rw_a2[l], rw_g2[l],
                                   rw_k_k[l], rw_k_a[l], rw_r_k[l], rw_gn_g[l], rw_gn_b[l], vres,
                                   batch, seq_len)
        if l == 0:
            v_first = v_out
        p_gm = _mm(hb, wl[:, OFF_GM:OFF_NSA].astype(BF16), F32)
        y_gm = _gmlp(p_gm, gm_ln_g[l], gm_ln_b[l], gm_ws[l], gm_bs[l])
        y_ns = _nsa_branch(hb, wl, nsa_pos_k[l], nsa_pos_v[l], nsa_phi_k1[l], nsa_phi_k2[l],
                           nsa_phi_v1[l], nsa_phi_v2[l], batch, seq_len)
        p_gate = _mm(hb, wl[:, OFF_GATE:].astype(BF16), F32)
        merged = _merge(y_rw, y_gm, y_ns, p_gate, w_br[l].astype(BF16))
        h = _out_ln(merged, w_o[l].astype(BF16), h, ln_g[l, 1], ln_b[l, 1])
        h = _ffn(h, ffn2_wg[l].astype(BF16), ffn2_wu[l].astype(BF16), ffn2_wd[l].astype(BF16),
                 ln_g[l, 2], ln_b[l, 2])
    return h.reshape(batch, seq_len, d)
```

```python
import functools

import jax
import jax.numpy as jnp
from jax import lax
from jax.experimental import pallas as pl
from jax.experimental.pallas import tpu as pltpu

F32 = jnp.float32
BF16 = jnp.bfloat16

D_MODEL = 2048
DEPTH = 2
RW_HEAD = 64
RW_WIDTH = D_MODEL
RW_DECAY_LORA = 96
RW_AAA_LORA = 96
RW_MV_LORA = 64
RW_GATE_LORA = 256
RW_LORA_COLS = RW_DECAY_LORA + RW_AAA_LORA + RW_GATE_LORA
RW_GN_EPS = RW_HEAD * 1e-5
GM_WIDTH = D_MODEL
GM_CHUNK = 128
GM_GROUP = 128
GM_GROUPS = GM_WIDTH // GM_GROUP
NSA_HEADS = 16
NSA_KV_GROUPS = 4
NSA_HPG = NSA_HEADS // NSA_KV_GROUPS
NSA_DK = 192
NSA_DV = 128
NSA_WIDTH = NSA_HEADS * NSA_DV
CMP_BLK = 32
CMP_STRIDE = 16
SEL_BLK = 64
N_SEL = 16
WIN = 512
D_FF = 5632
N_BRANCH = 3
ALPHA = (2 * DEPTH) ** 0.25
LN_EPS = 1e-5
NEG = -1e30
FORCED = 1e6

RW_COLS = 3 * RW_WIDTH + RW_LORA_COLS
GM_COLS = 2 * GM_WIDTH
NSA_Q_COLS = NSA_HEADS * NSA_DK
NSA_GK = NSA_KV_GROUPS * NSA_DK
NSA_GV = NSA_KV_GROUPS * NSA_DV
NSA_KV_COLS = 3 * (NSA_GK + NSA_GV)
NSA_G_COLS = 3 * NSA_HEADS
NSA_COLS = NSA_Q_COLS + NSA_KV_COLS + NSA_G_COLS
OFF_GM = RW_COLS
OFF_NSA = OFF_GM + GM_COLS
OFF_GATE = OFF_NSA + NSA_COLS

V7X_VMEM_BYTES = 64 * 1024 * 1024
VMEM_LIMIT = 56 * 1024 * 1024

FFN_TM = 512
FFN_TF = 512
MM_TM = 1024
MM_TN = 512
MERGE_TM = 512
MERGE_TN = 512
WO_TM = 512
PREP_TM = 128
WKV_L = 64
WKV_TC = 512
WKV_PAIRS = 2
GM_TM = 256
NSA_TQ = 128
NSA_TK = 512

NN = (((1,), (0,)), ((), ()))
NT = (((1,), (1,)), ((), ()))
TN = (((0,), (0,)), ((), ()))


def _cparams(sem):
    return pltpu.CompilerParams(dimension_semantics=sem, vmem_limit_bytes=VMEM_LIMIT)


def _dg(a, b, dims):
    return lax.dot_general(a, b, dims, preferred_element_type=F32)


def _split2(x):
    hi = x.astype(BF16)
    lo = (x - hi.astype(F32)).astype(BF16)
    return hi, lo


def _split3(x):
    x1 = x.astype(BF16)
    r1 = x - x1.astype(F32)
    x2 = r1.astype(BF16)
    x3 = (r1 - x2.astype(F32)).astype(BF16)
    return x1, x2, x3


def _dot3(a, b, dims=NN):
    ah, al = _split2(a)
    bh, bl = _split2(b)
    return _dg(ah, bh, dims) + (_dg(ah, bl, dims) + _dg(al, bh, dims))


def _dot_exact_rhs(a, m01, dims=NN):
    a1, a2, a3 = _split3(a)
    return _dg(a1, m01, dims) + (_dg(a2, m01, dims) + _dg(a3, m01, dims))


def _dot_exact_lhs(m01, b, dims=NN):
    b1, b2, b3 = _split3(b)
    return _dg(m01, b1, dims) + (_dg(m01, b2, dims) + _dg(m01, b3, dims))


def _layer_norm(y, g, b):
    mu = jnp.mean(y, axis=-1, keepdims=True)
    d = y - mu
    var = jnp.mean(d * d, axis=-1, keepdims=True)
    return d * lax.rsqrt(var + LN_EPS) * g + b


def _gelu_tanh(x):
    return 0.5 * x * (1.0 + jnp.tanh(0.7978845608028654 * (x + 0.044715 * (x * x * x))))


def _sigmoid(x):
    return 1.0 / (1.0 + jnp.exp(-x))


def _ffn_kernel(x_ref, wg_ref, wu_ref, wd_ref, g_ref, b_ref, o_ref, acc_ref, xb_ref):
    j = pl.program_id(1)

    @pl.when(j == 0)
    def _():
        acc_ref[...] = jnp.zeros_like(acc_ref)
        xb_ref[...] = x_ref[...].astype(BF16)

    xb = xb_ref[...]
    hg = _dg(xb, wg_ref[...], NN)
    hu = _dg(xb, wu_ref[...], NN)
    h = (hg * _sigmoid(hg) * hu).astype(BF16)
    acc_ref[...] += _dg(h, wd_ref[...], NN)

    @pl.when(j == pl.num_programs(1) - 1)
    def _():
        y = ALPHA * x_ref[...] + 0.5 * acc_ref[...]
        o_ref[...] = _layer_norm(y, g_ref[...], b_ref[...])


def _ffn(x, wg, wu, wd, g, b):
    m, d = x.shape
    ff = wg.shape[1]
    tm = min(FFN_TM, m)
    return pl.pallas_call(
        _ffn_kernel,
        grid=(m // tm, ff // FFN_TF),
        in_specs=[
            pl.BlockSpec((tm, d), lambda i, j: (i, 0)),
            pl.BlockSpec((d, FFN_TF), lambda i, j: (0, j)),
            pl.BlockSpec((d, FFN_TF), lambda i, j: (0, j)),
            pl.BlockSpec((FFN_TF, d), lambda i, j: (j, 0)),
            pl.BlockSpec((1, d), lambda i, j: (0, 0)),
            pl.BlockSpec((1, d), lambda i, j: (0, 0)),
        ],
        out_specs=pl.BlockSpec((tm, d), lambda i, j: (i, 0)),
        out_shape=jax.ShapeDtypeStruct((m, d), F32),
        scratch_shapes=[pltpu.VMEM((tm, d), F32), pltpu.VMEM((tm, d), BF16)],
        compiler_params=_cparams(("parallel", "arbitrary")),
        name="ffn_swiglu_ln",
    )(x, wg, wu, wd, g.reshape(1, d), b.reshape(1, d))


def _mm_kernel(x_ref, w_ref, o_ref):
    o_ref[...] = _dg(x_ref[...], w_ref[...], NN).astype(o_ref.dtype)


def _mm(x, w, out_dtype, tn=MM_TN):
    m, k = x.shape
    n = w.shape[1]
    tm = min(MM_TM, m)
    tn = min(tn, n)
    return pl.pallas_call(
        _mm_kernel,
        grid=(m // tm, n // tn),
        in_specs=[
            pl.BlockSpec((tm, k), lambda i, j: (i, 0)),
            pl.BlockSpec((k, tn), lambda i, j: (0, j)),
        ],
        out_specs=pl.BlockSpec((tm, tn), lambda i, j: (i, j)),
        out_shape=jax.ShapeDtypeStruct((m, n), out_dtype),
        compiler_params=_cparams(("parallel", "arbitrary")),
        name="proj_mm",
    )(x, w)


def _token_shift(x, prev_row, mu, first):
    rows = lax.broadcasted_iota(jnp.int32, x.shape, 0)
    prev_row = jnp.where(first, jnp.zeros_like(prev_row), prev_row)
    shifted = jnp.where(rows == 0, prev_row, pltpu.roll(x, 1, 0))
    return x + (shifted - x) * mu


def _rw_prep_kernel(seq_len, has_vres, *refs):
    if has_vres:
        (p_ref, pp_ref, q_ref, qp_ref, mu_ref, mul_ref, w0_ref, a0_ref, w2_ref, a2_ref, g2_ref,
         vf_ref, v0_ref, v1_ref, v2_ref, r_out, k_out, v_out, w_out, a_out, g_out) = refs
    else:
        (p_ref, pp_ref, q_ref, qp_ref, mu_ref, mul_ref, w0_ref, a0_ref, w2_ref, a2_ref, g2_ref,
         r_out, k_out, v_out, w_out, a_out, g_out) = refs
    tm = p_ref.shape[0]
    first = (pl.program_id(0) * tm) % seq_len == 0
    wd = RW_WIDTH

    def shifted(c0, c1):
        return _token_shift(p_ref[:, c0:c1], pp_ref[7:8, c0:c1], mu_ref[:, c0:c1], first)

    r_out[...] = shifted(0, wd)
    k_out[...] = shifted(wd, 2 * wd)
    v = shifted(2 * wd, 3 * wd)
    lo = _token_shift(q_ref[...], qp_ref[7:8, :], mul_ref[...], first)
    wl = lo[:, :RW_DECAY_LORA]
    al = lo[:, RW_DECAY_LORA:RW_DECAY_LORA + RW_AAA_LORA]
    gl = lo[:, RW_DECAY_LORA + RW_AAA_LORA:]
    z = w0_ref[...] + _dot3(jnp.tanh(wl), w2_ref[...])
    w_out[...] = -(jnp.maximum(-z, 0.0) + jnp.log1p(jnp.exp(-jnp.abs(z)))) - 0.5
    a_out[...] = _sigmoid(a0_ref[...] + _dot3(al, a2_ref[...]))
    g_out[...] = _dot3(_sigmoid(gl), g2_ref[...])
    if has_vres:
        mix = _sigmoid(v0_ref[...] + _dot3(_dot3(v, v1_ref[...]), v2_ref[...]))
        v = v + (vf_ref[...] - v) * mix
    v_out[...] = v


def _rw_prep(p_rkv, p_lora, mu, w0, a0, w2, a2, g2, vres, seq_len):
    m = p_rkv.shape[0]
    tm = PREP_TM
    wd = RW_WIDTH
    c3 = 3 * wd
    row = lambda i: (i, 0)
    prev = lambda i: (jnp.maximum(i * (tm // 8) - 1, 0), 0)
    fix = lambda i: (0, 0)
    in_specs = [
        pl.BlockSpec((tm, c3), row), pl.BlockSpec((8, c3), prev),
        pl.BlockSpec((tm, RW_LORA_COLS), row), pl.BlockSpec((8, RW_LORA_COLS), prev),
        pl.BlockSpec((1, c3), fix), pl.BlockSpec((1, RW_LORA_COLS), fix),
        pl.BlockSpec((1, wd), fix), pl.BlockSpec((1, wd), fix),
        pl.BlockSpec((RW_DECAY_LORA, wd), fix), pl.BlockSpec((RW_AAA_LORA, wd), fix),
        pl.BlockSpec((RW_GATE_LORA, wd), fix),
    ]
    args = [p_rkv, p_rkv, p_lora, p_lora, mu[:c3].reshape(1, c3), mu[c3:].reshape(1, RW_LORA_COLS),
            w0.reshape(1, wd), a0.reshape(1, wd), w2, a2, g2]
    if vres is not None:
        v_first, v0, v1, v2 = vres
        in_specs += [pl.BlockSpec((tm, wd), row), pl.BlockSpec((1, wd), fix),
                     pl.BlockSpec((wd, RW_MV_LORA), fix), pl.BlockSpec((RW_MV_LORA, wd), fix)]
        args += [v_first, v0.reshape(1, wd), v1, v2]
    out = jax.ShapeDtypeStruct((m, wd), F32)
    return pl.pallas_call(
        functools.partial(_rw_prep_kernel, seq_len, vres is not None),
        grid=(m // tm,),
        in_specs=in_specs,
        out_specs=[pl.BlockSpec((tm, wd), row)] * 6,
        out_shape=[out] * 6,
        compiler_params=_cparams(("parallel",)),
        name="rwkv_prep",
    )(*args)


def _sp(x):
    return _split2(x)


def _d3s(a, b, dims=NN):
    return _dg(a[0], b[0], dims) + (_dg(a[0], b[1], dims) + _dg(a[1], b[0], dims))


def _d2m(a, m01, dims=NN):
    return _dg(a[0], m01, dims) + _dg(a[1], m01, dims)


def _wkv_kernel(r_ref, k_ref, v_ref, w_ref, a_ref, g_ref, kk_ref, ka_ref, rk_ref, gg_ref, gb_ref,
                o_ref, s_ref):
    L = WKV_L
    n_chunks = r_ref.shape[0] // L

    @pl.when(pl.program_id(2) == 0)
    def _():
        s_ref[...] = jnp.zeros_like(s_ref)

    ri = lax.broadcasted_iota(jnp.int32, (2 * L, 2 * L), 0)
    ci = lax.broadcasted_iota(jnp.int32, (2 * L, 2 * L), 1)
    bd_f = ((ri // L) == (ci // L)).astype(F32)
    bd_b = bd_f.astype(BF16)
    rl = lax.broadcasted_iota(jnp.int32, (L, L), 0)
    cl = lax.broadcasted_iota(jnp.int32, (L, L), 1)
    tri_b = (cl <= rl).astype(BF16)
    rt_i = lax.broadcasted_iota(jnp.int32, (L, 2 * L), 0)
    cs_i = lax.broadcasted_iota(jnp.int32, (L, 2 * L), 1) % L
    strict = cs_i < rt_i
    incl = cs_i <= rt_i

    bd2_b = jnp.concatenate([bd_b, bd_b], axis=1)
    eye_f = (ri == ci).astype(F32)

    def sm(zb, mask=bd_b):
        return jnp.concatenate([zb, zb], axis=0) * mask

    def b16(x):
        return x.astype(BF16)

    inv_n = 1.0 / RW_HEAD
    n_lvl = L.bit_length() - 1
    n_pairs = r_ref.shape[1] // (2 * L)

    C = range(n_chunks * n_pairs)
    rows = [slice((c // n_pairs) * L, (c // n_pairs + 1) * L) for c in C]
    lanes = [slice((c % n_pairs) * 2 * L, (c % n_pairs + 1) * 2 * L) for c in C]
    k_k = [kk_ref[:, lanes[c]] for c in C]
    k_a = [ka_ref[:, lanes[c]] for c in C]
    r_k = [rk_ref[:, lanes[c]] for c in C]
    gn_g = [gg_ref[:, lanes[c]] for c in C]
    gn_b = [gb_ref[:, lanes[c]] for c in C]
    r = [r_ref[rows[c], lanes[c]] for c in C]
    k_raw = [k_ref[rows[c], lanes[c]] for c in C]
    v = [v_ref[rows[c], lanes[c]] for c in C]
    a = [a_ref[rows[c], lanes[c]] for c in C]
    kk = [k_raw[c] * k_k[c] for c in C]
    ssq = [_d2m(_sp(kk[c] * kk[c]), bd_b) for c in C]
    lw = [-jnp.exp(w_ref[rows[c], lanes[c]]) for c in C]
    cum = [_dot_exact_lhs(tri_b, lw[c]) for c in C]
    kk = [kk[c] * lax.rsqrt(jnp.maximum(ssq[c], 1e-24)) for c in C]
    k = [k_raw[c] * (1.0 + (a[c] - 1.0) * k_a[c]) for c in C]
    bv = [kk[c] * a[c] for c in C]
    tot = [cum[c][L - 1:L, :] for c in C]
    e_neg = [jnp.exp(-cum[c]) for c in C]
    rt = [r[c] * jnp.exp(cum[c]) for c in C]
    at = [-kk[c] * jnp.exp(cum[c] - lw[c]) for c in C]
    e_tot = [jnp.exp(tot[c] - cum[c]) for c in C]
    bh = [b16(bv[c] * e_tot[c]) for c in C]
    kh = [b16(k[c] * e_tot[c]) for c in C]
    vb = [b16(v[c]) for c in C]
    smv = [sm(vb[c]) for c in C]
    lhs2 = [b16(jnp.concatenate([at[c], rt[c]], axis=0)) for c in C]
    smk = [sm(b16(k[c] * e_neg[c])) for c in C]
    smb = [sm(b16(bv[c] * e_neg[c])) for c in C]
    ak = [_dg(lhs2[c], smk[c], NT) for c in C]
    ab = [_dg(lhs2[c], smb[c], NT) for c in C]
    a_ak = [b16(jnp.where(strict, ak[c][:L], 0.0)) for c in C]
    a_rk = [b16(jnp.where(incl, ak[c][L:], 0.0)) for c in C]
    p = [jnp.where(strict, ab[c][:L], 0.0) for c in C]
    a_rb = [b16(jnp.where(incl, ab[c][L:], 0.0)) for c in C]
    akv = [_dg(a_ak[c], smv[c], NN) for c in C]
    z = [jnp.concatenate([at[c], akv[c]], axis=1) for c in C]
    for lvl in range(n_lvl):
        pb = [b16(p[c]) for c in C]
        smz = [sm(b16(z[c]), bd2_b) for c in C]
        z = [z[c] + _dg(pb[c], smz[c], NN) for c in C]
        if lvl + 1 < n_lvl:
            smp = [sm(pb[c]) for c in C]
            p = [_dg(pb[c], smp[c], NN) for c in C]
    zb = [b16(z[c]) for c in C]
    at2 = [zb[c][:, :2 * L] for c in C]
    u0 = [zb[c][:, 2 * L:] for c in C]
    q = [b16(rt[c] + _dg(a_rb[c], sm(at2[c]), NN)) for c in C]
    y0 = [_dg(a_rk[c], smv[c], NN) + _dg(a_rb[c], sm(u0[c]), NN) for c in C]
    gam_s = [_sp(bd_f * _dg(at2[c], bh[c], TN) + eye_f * jnp.exp(tot[c])) for c in C]
    cc = [bd_f * (_dg(vb[c], kh[c], TN) + _dg(u0[c], bh[c], TN)) for c in C]
    bonus = [_d2m(_sp(r[c] * k[c] * r_k[c]), bd_b) * v[c] for c in C]

    state = [s_ref[pi] for pi in range(n_pairs)]
    y = []
    for c in C:
        pi = c % n_pairs
        st_s = _sp(state[pi])
        y.append(_dg(q[c], st_s[0], NT) + y0[c])
        state[pi] = _d3s(st_s, gam_s[c]) + cc[c]
    for pi in range(n_pairs):
        s_ref[pi] = state[pi]
    mean = [_d2m(_sp(y[c]), bd_b) * inv_n for c in C]
    d = [y[c] - mean[c] for c in C]
    var = [_d2m(_sp(d[c] * d[c]), bd_b) * inv_n for c in C]
    for c in C:
        yn = d[c] * lax.rsqrt(var[c] + RW_GN_EPS) * gn_g[c] + gn_b[c]
        o_ref[rows[c], lanes[c]] = ((yn + bonus[c]) * g_ref[rows[c], lanes[c]]).astype(o_ref.dtype)


def _wkv(r, k, v, w, a, g, k_k, k_a, r_k, gn_g, gn_b, batch, seq_len):
    m, wd = r.shape
    tc = min(WKV_TC, seq_len)
    nt = seq_len // tc
    pair = 2 * RW_HEAD
    lanes = WKV_PAIRS * pair
    act = pl.BlockSpec((tc, lanes), lambda b, h, t: (b * nt + t, h))
    par = pl.BlockSpec((1, lanes), lambda b, h, t: (0, h))
    params = [z.reshape(1, wd) for z in (k_k, k_a, r_k, gn_g, gn_b)]
    return pl.pallas_call(
        _wkv_kernel,
        grid=(batch, wd // lanes, nt),
        in_specs=[act] * 6 + [par] * 5,
        out_specs=act,
        out_shape=jax.ShapeDtypeStruct((m, wd), BF16),
        scratch_shapes=[pltpu.VMEM((WKV_PAIRS, pair, pair), F32)],
        compiler_params=_cparams(("parallel", "parallel", "arbitrary")),
        name="rwkv_wkv",
    )(r, k, v, w, a, g, *params)


def _gmlp_kernel(p_ref, lg_ref, lb_ref, ws_ref, bst_ref, o_ref):
    tm = p_ref.shape[0]
    u = _gelu_tanh(p_ref[:, :GM_WIDTH])
    v = _layer_norm(_gelu_tanh(p_ref[:, GM_WIDTH:]), lg_ref[...], lb_ref[...])
    rl = lax.broadcasted_iota(jnp.int32, (GM_CHUNK, GM_CHUNK), 0)
    cl = lax.broadcasted_iota(jnp.int32, (GM_CHUNK, GM_CHUNK), 1)
    causal = cl <= rl
    for g in range(GM_GROUPS):
        cols = slice(g * GM_GROUP, (g + 1) * GM_GROUP)
        wsg = jnp.where(causal, ws_ref[g], 0.0)
        bias = bst_ref[:, g:g + 1]
        for c in range(tm // GM_CHUNK):
            rows = slice(c * GM_CHUNK, (c + 1) * GM_CHUNK)
            s = _dot3(wsg, v[rows, cols]) + bias
            o_ref[rows, cols] = (u[rows, cols] * s).astype(o_ref.dtype)


def _gmlp(p_gm, ln_g, ln_b, ws, bs):
    m = p_gm.shape[0]
    tm = GM_TM
    fix2 = lambda i: (0, 0)
    return pl.pallas_call(
        _gmlp_kernel,
        grid=(m // tm,),
        in_specs=[
            pl.BlockSpec((tm, GM_COLS), lambda i: (i, 0)),
            pl.BlockSpec((1, GM_WIDTH), fix2),
            pl.BlockSpec((1, GM_WIDTH), fix2),
            pl.BlockSpec((GM_GROUPS, GM_CHUNK, GM_CHUNK), lambda i: (0, 0, 0)),
            pl.BlockSpec((GM_CHUNK, GM_GROUPS), fix2),
        ],
        out_specs=pl.BlockSpec((tm, GM_WIDTH), lambda i: (i, 0)),
        out_shape=jax.ShapeDtypeStruct((m, GM_WIDTH), BF16),
        compiler_params=_cparams(("parallel",)),
        name="gmlp_mix",
    )(p_gm, ln_g.reshape(1, -1), ln_b.reshape(1, -1), ws, bs.T)


def _compress_kernel(c_ref, pos_ref, w1_ref, w2_ref, o_ref):
    c = c_ref[...]
    n = c.shape[0]
    r1 = _dot3(c, w1_ref[0])
    r2 = _dot3(c, w1_ref[1])
    pos = jnp.broadcast_to(pos_ref[...], (8, pos_ref.shape[1]))
    half = w1_ref.shape[1]
    pterm = (_dot3(pos[:, :half], w1_ref[0]) + _dot3(pos[:, half:], w1_ref[1]))[0:1, :]
    h = r1 + pltpu.roll(r2, n - 1, 0) + pterm
    o_ref[...] = _dot3(_gelu_tanh(h), w2_ref[...])


def _compress(z, pos, w1, w2):
    gb, t, d = z.shape
    nb = t // CMP_STRIDE
    c = z.reshape(gb, nb, CMP_STRIDE * d)
    half = CMP_STRIDE * d
    return pl.pallas_call(
        _compress_kernel,
        grid=(gb,),
        in_specs=[
            pl.BlockSpec((None, nb, half), lambda i: (i, 0, 0)),
            pl.BlockSpec((1, 2 * half), lambda i: (0, 0)),
            pl.BlockSpec((2, half, d), lambda i: (0, 0, 0)),
            pl.BlockSpec((d, d), lambda i: (0, 0)),
        ],
        out_specs=pl.BlockSpec((None, nb, d), lambda i: (i, 0, 0)),
        out_shape=jax.ShapeDtypeStruct((gb, nb, d), F32),
        compiler_params=_cparams(("parallel",)),
        name="nsa_compress",
    )(c, pos.reshape(1, 2 * half), w1.reshape(2, half, d), w2)


def _nsa_proj_kernel(x_ref, wq_ref, wkc_ref, wvc_ref, wks_ref, wvs_ref, wkw_ref, wvw_ref, wg_ref,
                     q_ref, kc_ref, vc_ref, ks_ref, vs_ref, kw_ref, vw_ref, g_ref):
    x = x_ref[...]
    scale = NSA_DK ** -0.5
    for h in range(NSA_HPG):
        q_ref[h] = (_dg(x, wq_ref[h], NN) * scale).astype(q_ref.dtype)
    for w_ref, o_ref in ((wkc_ref, kc_ref), (wvc_ref, vc_ref), (wks_ref, ks_ref), (wvs_ref, vs_ref),
                         (wkw_ref, kw_ref), (wvw_ref, vw_ref), (wg_ref, g_ref)):
        o_ref[...] = _dg(x, w_ref[...], NN).astype(o_ref.dtype)


def _nsa_proj(xb, wl):
    m, kdim = xb.shape
    G, hp = NSA_KV_GROUPS, NSA_HPG
    tm = min(MM_TM, m)
    o = OFF_NSA

    def take(width, d):
        nonlocal o
        w3 = wl[:, o:o + width].astype(BF16).reshape(kdim, width // d, d).transpose(1, 0, 2)
        o += width
        return w3

    wq = take(NSA_Q_COLS, NSA_DK)
    ws = [take(NSA_GK, NSA_DK), take(NSA_GV, NSA_DV), take(NSA_GK, NSA_DK), take(NSA_GV, NSA_DV),
          take(NSA_GK, NSA_DK), take(NSA_GV, NSA_DV), take(NSA_G_COLS, 3 * hp)]
    dts = [F32, F32, BF16, BF16, BF16, BF16, F32]
    wspec = lambda d: pl.BlockSpec((None, kdim, d), lambda i, g: (g, 0, 0))
    ospec = lambda d: pl.BlockSpec((None, tm, d), lambda i, g: (g, i, 0))
    return pl.pallas_call(
        _nsa_proj_kernel,
        grid=(m // tm, G),
        in_specs=[pl.BlockSpec((tm, kdim), lambda i, g: (i, 0)),
                  pl.BlockSpec((hp, kdim, NSA_DK), lambda i, g: (g, 0, 0))]
                 + [wspec(w.shape[2]) for w in ws],
        out_specs=[pl.BlockSpec((hp, tm, NSA_DK), lambda i, g: (g, i, 0))]
                  + [ospec(w.shape[2]) for w in ws],
        out_shape=[jax.ShapeDtypeStruct((G * hp, m, NSA_DK), BF16)]
                  + [jax.ShapeDtypeStruct((G, m, w.shape[2]), dt) for w, dt in zip(ws, dts)],
        compiler_params=_cparams(("parallel", "arbitrary")),
        name="nsa_proj",
    )(xb, wq, *ws)


def _nsa_kernel(seq_len, q_ref, gt_ref, kc_ref, vc_ref, ks_ref, vs_ref, kw_ref, vw_ref,
                o_ref, bias_ref):
    tq = NSA_TQ
    hp = NSA_HPG
    tk = min(NSA_TK, seq_len)
    wk = WIN + tq
    n_s = seq_len // SEL_BLK
    k_sel = min(N_SEL, n_s)
    n_cb = kc_ref.shape[0]
    i = pl.program_id(2)
    t0 = i * tq

    qa = q_ref[...].reshape(hp * tq, NSA_DK)
    t_q = t0 + lax.broadcasted_iota(jnp.int32, (tq, 1), 0)
    t_lane = t0 + lax.broadcasted_iota(jnp.int32, (1, tq), 1)
    t_row = t0 + (lax.broadcasted_iota(jnp.int32, (hp * tq, 1), 0) & (tq - 1))

    w0 = pl.multiple_of(jnp.maximum(t0 - WIN, 0), tq)
    kpos_w = w0 + lax.broadcasted_iota(jnp.int32, (1, wk), 1)
    bias_w = jnp.where((kpos_w <= t_q) & (kpos_w > t_q - WIN), 0.0, NEG)
    kw_tile = kw_ref[pl.ds(w0, wk), :]
    vw_tile = vw_ref[pl.ds(w0, wk), :]
    s_w = _dg(qa, kw_tile, NT)
    s_c = _dg(qa, kc_ref[...].astype(BF16), NT)

    n_end = lax.broadcasted_iota(jnp.int32, (1, n_cb), 1) * CMP_STRIDE + (CMP_BLK - 1)
    m_c = (n_end <= t_row) & (n_end < seq_len)
    s_c = jnp.where(m_c, s_c, NEG)
    e_c = jnp.where(m_c, jnp.exp(s_c - jnp.max(s_c, axis=-1, keepdims=True)), 0.0)
    p_c = e_c * (1.0 / jnp.maximum(jnp.sum(e_c, axis=-1, keepdims=True), 1e-30))
    o_c = _dg(p_c.astype(BF16), vc_ref[...].astype(BF16), NN)

    p_sum = p_c[0:tq]
    for h in range(1, hp):
        p_sum = p_sum + p_c[h * tq:(h + 1) * tq]
    ss = lax.broadcasted_iota(jnp.int32, (n_s, n_cb), 0) * SEL_BLK
    cs = lax.broadcasted_iota(jnp.int32, (n_s, n_cb), 1) * CMP_STRIDE
    overlap_t = ((cs < ss + SEL_BLK) & (cs + (CMP_BLK - 1) >= ss)
                 & (cs + (CMP_BLK - 1) < seq_len)).astype(BF16)
    imp = _dot_exact_lhs(overlap_t, p_sum, NT)

    s_w = (s_w.reshape(hp, tq, wk) + bias_w[None]).reshape(hp * tq, wk)
    p_w = jnp.exp(s_w - jnp.max(s_w, axis=-1, keepdims=True))
    l_w = jnp.sum(p_w, axis=-1, keepdims=True)
    o_w = _dg(p_w.astype(BF16), vw_tile, NN) * (1.0 / jnp.maximum(l_w, 1e-30))

    blk = lax.broadcasted_iota(jnp.int32, (n_s, 1), 0)
    cur = t_lane // SEL_BLK
    valid = blk * SEL_BLK <= t_lane
    forced = valid & ((blk == 0) | (blk == cur) | (blk == cur - 1))
    score = jnp.where(forced, FORCED, jnp.where(valid, imp, NEG))
    rank = jnp.zeros((n_s, tq), jnp.int32)
    for s in range(n_s):
        row = score[s:s + 1, :]
        beats = (row > score) | ((row == score) & (blk > s))
        rank = rank + beats.astype(jnp.int32)
    sel_t = ((rank < k_sel) & (score > 0.5 * NEG)).astype(BF16)
    n_kt = (t0 + tq + tk - 1) // tk
    n_all = seq_len // tk
    half = max(n_all // 2, 1)

    def expand(kt0, kt1):
        width = (kt1 - kt0) * tk
        er = lax.broadcasted_iota(jnp.int32, (n_s, width), 0)
        ec = (lax.broadcasted_iota(jnp.int32, (n_s, width), 1) + kt0 * tk) // SEL_BLK
        chosen = _dg(sel_t, (er == ec).astype(BF16), TN)
        kpos = kt0 * tk + lax.broadcasted_iota(jnp.int32, (1, width), 1)
        bias = jnp.where((chosen > 0.5) & (kpos <= t_q), 0.0, NEG)
        for kt in range(kt0, kt1):
            bias_ref[kt] = bias[:, (kt - kt0) * tk:(kt - kt0 + 1) * tk]

    expand(0, half)
    if n_all > half:
        @pl.when(n_kt > half)
        def _():
            expand(half, n_all)

    def sel_body(kt, carry):
        m_run, l_run, acc = carry
        k0 = pl.multiple_of(kt * tk, tk)
        s = _dg(qa, ks_ref[pl.ds(k0, tk), :], NT)
        s = (s.reshape(hp, tq, tk) + bias_ref[kt][None]).reshape(hp * tq, tk)
        m_new = jnp.maximum(m_run, jnp.max(s, axis=-1, keepdims=True))
        alpha = jnp.exp(m_run - m_new)
        p = jnp.exp(s - m_new)
        l_new = alpha * l_run + jnp.sum(p, axis=-1, keepdims=True)
        acc = alpha * acc + _dg(p.astype(BF16), vs_ref[pl.ds(k0, tk), :], NN)
        return m_new, l_new, acc

    init = (jnp.full((hp * tq, 1), NEG, F32), jnp.zeros((hp * tq, 1), F32),
            jnp.zeros((hp * tq, NSA_DV), F32))
    _, l_s, acc_s = lax.fori_loop(0, n_kt, sel_body, init)
    o_s = acc_s * (1.0 / jnp.maximum(l_s, 1e-30))

    gates = _sigmoid(gt_ref[...])
    for h in range(hp):
        rows = slice(h * tq, (h + 1) * tq)
        out = (gates[:, 3 * h:3 * h + 1] * o_c[rows] + gates[:, 3 * h + 1:3 * h + 2] * o_s[rows]
               + gates[:, 3 * h + 2:3 * h + 3] * o_w[rows])
        o_ref[:, h * NSA_DV:(h + 1) * NSA_DV] = out.astype(o_ref.dtype)


def _nsa_attention(q, p_g, k_cmp, v_cmp, ks, vs, kw, vw, batch, seq_len):
    m = q.shape[1]
    tq = NSA_TQ
    assert seq_len >= WIN + tq and seq_len % NSA_TK == 0
    nq = seq_len // tq
    n_cb = k_cmp.shape[1]
    tk = min(NSA_TK, seq_len)
    kv = lambda b, g, i: (g * batch + b, 0, 0)
    return pl.pallas_call(
        functools.partial(_nsa_kernel, seq_len),
        grid=(batch, NSA_KV_GROUPS, nq),
        in_specs=[
            pl.BlockSpec((NSA_HPG, tq, NSA_DK), lambda b, g, i: (g, b * nq + i, 0)),
            pl.BlockSpec((None, tq, 3 * NSA_HPG), lambda b, g, i: (g, b * nq + i, 0)),
            pl.BlockSpec((None, n_cb, NSA_DK), kv),
            pl.BlockSpec((None, n_cb, NSA_DV), kv),
            pl.BlockSpec((None, seq_len, NSA_DK), kv),
            pl.BlockSpec((None, seq_len, NSA_DV), kv),
            pl.BlockSpec((None, seq_len, NSA_DK), kv),
            pl.BlockSpec((None, seq_len, NSA_DV), kv),
        ],
        out_specs=pl.BlockSpec((tq, NSA_HPG * NSA_DV), lambda b, g, i: (b * nq + i, g)),
        out_shape=jax.ShapeDtypeStruct((m, NSA_WIDTH), BF16),
        scratch_shapes=[pltpu.VMEM((seq_len // tk, tq, tk), F32)],
        compiler_params=_cparams(("parallel", "parallel", "arbitrary")),
        name="nsa_attention",
    )(q, p_g, k_cmp, v_cmp, ks, vs, kw, vw)


def _merge_kernel(y0_ref, y1_ref, y2_ref, g0_ref, g1_ref, g2_ref, w_ref, o_ref):
    acc = _sigmoid(g0_ref[...]) * _dg(y0_ref[...], w_ref[0], NN)
    acc = acc + _sigmoid(g1_ref[...]) * _dg(y1_ref[...], w_ref[1], NN)
    acc = acc + _sigmoid(g2_ref[...]) * _dg(y2_ref[...], w_ref[2], NN)
    o_ref[...] = acc.astype(o_ref.dtype)


def _merge(y_rw, y_gm, y_ns, p_gate, w_br):
    m, d = y_rw.shape
    tm = min(MERGE_TM, m)
    tn = MERGE_TN
    nj = d // tn
    ys = pl.BlockSpec((tm, d), lambda i, j: (i, 0))
    gspec = lambda br: pl.BlockSpec((tm, tn), lambda i, j: (i, br * nj + j))
    return pl.pallas_call(
        _merge_kernel,
        grid=(m // tm, nj),
        in_specs=[ys, ys, ys, gspec(0), gspec(1), gspec(2),
                  pl.BlockSpec((N_BRANCH, d, tn), lambda i, j: (0, 0, j))],
        out_specs=pl.BlockSpec((tm, tn), lambda i, j: (i, j)),
        out_shape=jax.ShapeDtypeStruct((m, d), BF16),
        compiler_params=_cparams(("parallel", "arbitrary")),
        name="branch_merge",
    )(y_rw, y_gm, y_ns, p_gate, p_gate, p_gate, w_br)


def _out_ln_kernel(y_ref, w_ref, x_ref, g_ref, b_ref, o_ref):
    y = ALPHA * x_ref[...] + _dg(y_ref[...], w_ref[...], NN)
    o_ref[...] = _layer_norm(y, g_ref[...], b_ref[...])


def _out_ln(merged, w_o, x, g, b):
    m, d = x.shape
    tm = min(WO_TM, m)
    row = lambda i: (i, 0)
    fix = lambda i: (0, 0)
    return pl.pallas_call(
        _out_ln_kernel,
        grid=(m // tm,),
        in_specs=[pl.BlockSpec((tm, d), row), pl.BlockSpec((d, d), fix), pl.BlockSpec((tm, d), row),
                  pl.BlockSpec((1, d), fix), pl.BlockSpec((1, d), fix)],
        out_specs=pl.BlockSpec((tm, d), row),
        out_shape=jax.ShapeDtypeStruct((m, d), F32),
        compiler_params=_cparams(("parallel",)),
        name="out_proj_ln",
    )(merged, w_o, x, g.reshape(1, d), b.reshape(1, d))


def _rwkv_branch(xb, wl, mu, w0, w2, a0, a2, g2, k_k, k_a, r_k, gn_g, gn_b, vres, batch, seq_len):
    p_rkv = _mm(xb, wl[:, :3 * RW_WIDTH].astype(BF16), F32)
    p_lora = _mm(xb, wl[:, 3 * RW_WIDTH:RW_COLS].astype(BF16), F32, tn=RW_LORA_COLS)
    r, k, v, w, a, g = _rw_prep(p_rkv, p_lora, mu, w0, a0, w2, a2, g2, vres, seq_len)
    y = _wkv(r, k, v, w, a, g, k_k, k_a, r_k.reshape(-1), gn_g, gn_b, batch, seq_len)
    return y, v


def _nsa_branch(xb, wl, pos_k, pos_v, phi_k1, phi_k2, phi_v1, phi_v2, batch, seq_len):
    q, kc, vc, ks, vs, kw, vw, p_g = _nsa_proj(xb, wl)
    gb = NSA_KV_GROUPS * batch
    per_seq = lambda z: z.reshape(gb, seq_len, z.shape[-1])
    k_cmp = _compress(per_seq(kc), pos_k, phi_k1, phi_k2)
    v_cmp = _compress(per_seq(vc), pos_v, phi_v1, phi_v2)
    return _nsa_attention(q, p_g, k_cmp, v_cmp, per_seq(ks), per_seq(vs), per_seq(kw), per_seq(vw),
                          batch, seq_len)


def kernel(x, w_in, rw_mu, rw_w0, rw_w2, rw_a0, rw_a2, rw_g2, rw_v0, rw_v1, rw_v2, rw_k_k, rw_k_a, rw_r_k, rw_gn_g, rw_gn_b, gm_ln_g, gm_ln_b, gm_ws, gm_bs, nsa_pos_k, nsa_pos_v, nsa_phi_k1, nsa_phi_k2, nsa_phi_v1, nsa_phi_v2, w_br, w_o, ffn1_wg, ffn1_wu, ffn1_wd, ffn2_wg, ffn2_wu, ffn2_wd, ln_g, ln_b):
    batch, seq_len, d = x.shape
    m = batch * seq_len
    h = x.reshape(m, d)
    v_first = None
    for l in range(DEPTH):
        h = _ffn(h, ffn1_wg[l].astype(BF16), ffn1_wu[l].astype(BF16), ffn1_wd[l].astype(BF16),
                 ln_g[l, 0], ln_b[l, 0])
        hb = h.astype(BF16)
        wl = w_in[l]
        vres = None if l == 0 else (v_first, rw_v0[l - 1], rw_v1[l - 1], rw_v2[l - 1])
        y_rw, v_out = _rwkv_branch(hb, wl, rw_mu[l], rw_w0[l], rw_w2[l], rw_a0[l], rw_a2[l], rw_g2[l],
                                   rw_k_k[l], rw_k_a[l], rw_r_k[l], rw_gn_g[l], rw_gn_b[l], vres,
                                   batch, seq_len)
        if l == 0:
            v_first = v_out
        p_gm = _mm(hb, wl[:, OFF_GM:OFF_NSA].astype(BF16), F32)
        y_gm = _gmlp(p_gm, gm_ln_g[l], gm_ln_b[l], gm_ws[l], gm_bs[l])
        y_ns = _nsa_branch(hb, wl, nsa_pos_k[l], nsa_pos_v[l], nsa_phi_k1[l], nsa_phi_k2[l],
                           nsa_phi_v1[l], nsa_phi_v2[l], batch, seq_len)
        p_gate = _mm(hb, wl[:, OFF_GATE:].astype(BF16), F32)
        merged = _merge(y_rw, y_gm, y_ns, p_gate, w_br[l].astype(BF16))
        h = _out_ln(merged, w_o[l].astype(BF16), h, ln_g[l, 1], ln_b[l, 1])
        h = _ffn(h, ffn2_wg[l].astype(BF16), ffn2_wu[l].astype(BF16), ffn2_wd[l].astype(BF16),
                 ln_g[l, 2], ln_b[l, 2])
    return h.reshape(batch, seq_len, d)
```

```python
import functools

import jax
import jax.numpy as jnp
from jax import lax
from jax.experimental import pallas as pl
from jax.experimental.pallas import tpu as pltpu

F32 = jnp.float32
BF16 = jnp.bfloat16

D_MODEL = 2048
DEPTH = 2
RW_HEAD = 64
RW_WIDTH = D_MODEL
RW_DECAY_LORA = 96
RW_AAA_LORA = 96
RW_MV_LORA = 64
RW_GATE_LORA = 256
RW_LORA_COLS = RW_DECAY_LORA + RW_AAA_LORA + RW_GATE_LORA
RW_GN_EPS = RW_HEAD * 1e-5
GM_WIDTH = D_MODEL
GM_CHUNK = 128
GM_GROUP = 128
GM_GROUPS = GM_WIDTH // GM_GROUP
NSA_HEADS = 16
NSA_KV_GROUPS = 4
NSA_HPG = NSA_HEADS // NSA_KV_GROUPS
NSA_DK = 192
NSA_DV = 128
NSA_WIDTH = NSA_HEADS * NSA_DV
CMP_BLK = 32
CMP_STRIDE = 16
SEL_BLK = 64
N_SEL = 16
WIN = 512
D_FF = 5632
N_BRANCH = 3
ALPHA = (2 * DEPTH) ** 0.25
LN_EPS = 1e-5
NEG = -1e30
FORCED = 1e6

RW_COLS = 3 * RW_WIDTH + RW_LORA_COLS
GM_COLS = 2 * GM_WIDTH
NSA_Q_COLS = NSA_HEADS * NSA_DK
NSA_GK = NSA_KV_GROUPS * NSA_DK
NSA_GV = NSA_KV_GROUPS * NSA_DV
NSA_KV_COLS = 3 * (NSA_GK + NSA_GV)
NSA_G_COLS = 3 * NSA_HEADS
NSA_COLS = NSA_Q_COLS + NSA_KV_COLS + NSA_G_COLS
OFF_GM = RW_COLS
OFF_NSA = OFF_GM + GM_COLS
OFF_GATE = OFF_NSA + NSA_COLS

V7X_VMEM_BYTES = 64 * 1024 * 1024
VMEM_LIMIT = 56 * 1024 * 1024

FFN_TM = 512
FFN_TF = 512
MM_TM = 1024
MM_TN = 512
MERGE_TM = 512
MERGE_TN = 512
WO_TM = 512
PREP_TM = 128
WKV_L = 64
WKV_TC = 512
WKV_PAIRS = 2
WKV_GROUPS = 2
GM_TM = 256
NSA_TQ = 128
NSA_TK = 512
NSA_GROWS = 16

NN = (((1,), (0,)), ((), ()))
NT = (((1,), (1,)), ((), ()))
TN = (((0,), (0,)), ((), ()))


def _cparams(sem):
    return pltpu.CompilerParams(dimension_semantics=sem, vmem_limit_bytes=VMEM_LIMIT)


def _dg(a, b, dims):
    return lax.dot_general(a, b, dims, preferred_element_type=F32)


def _split2(x):
    hi = x.astype(BF16)
    lo = (x - hi.astype(F32)).astype(BF16)
    return hi, lo


def _split3(x):
    x1 = x.astype(BF16)
    r1 = x - x1.astype(F32)
    x2 = r1.astype(BF16)
    x3 = (r1 - x2.astype(F32)).astype(BF16)
    return x1, x2, x3


def _dot3(a, b, dims=NN):
    ah, al = _split2(a)
    bh, bl = _split2(b)
    return _dg(ah, bh, dims) + (_dg(ah, bl, dims) + _dg(al, bh, dims))


def _dot_exact_rhs(a, m01, dims=NN):
    a1, a2, a3 = _split3(a)
    return _dg(a1, m01, dims) + (_dg(a2, m01, dims) + _dg(a3, m01, dims))


def _dot_exact_lhs(m01, b, dims=NN):
    b1, b2, b3 = _split3(b)
    return _dg(m01, b1, dims) + (_dg(m01, b2, dims) + _dg(m01, b3, dims))


def _layer_norm(y, g, b):
    mu = jnp.mean(y, axis=-1, keepdims=True)
    d = y - mu
    var = jnp.mean(d * d, axis=-1, keepdims=True)
    return d * lax.rsqrt(var + LN_EPS) * g + b


def _gelu_tanh(x):
    return 0.5 * x * (1.0 + jnp.tanh(0.7978845608028654 * (x + 0.044715 * (x * x * x))))


def _sigmoid(x):
    return 1.0 / (1.0 + jnp.exp(-x))


def _ffn_kernel(x_ref, wg_ref, wu_ref, wd_ref, g_ref, b_ref, o_ref, xb_ref):
    j = pl.program_id(1)

    @pl.when(j == 0)
    def _():
        o_ref[...] = jnp.zeros_like(o_ref)
        xb_ref[...] = x_ref[...].astype(BF16)

    xb = xb_ref[...]
    hg = _dg(xb, wg_ref[...], NN)
    hu = _dg(xb, wu_ref[...], NN)
    h = (hg * _sigmoid(hg) * hu).astype(BF16)
    o_ref[...] += _dg(h, wd_ref[...], NN)

    @pl.when(j == pl.num_programs(1) - 1)
    def _():
        y = ALPHA * x_ref[...] + 0.5 * o_ref[...]
        o_ref[...] = _layer_norm(y, g_ref[...], b_ref[...])


def _ffn(x, wg, wu, wd, g, b):
    m, d = x.shape
    ff = wg.shape[1]
    tm = min(FFN_TM, m)
    return pl.pallas_call(
        _ffn_kernel,
        grid=(m // tm, ff // FFN_TF),
        in_specs=[
            pl.BlockSpec((tm, d), lambda i, j: (i, 0)),
            pl.BlockSpec((d, FFN_TF), lambda i, j: (0, j)),
            pl.BlockSpec((d, FFN_TF), lambda i, j: (0, j)),
            pl.BlockSpec((FFN_TF, d), lambda i, j: (j, 0)),
            pl.BlockSpec((1, d), lambda i, j: (0, 0)),
            pl.BlockSpec((1, d), lambda i, j: (0, 0)),
        ],
        out_specs=pl.BlockSpec((tm, d), lambda i, j: (i, 0)),
        out_shape=jax.ShapeDtypeStruct((m, d), F32),
        scratch_shapes=[pltpu.VMEM((tm, d), BF16)],
        compiler_params=_cparams(("parallel", "arbitrary")),
        name="ffn_swiglu_ln",
    )(x, wg, wu, wd, g.reshape(1, d), b.reshape(1, d))


def _mm_kernel(x_ref, w_ref, o_ref):
    o_ref[...] = _dg(x_ref[...], w_ref[...], NN).astype(o_ref.dtype)


def _mm(x, w, out_dtype, tn=MM_TN):
    m, k = x.shape
    n = w.shape[1]
    tm = min(MM_TM, m)
    tn = min(tn, n)
    return pl.pallas_call(
        _mm_kernel,
        grid=(m // tm, n // tn),
        in_specs=[
            pl.BlockSpec((tm, k), lambda i, j: (i, 0)),
            pl.BlockSpec((k, tn), lambda i, j: (0, j)),
        ],
        out_specs=pl.BlockSpec((tm, tn), lambda i, j: (i, j)),
        out_shape=jax.ShapeDtypeStruct((m, n), out_dtype),
        compiler_params=_cparams(("parallel", "arbitrary")),
        name="proj_mm",
    )(x, w)


def _token_shift(x, prev_row, mu, first):
    rows = lax.broadcasted_iota(jnp.int32, x.shape, 0)
    prev_row = jnp.where(first, jnp.zeros_like(prev_row), prev_row)
    shifted = jnp.where(rows == 0, prev_row, pltpu.roll(x, 1, 0))
    return x + (shifted - x) * mu


def _rw_vlora_kernel(seq_len, p_ref, pp_ref, mu_ref, v1_ref, o_ref):
    first = (pl.program_id(0) * p_ref.shape[0]) % seq_len == 0
    v = _token_shift(p_ref[...], pp_ref[7:8, :], mu_ref[...], first)
    o_ref[...] = _dot3(v, v1_ref[...])


def _rw_vlora(p_rkv, mu_v, v1, seq_len):
    m = p_rkv.shape[0]
    tm = PREP_TM
    wd = RW_WIDTH
    return pl.pallas_call(
        functools.partial(_rw_vlora_kernel, seq_len),
        grid=(m // tm,),
        in_specs=[pl.BlockSpec((tm, wd), lambda i: (i, 2)),
                  pl.BlockSpec((8, wd), lambda i: (jnp.maximum(i * (tm // 8) - 1, 0), 2)),
                  pl.BlockSpec((1, wd), lambda i: (0, 0)),
                  pl.BlockSpec((wd, RW_MV_LORA), lambda i: (0, 0))],
        out_specs=pl.BlockSpec((tm, RW_MV_LORA), lambda i: (i, 0)),
        out_shape=jax.ShapeDtypeStruct((m, RW_MV_LORA), F32),
        compiler_params=_cparams(("parallel",)),
        name="rwkv_vlora",
    )(p_rkv, p_rkv, mu_v.reshape(1, wd), v1)


def _sp(x):
    return _split2(x)


def _d3s(a, b, dims=NN):
    return _dg(a[0], b[0], dims) + (_dg(a[0], b[1], dims) + _dg(a[1], b[0], dims))


def _d2m(a, m01, dims=NN):
    return _dg(a[0], m01, dims) + _dg(a[1], m01, dims)


def _wkv_kernel(has_vres, *refs):
    (pr_ref, pk_ref, pv_ref, ppr_ref, ppk_ref, ppv_ref, lo_ref, plo_ref,
     mur_ref, muk_ref, muv_ref, mul_ref, w0_ref, a0_ref, w2_ref, a2_ref, g2_ref,
     kk_ref, ka_ref, rk_ref, gg_ref, gb_ref) = refs[:22]
    if has_vres:
        vf_ref, v0_ref, vlo_ref, v2_ref, o_ref, s_ref = refs[22:]
    else:
        o_ref, vout_ref, s_ref = refs[22:]
    L = WKV_L
    n_groups = WKV_GROUPS
    rg = pr_ref.shape[0] // n_groups
    n_chunks = rg // L

    @pl.when(pl.program_id(2) == 0)
    def _():
        s_ref[...] = jnp.zeros_like(s_ref)

    ri = lax.broadcasted_iota(jnp.int32, (2 * L, 2 * L), 0)
    ci = lax.broadcasted_iota(jnp.int32, (2 * L, 2 * L), 1)
    bd_f = ((ri // L) == (ci // L)).astype(F32)
    bd_b = bd_f.astype(BF16)
    rl = lax.broadcasted_iota(jnp.int32, (L, L), 0)
    cl = lax.broadcasted_iota(jnp.int32, (L, L), 1)
    tri_b = (cl <= rl).astype(BF16)
    rt_i = lax.broadcasted_iota(jnp.int32, (L, 2 * L), 0)
    cs_i = lax.broadcasted_iota(jnp.int32, (L, 2 * L), 1) % L
    strict = cs_i < rt_i
    incl = cs_i <= rt_i

    bd2_b = jnp.concatenate([bd_b, bd_b], axis=1)
    eye_f = (ri == ci).astype(F32)

    def sm(zb, mask=bd_b):
        return jnp.concatenate([zb, zb], axis=0) * mask

    def b16(x):
        return x.astype(BF16)

    inv_n = 1.0 / RW_HEAD
    n_lvl = L.bit_length() - 1
    n_pairs = pr_ref.shape[1] // (2 * L)

    prepped = {}

    def prep(g):
        rs = slice(g * rg, (g + 1) * rg)
        if g == 0:
            first = pl.program_id(2) == 0
            prev = lambda p_ref, pp_ref: pp_ref[7:8, :]
        else:
            first = False
            prev = lambda p_ref, pp_ref: p_ref[g * rg - 1:g * rg, :]
        r_all = _token_shift(pr_ref[rs, :], prev(pr_ref, ppr_ref), mur_ref[...], first)
        yield
        k_all = _token_shift(pk_ref[rs, :], prev(pk_ref, ppk_ref), muk_ref[...], first)
        yield
        v_all = _token_shift(pv_ref[rs, :], prev(pv_ref, ppv_ref), muv_ref[...], first)
        yield
        lo = _token_shift(lo_ref[rs, :], prev(lo_ref, plo_ref), mul_ref[...], first)
        wl = lo[:, :RW_DECAY_LORA]
        al = lo[:, RW_DECAY_LORA:RW_DECAY_LORA + RW_AAA_LORA]
        gl = lo[:, RW_DECAY_LORA + RW_AAA_LORA:]
        yield
        z = w0_ref[...] + _dot3(jnp.tanh(wl), w2_ref[...])
        yield
        w_all = -(jnp.maximum(-z, 0.0) + jnp.log1p(jnp.exp(-jnp.abs(z)))) - 0.5
        yield
        a_all = _sigmoid(a0_ref[...] + _dot3(al, a2_ref[...]))
        yield
        g_all = _dot3(_sigmoid(gl), g2_ref[...])
        yield
        if has_vres:
            mix = _sigmoid(v0_ref[...] + _dot3(vlo_ref[rs, :], v2_ref[...]))
            v_all = v_all + (vf_ref[rs, :] - v_all) * mix
        else:
            vout_ref[rs, :] = v_all
        prepped[g] = (r_all, k_all, v_all, w_all, a_all, g_all)

    def drain(gen):
        for _ in gen:
            pass

    def run_group(g, state, pump):
        r_all, k_all, v_all, w_all, a_all, g_all = prepped.pop(g)
        C = range(n_chunks * n_pairs)
        rows = [slice((c // n_pairs) * L, (c // n_pairs + 1) * L) for c in C]
        lanes = [slice((c % n_pairs) * 2 * L, (c % n_pairs + 1) * 2 * L) for c in C]
        k_k = [kk_ref[:, lanes[c]] for c in C]
        k_a = [ka_ref[:, lanes[c]] for c in C]
        r_k = [rk_ref[:, lanes[c]] for c in C]
        gn_g = [gg_ref[:, lanes[c]] for c in C]
        gn_b = [gb_ref[:, lanes[c]] for c in C]
        r = [r_all[rows[c], lanes[c]] for c in C]
        k_raw = [k_all[rows[c], lanes[c]] for c in C]
        v = [v_all[rows[c], lanes[c]] for c in C]
        a = [a_all[rows[c], lanes[c]] for c in C]
        kk = [k_raw[c] * k_k[c] for c in C]
        ssq = [_d2m(_sp(kk[c] * kk[c]), bd_b) for c in C]
        lw = [-jnp.exp(w_all[rows[c], lanes[c]]) for c in C]
        cum = [_dot_exact_lhs(tri_b, lw[c]) for c in C]
        kk = [kk[c] * lax.rsqrt(jnp.maximum(ssq[c], 1e-24)) for c in C]
        k = [k_raw[c] * (1.0 + (a[c] - 1.0) * k_a[c]) for c in C]
        bv = [kk[c] * a[c] for c in C]
        tot = [cum[c][L - 1:L, :] for c in C]
        e_neg = [jnp.exp(-cum[c]) for c in C]
        rt = [r[c] * jnp.exp(cum[c]) for c in C]
        at = [-kk[c] * jnp.exp(cum[c] - lw[c]) for c in C]
        e_tot = [jnp.exp(tot[c] - cum[c]) for c in C]
        bh = [b16(bv[c] * e_tot[c]) for c in C]
        kh = [b16(k[c] * e_tot[c]) for c in C]
        vb = [b16(v[c]) for c in C]
        smv = [sm(vb[c]) for c in C]
        pump()
        lhs2 = [b16(jnp.concatenate([at[c], rt[c]], axis=0)) for c in C]
        smk = [sm(b16(k[c] * e_neg[c])) for c in C]
        smb = [sm(b16(bv[c] * e_neg[c])) for c in C]
        ak = [_dg(lhs2[c], smk[c], NT) for c in C]
        pump()
        ab = [_dg(lhs2[c], smb[c], NT) for c in C]
        a_ak = [b16(jnp.where(strict, ak[c][:L], 0.0)) for c in C]
        a_rk = [b16(jnp.where(incl, ak[c][L:], 0.0)) for c in C]
        p = [jnp.where(strict, ab[c][:L], 0.0) for c in C]
        a_rb = [b16(jnp.where(incl, ab[c][L:], 0.0)) for c in C]
        pump()
        akv = [_dg(a_ak[c], smv[c], NN) for c in C]
        z = [jnp.concatenate([at[c], akv[c]], axis=1) for c in C]
        for lvl in range(n_lvl):
            pb = [b16(p[c]) for c in C]
            smz = [sm(b16(z[c]), bd2_b) for c in C]
            z = [z[c] + _dg(pb[c], smz[c], NN) for c in C]
            pump()
            if lvl + 1 < n_lvl:
                smp = [sm(pb[c]) for c in C]
                p = [_dg(pb[c], smp[c], NN) for c in C]
        pump()
        zb = [b16(z[c]) for c in C]
        at2 = [zb[c][:, :2 * L] for c in C]
        u0 = [zb[c][:, 2 * L:] for c in C]
        q = [b16(rt[c] + _dg(a_rb[c], sm(at2[c]), NN)) for c in C]
        y0 = [_dg(a_rk[c], smv[c], NN) + _dg(a_rb[c], sm(u0[c]), NN) for c in C]
        pump()
        gam_s = [_sp(bd_f * _dg(at2[c], bh[c], TN) + eye_f * jnp.exp(tot[c])) for c in C]
        cc = [bd_f * (_dg(vb[c], kh[c], TN) + _dg(u0[c], bh[c], TN)) for c in C]
        bonus = [_d2m(_sp(r[c] * k[c] * r_k[c]), bd_b) * v[c] for c in C]

        y = []
        for c in C:
            pi = c % n_pairs
            st_s = _sp(state[pi])
            y.append(_dg(q[c], st_s[0], NT) + y0[c])
            state[pi] = _d3s(st_s, gam_s[c]) + cc[c]
        mean = [_d2m(_sp(y[c]), bd_b) * inv_n for c in C]
        d = [y[c] - mean[c] for c in C]
        var = [_d2m(_sp(d[c] * d[c]), bd_b) * inv_n for c in C]
        for c in C:
            yn = d[c] * lax.rsqrt(var[c] + RW_GN_EPS) * gn_g[c] + gn_b[c]
            o_ref[pl.ds(g * rg + rows[c].start, L), lanes[c]] = ((yn + bonus[c]) * g_all[rows[c], lanes[c]]).astype(o_ref.dtype)
        return state

    state = [s_ref[pi] for pi in range(n_pairs)]
    drain(prep(0))
    for g in range(n_groups):
        nxt = prep(g + 1) if g + 1 < n_groups else iter(())
        state = run_group(g, state, lambda: next(nxt, None))
        drain(nxt)
    for pi in range(n_pairs):
        s_ref[pi] = state[pi]


def _wkv(p_rkv, p_lora, mu, w0, a0, w2, a2, g2, k_k, k_a, r_k, gn_g, gn_b, vres, batch, seq_len):
    m = p_rkv.shape[0]
    wd = RW_WIDTH
    tc = min(WKV_TC, seq_len)
    nt = seq_len // tc
    pair = 2 * RW_HEAD
    lanes = WKV_PAIRS * pair
    nb = wd // lanes
    row = lambda b, h, t: b * nt + t
    prv = lambda b, h, t: jnp.maximum((b * nt + t) * (tc // 8) - 1, 0)
    act = lambda sec: pl.BlockSpec((tc, lanes), lambda b, h, t: (row(b, h, t), sec * nb + h))
    prev = lambda sec: pl.BlockSpec((8, lanes), lambda b, h, t: (prv(b, h, t), sec * nb + h))
    par = lambda sec=0: pl.BlockSpec((1, lanes), lambda b, h, t: (0, sec * nb + h))
    mat = lambda k: pl.BlockSpec((k, lanes), lambda b, h, t: (0, h))
    lo_w = RW_LORA_COLS
    c3 = 3 * wd
    in_specs = [act(0), act(1), act(2), prev(0), prev(1), prev(2),
                pl.BlockSpec((tc, lo_w), lambda b, h, t: (row(b, h, t), 0)),
                pl.BlockSpec((8, lo_w), lambda b, h, t: (prv(b, h, t), 0)),
                par(0), par(1), par(2), pl.BlockSpec((1, lo_w), lambda b, h, t: (0, 0)),
                par(), par(), mat(RW_DECAY_LORA), mat(RW_AAA_LORA), mat(RW_GATE_LORA)] + [par()] * 5
    mu_rkv = mu[:c3].reshape(1, c3)
    args = [p_rkv, p_rkv, p_rkv, p_rkv, p_rkv, p_rkv, p_lora, p_lora,
            mu_rkv, mu_rkv, mu_rkv, mu[c3:].reshape(1, lo_w),
            w0.reshape(1, wd), a0.reshape(1, wd), w2, a2, g2]
    args += [z.reshape(1, wd) for z in (k_k, k_a, r_k, gn_g, gn_b)]
    out_act = pl.BlockSpec((tc, lanes), lambda b, h, t: (row(b, h, t), h))
    if vres is not None:
        v_first, v0, vlo, v2 = vres
        in_specs += [out_act, par(), pl.BlockSpec((tc, RW_MV_LORA), lambda b, h, t: (row(b, h, t), 0)),
                     mat(RW_MV_LORA)]
        args += [v_first, v0.reshape(1, wd), vlo, v2]
        out_specs = out_act
        out_shape = jax.ShapeDtypeStruct((m, wd), BF16)
    else:
        out_specs = [out_act, out_act]
        out_shape = [jax.ShapeDtypeStruct((m, wd), BF16), jax.ShapeDtypeStruct((m, wd), F32)]
    return pl.pallas_call(
        functools.partial(_wkv_kernel, vres is not None),
        grid=(batch, nb, nt),
        in_specs=in_specs,
        out_specs=out_specs,
        out_shape=out_shape,
        scratch_shapes=[pltpu.VMEM((WKV_PAIRS, pair, pair), F32)],
        compiler_params=_cparams(("parallel", "parallel", "arbitrary")),
        name="rwkv_wkv",
    )(*args)


def _gmlp_kernel(p_ref, lg_ref, lb_ref, ws_ref, bst_ref, o_ref):
    tm = p_ref.shape[0]
    u = _gelu_tanh(p_ref[:, :GM_WIDTH])
    v = _layer_norm(_gelu_tanh(p_ref[:, GM_WIDTH:]), lg_ref[...], lb_ref[...])
    rl = lax.broadcasted_iota(jnp.int32, (GM_CHUNK, GM_CHUNK), 0)
    cl = lax.broadcasted_iota(jnp.int32, (GM_CHUNK, GM_CHUNK), 1)
    causal = cl <= rl
    for g in range(GM_GROUPS):
        cols = slice(g * GM_GROUP, (g + 1) * GM_GROUP)
        wsg = jnp.where(causal, ws_ref[g], 0.0)
        bias = bst_ref[:, g:g + 1]
        for c in range(tm // GM_CHUNK):
            rows = slice(c * GM_CHUNK, (c + 1) * GM_CHUNK)
            s = _dot3(wsg, v[rows, cols]) + bias
            o_ref[rows, cols] = (u[rows, cols] * s).astype(o_ref.dtype)


def _gmlp(p_gm, ln_g, ln_b, ws, bs):
    m = p_gm.shape[0]
    tm = GM_TM
    fix2 = lambda i: (0, 0)
    return pl.pallas_call(
        _gmlp_kernel,
        grid=(m // tm,),
        in_specs=[
            pl.BlockSpec((tm, GM_COLS), lambda i: (i, 0)),
            pl.BlockSpec((1, GM_WIDTH), fix2),
            pl.BlockSpec((1, GM_WIDTH), fix2),
            pl.BlockSpec((GM_GROUPS, GM_CHUNK, GM_CHUNK), lambda i: (0, 0, 0)),
            pl.BlockSpec((GM_CHUNK, GM_GROUPS), fix2),
        ],
        out_specs=pl.BlockSpec((tm, GM_WIDTH), lambda i: (i, 0)),
        out_shape=jax.ShapeDtypeStruct((m, GM_WIDTH), BF16),
        compiler_params=_cparams(("parallel",)),
        name="gmlp_mix",
    )(p_gm, ln_g.reshape(1, -1), ln_b.reshape(1, -1), ws, bs.T)


def _compress_kernel(c_ref, pos_ref, w1_ref, w2_ref, o_ref):
    c = c_ref[...]
    n = c.shape[0]
    r1 = _dot3(c, w1_ref[0])
    r2 = _dot3(c, w1_ref[1])
    pos = jnp.broadcast_to(pos_ref[...], (8, pos_ref.shape[1]))
    half = w1_ref.shape[1]
    pterm = (_dot3(pos[:, :half], w1_ref[0]) + _dot3(pos[:, half:], w1_ref[1]))[0:1, :]
    h = r1 + pltpu.roll(r2, n - 1, 0) + pterm
    o_ref[...] = _dot3(_gelu_tanh(h), w2_ref[...])


def _compress(z, pos, w1, w2):
    gb, t, d = z.shape
    nb = t // CMP_STRIDE
    c = z.reshape(gb, nb, CMP_STRIDE * d)
    half = CMP_STRIDE * d
    return pl.pallas_call(
        _compress_kernel,
        grid=(gb,),
        in_specs=[
            pl.BlockSpec((None, nb, half), lambda i: (i, 0, 0)),
            pl.BlockSpec((1, 2 * half), lambda i: (0, 0)),
            pl.BlockSpec((2, half, d), lambda i: (0, 0, 0)),
            pl.BlockSpec((d, d), lambda i: (0, 0)),
        ],
        out_specs=pl.BlockSpec((None, nb, d), lambda i: (i, 0, 0)),
        out_shape=jax.ShapeDtypeStruct((gb, nb, d), F32),
        compiler_params=_cparams(("parallel",)),
        name="nsa_compress",
    )(c, pos.reshape(1, 2 * half), w1.reshape(2, half, d), w2)


def _nsa_proj_kernel(x_ref, wq_ref, wkc_ref, wvc_ref, wks_ref, wvs_ref, wkw_ref, wvw_ref, wgt_ref,
                     q_ref, kc_ref, vc_ref, ks_ref, vs_ref, kw_ref, vw_ref, gt_ref):
    x = x_ref[...]
    scale = NSA_DK ** -0.5
    for h in range(NSA_HPG):
        q_ref[h] = (_dg(x, wq_ref[h], NN) * scale).astype(q_ref.dtype)
    for w_ref, o_ref in ((wkc_ref, kc_ref), (wvc_ref, vc_ref), (wks_ref, ks_ref), (wvs_ref, vs_ref),
                         (wkw_ref, kw_ref), (wvw_ref, vw_ref)):
        o_ref[...] = _dg(x, w_ref[...], NN).astype(o_ref.dtype)
    gt_ref[...] = _dg(wgt_ref[...], x, NT)


def _nsa_proj(xb, wl):
    m, kdim = xb.shape
    G, hp = NSA_KV_GROUPS, NSA_HPG
    tm = min(MM_TM, m)
    o = OFF_NSA

    def take(width, d):
        nonlocal o
        w3 = wl[:, o:o + width].astype(BF16).reshape(kdim, width // d, d).transpose(1, 0, 2)
        o += width
        return w3

    wq = take(NSA_Q_COLS, NSA_DK)
    ws = [take(NSA_GK, NSA_DK), take(NSA_GV, NSA_DV), take(NSA_GK, NSA_DK), take(NSA_GV, NSA_DV),
          take(NSA_GK, NSA_DK), take(NSA_GV, NSA_DV)]
    wgt = take(NSA_G_COLS, 3 * hp).transpose(0, 2, 1)
    wgt = jnp.pad(wgt, ((0, 0), (0, NSA_GROWS - 3 * hp), (0, 0)))
    dts = [F32, F32, BF16, BF16, BF16, BF16]
    wspec = lambda d: pl.BlockSpec((None, kdim, d), lambda i, g: (g, 0, 0))
    ospec = lambda d: pl.BlockSpec((None, tm, d), lambda i, g: (g, i, 0))
    return pl.pallas_call(
        _nsa_proj_kernel,
        grid=(m // tm, G),
        in_specs=[pl.BlockSpec((tm, kdim), lambda i, g: (i, 0)),
                  pl.BlockSpec((hp, kdim, NSA_DK), lambda i, g: (g, 0, 0))]
                 + [wspec(w.shape[2]) for w in ws]
                 + [pl.BlockSpec((None, NSA_GROWS, kdim), lambda i, g: (g, 0, 0))],
        out_specs=[pl.BlockSpec((hp, tm, NSA_DK), lambda i, g: (g, i, 0))]
                  + [ospec(w.shape[2]) for w in ws]
                  + [pl.BlockSpec((None, NSA_GROWS, tm), lambda i, g: (g, 0, i))],
        out_shape=[jax.ShapeDtypeStruct((G * hp, m, NSA_DK), BF16)]
                  + [jax.ShapeDtypeStruct((G, m, w.shape[2]), dt) for w, dt in zip(ws, dts)]
                  + [jax.ShapeDtypeStruct((G, NSA_GROWS, m), F32)],
        compiler_params=_cparams(("parallel", "arbitrary")),
        name="nsa_proj",
    )(xb, wq, *ws, wgt)


def _nsa_kernel(seq_len, q_ref, gt_ref, kc_ref, vc_ref, ks_ref, vs_ref, kw_ref, vw_ref,
                o_ref, bias_ref):
    tq = NSA_TQ
    hp = NSA_HPG
    nr = hp * tq
    tk = min(NSA_TK, seq_len)
    wk = WIN + tq
    n_s = seq_len // SEL_BLK
    k_sel = min(N_SEL, n_s)
    n_cb = kc_ref.shape[0]
    i = pl.program_id(2)
    t0 = i * tq

    qa = q_ref[...].reshape(nr, NSA_DK)
    t_lane = t0 + lax.broadcasted_iota(jnp.int32, (1, tq), 1)
    t_all = t0 + (lax.broadcasted_iota(jnp.int32, (1, nr), 1) & (tq - 1))
    tile_heads = lambda z: jnp.concatenate([z] * hp, axis=1)

    w0 = pl.multiple_of(jnp.maximum(t0 - WIN, 0), tq)
    kw_tile = kw_ref[pl.ds(w0, wk), :]
    vw_tile = vw_ref[pl.ds(w0, wk), :]
    s_w = _dg(kw_tile, qa, NT)
    s_c = _dg(kc_ref[...].astype(BF16), qa, NT)

    n_end = lax.broadcasted_iota(jnp.int32, (n_cb, 1), 0) * CMP_STRIDE + (CMP_BLK - 1)
    m_c = (n_end <= t_all) & (n_end < seq_len)
    s_c = jnp.where(m_c, s_c, NEG)
    e_c = jnp.where(m_c, jnp.exp(s_c - jnp.max(s_c, axis=0, keepdims=True)), 0.0)
    p_c = e_c * (1.0 / jnp.maximum(jnp.sum(e_c, axis=0, keepdims=True), 1e-30))
    o_c = _dg(vc_ref[...].astype(BF16), p_c.astype(BF16), TN)

    p_sum = p_c[:, 0:tq]
    for h in range(1, hp):
        p_sum = p_sum + p_c[:, h * tq:(h + 1) * tq]
    ss = lax.broadcasted_iota(jnp.int32, (n_s, n_cb), 0) * SEL_BLK
    cs = lax.broadcasted_iota(jnp.int32, (n_s, n_cb), 1) * CMP_STRIDE
    overlap_t = ((cs < ss + SEL_BLK) & (cs + (CMP_BLK - 1) >= ss)
                 & (cs + (CMP_BLK - 1) < seq_len)).astype(BF16)
    imp = _dot_exact_lhs(overlap_t, p_sum, NN)

    kpos_w = w0 + lax.broadcasted_iota(jnp.int32, (wk, 1), 0)
    bias_w = jnp.where((kpos_w <= t_lane) & (kpos_w > t_lane - WIN), 0.0, NEG)
    s_w = s_w + tile_heads(bias_w)
    p_w = jnp.exp(s_w - jnp.max(s_w, axis=0, keepdims=True))
    l_w = jnp.sum(p_w, axis=0, keepdims=True)
    o_w = _dg(vw_tile, p_w.astype(BF16), TN) * (1.0 / jnp.maximum(l_w, 1e-30))

    blk = lax.broadcasted_iota(jnp.int32, (n_s, 1), 0)
    cur = t_lane // SEL_BLK
    valid = blk * SEL_BLK <= t_lane
    forced = valid & ((blk == 0) | (blk == cur) | (blk == cur - 1))
    score = jnp.where(forced, FORCED, jnp.where(valid, imp, NEG))
    rank = jnp.zeros((n_s, tq), jnp.int32)
    for s in range(n_s):
        row = score[s:s + 1, :]
        beats = (row > score) | ((row == score) & (blk > s))
        rank = rank + beats.astype(jnp.int32)
    sel_t = ((rank < k_sel) & (score > 0.5 * NEG)).astype(BF16)
    n_kt = (t0 + tq + tk - 1) // tk
    n_all = seq_len // tk
    half = max(n_all // 2, 1)

    def expand(kt0, kt1):
        width = (kt1 - kt0) * tk
        kpos = kt0 * tk + lax.broadcasted_iota(jnp.int32, (width, 1), 0)
        er = (lax.broadcasted_iota(jnp.int32, (width, n_s), 0) + kt0 * tk) // SEL_BLK
        ec = lax.broadcasted_iota(jnp.int32, (width, n_s), 1)
        chosen = _dg((er == ec).astype(BF16), sel_t, NN)
        bias = jnp.where((chosen > 0.5) & (kpos <= t_lane), 0.0, NEG)
        for kt in range(kt0, kt1):
            bias_ref[kt] = bias[(kt - kt0) * tk:(kt - kt0 + 1) * tk]

    expand(0, half)
    if n_all > half:
        @pl.when(n_kt > half)
        def _():
            expand(half, n_all)

    def sel_body(kt, carry):
        m_run, l_run, acc = carry
        k0 = pl.multiple_of(kt * tk, tk)
        s = _dg(ks_ref[pl.ds(k0, tk), :], qa, NT) + tile_heads(bias_ref[kt])
        m_new = jnp.maximum(m_run, jnp.max(s, axis=0, keepdims=True))
        alpha = jnp.exp(m_run - m_new)
        p = jnp.exp(s - m_new)
        l_new = alpha * l_run + jnp.sum(p, axis=0, keepdims=True)
        acc = alpha * acc + _dg(vs_ref[pl.ds(k0, tk), :], p.astype(BF16), TN)
        return m_new, l_new, acc

    init = (jnp.full((1, nr), NEG, F32), jnp.zeros((1, nr), F32), jnp.zeros((NSA_DV, nr), F32))
    _, l_s, acc_s = lax.fori_loop(0, n_kt, sel_body, init)
    o_s = acc_s * (1.0 / jnp.maximum(l_s, 1e-30))

    gates = _sigmoid(gt_ref[...])
    for h in range(hp):
        lanes = slice(h * tq, (h + 1) * tq)
        out_t = (gates[3 * h:3 * h + 1] * o_c[:, lanes] + gates[3 * h + 1:3 * h + 2] * o_s[:, lanes]
                 + gates[3 * h + 2:3 * h + 3] * o_w[:, lanes])
        o_ref[:, h * NSA_DV:(h + 1) * NSA_DV] = out_t.T.astype(o_ref.dtype)


def _nsa_attention(q, p_gt, k_cmp, v_cmp, ks, vs, kw, vw, batch, seq_len):
    m = q.shape[1]
    tq = NSA_TQ
    assert seq_len >= WIN + tq and seq_len % NSA_TK == 0
    nq = seq_len // tq
    n_cb = k_cmp.shape[1]
    tk = min(NSA_TK, seq_len)
    kv = lambda b, g, i: (g * batch + b, 0, 0)
    return pl.pallas_call(
        functools.partial(_nsa_kernel, seq_len),
        grid=(batch, NSA_KV_GROUPS, nq),
        in_specs=[
            pl.BlockSpec((NSA_HPG, tq, NSA_DK), lambda b, g, i: (g, b * nq + i, 0)),
            pl.BlockSpec((None, NSA_GROWS, tq), lambda b, g, i: (g, 0, b * nq + i)),
            pl.BlockSpec((None, n_cb, NSA_DK), kv),
            pl.BlockSpec((None, n_cb, NSA_DV), kv),
            pl.BlockSpec((None, seq_len, NSA_DK), kv),
            pl.BlockSpec((None, seq_len, NSA_DV), kv),
            pl.BlockSpec((None, seq_len, NSA_DK), kv),
            pl.BlockSpec((None, seq_len, NSA_DV), kv),
        ],
        out_specs=pl.BlockSpec((tq, NSA_HPG * NSA_DV), lambda b, g, i: (b * nq + i, g)),
        out_shape=jax.ShapeDtypeStruct((m, NSA_WIDTH), BF16),
        scratch_shapes=[pltpu.VMEM((seq_len // tk, tk, tq), F32)],
        compiler_params=_cparams(("parallel", "parallel", "arbitrary")),
        name="nsa_attention",
    )(q, p_gt, k_cmp, v_cmp, ks, vs, kw, vw)


def _merge_kernel(y0_ref, y1_ref, y2_ref, g0_ref, g1_ref, g2_ref, w_ref, o_ref):
    acc = _sigmoid(g0_ref[...]) * _dg(y0_ref[...], w_ref[0], NN)
    acc = acc + _sigmoid(g1_ref[...]) * _dg(y1_ref[...], w_ref[1], NN)
    acc = acc + _sigmoid(g2_ref[...]) * _dg(y2_ref[...], w_ref[2], NN)
    o_ref[...] = acc.astype(o_ref.dtype)


def _merge(y_rw, y_gm, y_ns, p_gate, w_br):
    m, d = y_rw.shape
    tm = min(MERGE_TM, m)
    tn = MERGE_TN
    nj = d // tn
    ys = pl.BlockSpec((tm, d), lambda i, j: (i, 0))
    gspec = lambda br: pl.BlockSpec((tm, tn), lambda i, j: (i, br * nj + j))
    return pl.pallas_call(
        _merge_kernel,
        grid=(m // tm, nj),
        in_specs=[ys, ys, ys, gspec(0), gspec(1), gspec(2),
                  pl.BlockSpec((N_BRANCH, d, tn), lambda i, j: (0, 0, j))],
        out_specs=pl.BlockSpec((tm, tn), lambda i, j: (i, j)),
        out_shape=jax.ShapeDtypeStruct((m, d), BF16),
        compiler_params=_cparams(("parallel", "arbitrary")),
        name="branch_merge",
    )(y_rw, y_gm, y_ns, p_gate, p_gate, p_gate, w_br)


def _out_ln_kernel(y_ref, w_ref, x_ref, g_ref, b_ref, o_ref):
    y = ALPHA * x_ref[...] + _dg(y_ref[...], w_ref[...], NN)
    o_ref[...] = _layer_norm(y, g_ref[...], b_ref[...])


def _out_ln(merged, w_o, x, g, b):
    m, d = x.shape
    tm = min(WO_TM, m)
    row = lambda i: (i, 0)
    fix = lambda i: (0, 0)
    return pl.pallas_call(
        _out_ln_kernel,
        grid=(m // tm,),
        in_specs=[pl.BlockSpec((tm, d), row), pl.BlockSpec((d, d), fix), pl.BlockSpec((tm, d), row),
                  pl.BlockSpec((1, d), fix), pl.BlockSpec((1, d), fix)],
        out_specs=pl.BlockSpec((tm, d), row),
        out_shape=jax.ShapeDtypeStruct((m, d), F32),
        compiler_params=_cparams(("parallel",)),
        name="out_proj_ln",
    )(merged, w_o, x, g.reshape(1, d), b.reshape(1, d))


def _rwkv_branch(xb, wl, mu, w0, w2, a0, a2, g2, k_k, k_a, r_k, gn_g, gn_b, vres, batch, seq_len):
    p_rkv = _mm(xb, wl[:, :3 * RW_WIDTH].astype(BF16), F32)
    p_lora = _mm(xb, wl[:, 3 * RW_WIDTH:RW_COLS].astype(BF16), F32, tn=RW_LORA_COLS)
    if vres is None:
        return _wkv(p_rkv, p_lora, mu, w0, a0, w2, a2, g2, k_k, k_a, r_k.reshape(-1), gn_g, gn_b,
                    None, batch, seq_len)
    v_first, v0, v1, v2 = vres
    vlo = _rw_vlora(p_rkv, mu[2 * RW_WIDTH:3 * RW_WIDTH], v1, seq_len)
    y = _wkv(p_rkv, p_lora, mu, w0, a0, w2, a2, g2, k_k, k_a, r_k.reshape(-1), gn_g, gn_b,
             (v_first, v0, vlo, v2), batch, seq_len)
    return y, v_first


def _nsa_branch(xb, wl, pos_k, pos_v, phi_k1, phi_k2, phi_v1, phi_v2, batch, seq_len):
    q, kc, vc, ks, vs, kw, vw, p_g = _nsa_proj(xb, wl)
    gb = NSA_KV_GROUPS * batch
    per_seq = lambda z: z.reshape(gb, seq_len, z.shape[-1])
    k_cmp = _compress(per_seq(kc), pos_k, phi_k1, phi_k2)
    v_cmp = _compress(per_seq(vc), pos_v, phi_v1, phi_v2)
    return _nsa_attention(q, p_g, k_cmp, v_cmp, per_seq(ks), per_seq(vs), per_seq(kw), per_seq(vw),
                          batch, seq_len)


def kernel(x, w_in, rw_mu, rw_w0, rw_w2, rw_a0, rw_a2, rw_g2, rw_v0, rw_v1, rw_v2, rw_k_k, rw_k_a, rw_r_k, rw_gn_g, rw_gn_b, gm_ln_g, gm_ln_b, gm_ws, gm_bs, nsa_pos_k, nsa_pos_v, nsa_phi_k1, nsa_phi_k2, nsa_phi_v1, nsa_phi_v2, w_br, w_o, ffn1_wg, ffn1_wu, ffn1_wd, ffn2_wg, ffn2_wu, ffn2_wd, ln_g, ln_b):
    batch, seq_len, d = x.shape
    m = batch * seq_len
    h = x.reshape(m, d)
    v_first = None
    for l in range(DEPTH):
        h = _ffn(h, ffn1_wg[l].astype(BF16), ffn1_wu[l].astype(BF16), ffn1_wd[l].astype(BF16),
                 ln_g[l, 0], ln_b[l, 0])
        hb = h.astype(BF16)
        wl = w_in[l]
        vres = None if l == 0 else (v_first, rw_v0[l - 1], rw_v1[l - 1], rw_v2[l - 1])
        y_rw, v_out = _rwkv_branch(hb, wl, rw_mu[l], rw_w0[l], rw_w2[l], rw_a0[l], rw_a2[l], rw_g2[l],
                                   rw_k_k[l], rw_k_a[l], rw_r_k[l], rw_gn_g[l], rw_gn_b[l], vres,
                                   batch, seq_len)
        if l == 0:
            v_first = v_out
        p_gm = _mm(hb, wl[:, OFF_GM:OFF_NSA].astype(BF16), F32)
        y_gm = _gmlp(p_gm, gm_ln_g[l], gm_ln_b[l], gm_ws[l], gm_bs[l])
        y_ns = _nsa_branch(hb, wl, nsa_pos_k[l], nsa_pos_v[l], nsa_phi_k1[l], nsa_phi_k2[l],
                           nsa_phi_v1[l], nsa_phi_v2[l], batch, seq_len)
        p_gate = _mm(hb, wl[:, OFF_GATE:].astype(BF16), F32)
        merged = _merge(y_rw, y_gm, y_ns, p_gate, w_br[l].astype(BF16))
        h = _out_ln(merged, w_o[l].astype(BF16), h, ln_g[l, 1], ln_b[l, 1])
        h = _ffn(h, ffn2_wg[l].astype(BF16), ffn2_wu[l].astype(BF16), ffn2_wd[l].astype(BF16),
                 ln_g[l, 2], ln_b[l, 2])
    return h.reshape(batch, seq_len, d)
```

```python
import functools

import jax
import jax.numpy as jnp
from jax import lax
from jax.experimental import pallas as pl
from jax.experimental.pallas import tpu as pltpu

F32 = jnp.float32
BF16 = jnp.bfloat16

D_MODEL = 2048
DEPTH = 2
RW_HEAD = 64
RW_WIDTH = D_MODEL
RW_DECAY_LORA = 96
RW_AAA_LORA = 96
RW_MV_LORA = 64
RW_GATE_LORA = 256
RW_LORA_COLS = RW_DECAY_LORA + RW_AAA_LORA + RW_GATE_LORA
RW_GN_EPS = RW_HEAD * 1e-5
GM_WIDTH = D_MODEL
GM_CHUNK = 128
GM_GROUP = 128
GM_GROUPS = GM_WIDTH // GM_GROUP
NSA_HEADS = 16
NSA_KV_GROUPS = 4
NSA_HPG = NSA_HEADS // NSA_KV_GROUPS
NSA_DK = 192
NSA_DV = 128
NSA_WIDTH = NSA_HEADS * NSA_DV
CMP_BLK = 32
CMP_STRIDE = 16
SEL_BLK = 64
N_SEL = 16
WIN = 512
D_FF = 5632
N_BRANCH = 3
ALPHA = (2 * DEPTH) ** 0.25
LN_EPS = 1e-5
NEG = -1e30
FORCED = 1e6

RW_COLS = 3 * RW_WIDTH + RW_LORA_COLS
GM_COLS = 2 * GM_WIDTH
NSA_Q_COLS = NSA_HEADS * NSA_DK
NSA_GK = NSA_KV_GROUPS * NSA_DK
NSA_GV = NSA_KV_GROUPS * NSA_DV
NSA_KV_COLS = 3 * (NSA_GK + NSA_GV)
NSA_G_COLS = 3 * NSA_HEADS
NSA_COLS = NSA_Q_COLS + NSA_KV_COLS + NSA_G_COLS
OFF_GM = RW_COLS
OFF_NSA = OFF_GM + GM_COLS
OFF_GATE = OFF_NSA + NSA_COLS

V7X_VMEM_BYTES = 64 * 1024 * 1024
VMEM_LIMIT = 56 * 1024 * 1024

FFN_TM = 512
FFN_TF = 512
MM_TM = 1024
MM_TN = 512
MERGE_TM = 512
MERGE_TN = 512
WO_TM = 512
PREP_TM = 128
WKV_L = 64
WKV_TC = 512
WKV_PAIRS = 2
WKV_GROUPS = 2
GM_TM = 256
NSA_TQ = 128
NSA_TK = 512
NSA_GROWS = 16

NN = (((1,), (0,)), ((), ()))
NT = (((1,), (1,)), ((), ()))
TN = (((0,), (0,)), ((), ()))


def _cparams(sem):
    return pltpu.CompilerParams(dimension_semantics=sem, vmem_limit_bytes=VMEM_LIMIT)


def _dg(a, b, dims):
    return lax.dot_general(a, b, dims, preferred_element_type=F32)


def _split2(x):
    hi = x.astype(BF16)
    lo = (x - hi.astype(F32)).astype(BF16)
    return hi, lo


def _split3(x):
    x1 = x.astype(BF16)
    r1 = x - x1.astype(F32)
    x2 = r1.astype(BF16)
    x3 = (r1 - x2.astype(F32)).astype(BF16)
    return x1, x2, x3


def _dot3(a, b, dims=NN):
    ah, al = _split2(a)
    bh, bl = _split2(b)
    return _dg(ah, bh, dims) + (_dg(ah, bl, dims) + _dg(al, bh, dims))


def _dot_exact_rhs(a, m01, dims=NN):
    a1, a2, a3 = _split3(a)
    return _dg(a1, m01, dims) + (_dg(a2, m01, dims) + _dg(a3, m01, dims))


def _dot_exact_lhs(m01, b, dims=NN):
    b1, b2, b3 = _split3(b)
    return _dg(m01, b1, dims) + (_dg(m01, b2, dims) + _dg(m01, b3, dims))


def _layer_norm(y, g, b):
    mu = jnp.mean(y, axis=-1, keepdims=True)
    d = y - mu
    var = jnp.mean(d * d, axis=-1, keepdims=True)
    return d * lax.rsqrt(var + LN_EPS) * g + b


def _gelu_tanh(x):
    return 0.5 * x * (1.0 + jnp.tanh(0.7978845608028654 * (x + 0.044715 * (x * x * x))))


def _sigmoid(x):
    return 1.0 / (1.0 + jnp.exp(-x))


def _ffn_kernel(x_ref, wg_ref, wu_ref, wd_ref, g_ref, b_ref, o_ref, xb_ref):
    j = pl.program_id(1)

    @pl.when(j == 0)
    def _():
        o_ref[...] = jnp.zeros_like(o_ref)
        xb_ref[...] = x_ref[...].astype(BF16)

    xb = xb_ref[...]
    hg = _dg(xb, wg_ref[...], NN)
    hu = _dg(xb, wu_ref[...], NN)
    h = (hg * _sigmoid(hg) * hu).astype(BF16)
    o_ref[...] += _dg(h, wd_ref[...], NN)

    @pl.when(j == pl.num_programs(1) - 1)
    def _():
        y = ALPHA * x_ref[...] + 0.5 * o_ref[...]
        o_ref[...] = _layer_norm(y, g_ref[...], b_ref[...])


def _ffn(x, wg, wu, wd, g, b):
    m, d = x.shape
    ff = wg.shape[1]
    tm = min(FFN_TM, m)
    return pl.pallas_call(
        _ffn_kernel,
        grid=(m // tm, ff // FFN_TF),
        in_specs=[
            pl.BlockSpec((tm, d), lambda i, j: (i, 0)),
            pl.BlockSpec((d, FFN_TF), lambda i, j: (0, j)),
            pl.BlockSpec((d, FFN_TF), lambda i, j: (0, j)),
            pl.BlockSpec((FFN_TF, d), lambda i, j: (j, 0)),
            pl.BlockSpec((1, d), lambda i, j: (0, 0)),
            pl.BlockSpec((1, d), lambda i, j: (0, 0)),
        ],
        out_specs=pl.BlockSpec((tm, d), lambda i, j: (i, 0)),
        out_shape=jax.ShapeDtypeStruct((m, d), F32),
        scratch_shapes=[pltpu.VMEM((tm, d), BF16)],
        compiler_params=_cparams(("parallel", "arbitrary")),
        name="ffn_swiglu_ln",
    )(x, wg, wu, wd, g.reshape(1, d), b.reshape(1, d))


def _mm_kernel(x_ref, w_ref, o_ref):
    o_ref[...] = _dg(x_ref[...], w_ref[...], NN).astype(o_ref.dtype)


def _mm(x, w, out_dtype, tn=MM_TN):
    m, k = x.shape
    n = w.shape[1]
    tm = min(MM_TM, m)
    tn = min(tn, n)
    return pl.pallas_call(
        _mm_kernel,
        grid=(m // tm, n // tn),
        in_specs=[
            pl.BlockSpec((tm, k), lambda i, j: (i, 0)),
            pl.BlockSpec((k, tn), lambda i, j: (0, j)),
        ],
        out_specs=pl.BlockSpec((tm, tn), lambda i, j: (i, j)),
        out_shape=jax.ShapeDtypeStruct((m, n), out_dtype),
        compiler_params=_cparams(("parallel", "arbitrary")),
        name="proj_mm",
    )(x, w)


def _token_shift(x, prev_row, mu, first):
    rows = lax.broadcasted_iota(jnp.int32, x.shape, 0)
    prev_row = jnp.where(first, jnp.zeros_like(prev_row), prev_row)
    shifted = jnp.where(rows == 0, prev_row, pltpu.roll(x, 1, 0))
    return x + (shifted - x) * mu


def _rw_vlora_kernel(seq_len, p_ref, pp_ref, mu_ref, v1_ref, o_ref):
    first = (pl.program_id(0) * p_ref.shape[0]) % seq_len == 0
    v = _token_shift(p_ref[...], pp_ref[7:8, :], mu_ref[...], first)
    o_ref[...] = _dot3(v, v1_ref[...])


def _rw_vlora(p_rkv, mu_v, v1, seq_len):
    m = p_rkv.shape[0]
    tm = PREP_TM
    wd = RW_WIDTH
    return pl.pallas_call(
        functools.partial(_rw_vlora_kernel, seq_len),
        grid=(m // tm,),
        in_specs=[pl.BlockSpec((tm, wd), lambda i: (i, 2)),
                  pl.BlockSpec((8, wd), lambda i: (jnp.maximum(i * (tm // 8) - 1, 0), 2)),
                  pl.BlockSpec((1, wd), lambda i: (0, 0)),
                  pl.BlockSpec((wd, RW_MV_LORA), lambda i: (0, 0))],
        out_specs=pl.BlockSpec((tm, RW_MV_LORA), lambda i: (i, 0)),
        out_shape=jax.ShapeDtypeStruct((m, RW_MV_LORA), F32),
        compiler_params=_cparams(("parallel",)),
        name="rwkv_vlora",
    )(p_rkv, p_rkv, mu_v.reshape(1, wd), v1)


def _sp(x):
    return _split2(x)


def _d3s(a, b, dims=NN):
    return _dg(a[0], b[0], dims) + (_dg(a[0], b[1], dims) + _dg(a[1], b[0], dims))


def _d2m(a, m01, dims=NN):
    return _dg(a[0], m01, dims) + _dg(a[1], m01, dims)


def _wkv_kernel(has_vres, *refs):
    (pr_ref, pk_ref, pv_ref, ppr_ref, ppk_ref, ppv_ref, lo_ref, plo_ref,
     mur_ref, muk_ref, muv_ref, mul_ref, w0_ref, a0_ref, w2_ref, a2_ref, g2_ref,
     kk_ref, ka_ref, rk_ref, gg_ref, gb_ref) = refs[:22]
    if has_vres:
        vf_ref, v0_ref, vlo_ref, v2_ref, o_ref, s_ref = refs[22:]
    else:
        o_ref, vout_ref, s_ref = refs[22:]
    L = WKV_L
    n_groups = WKV_GROUPS
    rg = pr_ref.shape[0] // n_groups
    n_chunks = rg // L

    @pl.when(pl.program_id(2) == 0)
    def _():
        s_ref[...] = jnp.zeros_like(s_ref)

    ri = lax.broadcasted_iota(jnp.int32, (2 * L, 2 * L), 0)
    ci = lax.broadcasted_iota(jnp.int32, (2 * L, 2 * L), 1)
    bd_f = ((ri // L) == (ci // L)).astype(F32)
    bd_b = bd_f.astype(BF16)
    rl = lax.broadcasted_iota(jnp.int32, (L, L), 0)
    cl = lax.broadcasted_iota(jnp.int32, (L, L), 1)
    tri_b = (cl <= rl).astype(BF16)
    rt_i = lax.broadcasted_iota(jnp.int32, (L, 2 * L), 0)
    cs_i = lax.broadcasted_iota(jnp.int32, (L, 2 * L), 1) % L
    strict = cs_i < rt_i
    incl = cs_i <= rt_i

    bd2_b = jnp.concatenate([bd_b, bd_b], axis=1)
    eye_f = (ri == ci).astype(F32)

    def sm(zb, mask=bd_b):
        return jnp.concatenate([zb, zb], axis=0) * mask

    def b16(x):
        return x.astype(BF16)

    def hsum(x):
        hi, lo = _sp(x)
        s2 = _dg(jnp.concatenate([hi, lo], axis=0), bd_b, NN)
        return s2[:L] + s2[L:]

    def cumsum_rows(x):
        x1, x2, x3 = _split3(x)
        s3 = _dg(tri_b, jnp.concatenate([x1, x2, x3], axis=1), NN)
        w = x.shape[1]
        return s3[:, :w] + (s3[:, w:2 * w] + s3[:, 2 * w:])

    inv_n = 1.0 / RW_HEAD
    n_lvl = L.bit_length() - 1
    n_pairs = pr_ref.shape[1] // (2 * L)

    C = range(n_chunks * n_pairs)
    rows = [slice((c // n_pairs) * L, (c // n_pairs + 1) * L) for c in C]
    lanes = [slice((c % n_pairs) * 2 * L, (c % n_pairs + 1) * 2 * L) for c in C]
    fronts = {}

    def front(g):
        rs = slice(g * rg, (g + 1) * rg)
        if g == 0:
            first = pl.program_id(2) == 0
            prev = lambda p_ref, pp_ref: pp_ref[7:8, :]
        else:
            first = False
            prev = lambda p_ref, pp_ref: p_ref[g * rg - 1:g * rg, :]
        r_all = _token_shift(pr_ref[rs, :], prev(pr_ref, ppr_ref), mur_ref[...], first)
        yield
        k_all = _token_shift(pk_ref[rs, :], prev(pk_ref, ppk_ref), muk_ref[...], first)
        yield
        v_all = _token_shift(pv_ref[rs, :], prev(pv_ref, ppv_ref), muv_ref[...], first)
        yield
        lo = _token_shift(lo_ref[rs, :], prev(lo_ref, plo_ref), mul_ref[...], first)
        wl = lo[:, :RW_DECAY_LORA]
        al = lo[:, RW_DECAY_LORA:RW_DECAY_LORA + RW_AAA_LORA]
        gl = lo[:, RW_DECAY_LORA + RW_AAA_LORA:]
        yield
        z = w0_ref[...] + _dot3(jnp.tanh(wl), w2_ref[...])
        yield
        w_all = -(jnp.maximum(-z, 0.0) + jnp.log1p(jnp.exp(-jnp.abs(z)))) - 0.5
        yield
        a_all = _sigmoid(a0_ref[...] + _dot3(al, a2_ref[...]))
        yield
        g_all = _dot3(_sigmoid(gl), g2_ref[...])
        yield
        if has_vres:
            mix = _sigmoid(v0_ref[...] + _dot3(vlo_ref[rs, :], v2_ref[...]))
            v_all = v_all + (vf_ref[rs, :] - v_all) * mix
        else:
            vout_ref[rs, :] = v_all
        yield
        k_k = [kk_ref[:, lanes[c]] for c in C]
        k_a = [ka_ref[:, lanes[c]] for c in C]
        r_k = [rk_ref[:, lanes[c]] for c in C]
        r = [r_all[rows[c], lanes[c]] for c in C]
        k_raw = [k_all[rows[c], lanes[c]] for c in C]
        v = [v_all[rows[c], lanes[c]] for c in C]
        a = [a_all[rows[c], lanes[c]] for c in C]
        kk = [k_raw[c] * k_k[c] for c in C]
        ssq = [hsum(kk[c] * kk[c]) for c in C]
        yield
        lw = [-jnp.exp(w_all[rows[c], lanes[c]]) for c in C]
        cum = [cumsum_rows(lw[c]) for c in C]
        yield
        kk = [kk[c] * lax.rsqrt(jnp.maximum(ssq[c], 1e-24)) for c in C]
        k = [k_raw[c] * (1.0 + (a[c] - 1.0) * k_a[c]) for c in C]
        bv = [kk[c] * a[c] for c in C]
        yield
        tot = [cum[c][L - 1:L, :] for c in C]
        e_neg = [jnp.exp(-cum[c]) for c in C]
        rt = [r[c] * jnp.exp(cum[c]) for c in C]
        yield
        at = [-kk[c] * jnp.exp(cum[c] - lw[c]) for c in C]
        e_tot = [jnp.exp(tot[c] - cum[c]) for c in C]
        yield
        bh = [b16(bv[c] * e_tot[c]) for c in C]
        kh = [b16(k[c] * e_tot[c]) for c in C]
        vb = [b16(v[c]) for c in C]
        smv = [sm(vb[c]) for c in C]
        yield
        lhs2 = [b16(jnp.concatenate([at[c], rt[c]], axis=0)) for c in C]
        smkb = [jnp.concatenate([sm(b16(k[c] * e_neg[c])), sm(b16(bv[c] * e_neg[c]))], axis=0) for c in C]
        yield
        bonus = [hsum(r[c] * k[c] * r_k[c]) * v[c] for c in C]
        g_out = [g_all[rows[c], lanes[c]] for c in C]
        fronts[g] = (tot, rt, at, bh, kh, vb, smv, lhs2, smkb, bonus, g_out)

    def drain(gen):
        for _ in gen:
            pass

    def run_group(g, state, pump):
        tot, rt, at, bh, kh, vb, smv, lhs2, smkb, bonus, g_out = fronts.pop(g)
        akb = [_dg(lhs2[c], smkb[c], NT) for c in C]
        pump()
        ak = [akb[c][:, :2 * L] for c in C]
        ab = [akb[c][:, 2 * L:] for c in C]
        a_ak = [b16(jnp.where(strict, ak[c][:L], 0.0)) for c in C]
        a_rk = [b16(jnp.where(incl, ak[c][L:], 0.0)) for c in C]
        p = [jnp.where(strict, ab[c][:L], 0.0) for c in C]
        a_rb = [b16(jnp.where(incl, ab[c][L:], 0.0)) for c in C]
        akv = [_dg(a_ak[c], smv[c], NN) for c in C]
        pump()
        z = [jnp.concatenate([at[c], akv[c]], axis=1) for c in C]
        for lvl in range(n_lvl):
            pb = [b16(p[c]) for c in C]
            smz = [sm(b16(z[c]), bd2_b) for c in C]
            z = [z[c] + _dg(pb[c], smz[c], NN) for c in C]
            pump()
            if lvl + 1 < n_lvl:
                smp = [sm(pb[c]) for c in C]
                p = [_dg(pb[c], smp[c], NN) for c in C]
                pump()
        zb = [b16(z[c]) for c in C]
        qy = [_dg(a_rb[c], sm(zb[c], bd2_b), NN) for c in C]
        pump()
        q = [b16(rt[c] + qy[c][:, :2 * L]) for c in C]
        y0 = [_dg(a_rk[c], smv[c], NN) + qy[c][:, 2 * L:] for c in C]
        pump()
        zbh = [_dg(zb[c], bh[c], TN) for c in C]
        gam_s = [_sp(bd_f * zbh[c][:2 * L] + eye_f * jnp.exp(tot[c])) for c in C]
        cc = [bd_f * (_dg(vb[c], kh[c], TN) + zbh[c][2 * L:]) for c in C]
        pump()

        y = []
        for c in C:
            pi = c % n_pairs
            st_s = _sp(state[pi])
            y.append(_dg(q[c], st_s[0], NT) + y0[c])
            state[pi] = _d3s(st_s, gam_s[c]) + cc[c]
        mean = [hsum(y[c]) * inv_n for c in C]
        d = [y[c] - mean[c] for c in C]
        var = [hsum(d[c] * d[c]) * inv_n for c in C]
        for c in C:
            yn = d[c] * lax.rsqrt(var[c] + RW_GN_EPS) * gg_ref[:, lanes[c]] + gb_ref[:, lanes[c]]
            o_ref[pl.ds(g * rg + rows[c].start, L), lanes[c]] = ((yn + bonus[c]) * g_out[c]).astype(o_ref.dtype)
        return state

    state = [s_ref[pi] for pi in range(n_pairs)]
    drain(front(0))
    for g in range(n_groups):
        nxt = front(g + 1) if g + 1 < n_groups else iter(())
        state = run_group(g, state, lambda: next(nxt, None))
        drain(nxt)
    for pi in range(n_pairs):
        s_ref[pi] = state[pi]


def _wkv(p_rkv, p_lora, mu, w0, a0, w2, a2, g2, k_k, k_a, r_k, gn_g, gn_b, vres, batch, seq_len):
    m = p_rkv.shape[0]
    wd = RW_WIDTH
    tc = min(WKV_TC, seq_len)
    nt = seq_len // tc
    pair = 2 * RW_HEAD
    lanes = WKV_PAIRS * pair
    nb = wd // lanes
    row = lambda b, h, t: b * nt + t
    prv = lambda b, h, t: jnp.maximum((b * nt + t) * (tc // 8) - 1, 0)
    act = lambda sec: pl.BlockSpec((tc, lanes), lambda b, h, t: (row(b, h, t), sec * nb + h))
    prev = lambda sec: pl.BlockSpec((8, lanes), lambda b, h, t: (prv(b, h, t), sec * nb + h))
    par = lambda sec=0: pl.BlockSpec((1, lanes), lambda b, h, t: (0, sec * nb + h))
    mat = lambda k: pl.BlockSpec((k, lanes), lambda b, h, t: (0, h))
    lo_w = RW_LORA_COLS
    c3 = 3 * wd
    in_specs = [act(0), act(1), act(2), prev(0), prev(1), prev(2),
                pl.BlockSpec((tc, lo_w), lambda b, h, t: (row(b, h, t), 0)),
                pl.BlockSpec((8, lo_w), lambda b, h, t: (prv(b, h, t), 0)),
                par(0), par(1), par(2), pl.BlockSpec((1, lo_w), lambda b, h, t: (0, 0)),
                par(), par(), mat(RW_DECAY_LORA), mat(RW_AAA_LORA), mat(RW_GATE_LORA)] + [par()] * 5
    mu_rkv = mu[:c3].reshape(1, c3)
    args = [p_rkv, p_rkv, p_rkv, p_rkv, p_rkv, p_rkv, p_lora, p_lora,
            mu_rkv, mu_rkv, mu_rkv, mu[c3:].reshape(1, lo_w),
            w0.reshape(1, wd), a0.reshape(1, wd), w2, a2, g2]
    args += [z.reshape(1, wd) for z in (k_k, k_a, r_k, gn_g, gn_b)]
    out_act = pl.BlockSpec((tc, lanes), lambda b, h, t: (row(b, h, t), h))
    if vres is not None:
        v_first, v0, vlo, v2 = vres
        in_specs += [out_act, par(), pl.BlockSpec((tc, RW_MV_LORA), lambda b, h, t: (row(b, h, t), 0)),
                     mat(RW_MV_LORA)]
        args += [v_first, v0.reshape(1, wd), vlo, v2]
        out_specs = out_act
        out_shape = jax.ShapeDtypeStruct((m, wd), BF16)
    else:
        out_specs = [out_act, out_act]
        out_shape = [jax.ShapeDtypeStruct((m, wd), BF16), jax.ShapeDtypeStruct((m, wd), F32)]
    return pl.pallas_call(
        functools.partial(_wkv_kernel, vres is not None),
        grid=(batch, nb, nt),
        in_specs=in_specs,
        out_specs=out_specs,
        out_shape=out_shape,
        scratch_shapes=[pltpu.VMEM((WKV_PAIRS, pair, pair), F32)],
        compiler_params=_cparams(("parallel", "parallel", "arbitrary")),
        name="rwkv_wkv",
    )(*args)


def _gmlp_kernel(p_ref, lg_ref, lb_ref, ws_ref, bst_ref, o_ref):
    tm = p_ref.shape[0]
    u = _gelu_tanh(p_ref[:, :GM_WIDTH])
    v = _layer_norm(_gelu_tanh(p_ref[:, GM_WIDTH:]), lg_ref[...], lb_ref[...])
    rl = lax.broadcasted_iota(jnp.int32, (GM_CHUNK, GM_CHUNK), 0)
    cl = lax.broadcasted_iota(jnp.int32, (GM_CHUNK, GM_CHUNK), 1)
    causal = cl <= rl
    for g in range(GM_GROUPS):
        cols = slice(g * GM_GROUP, (g + 1) * GM_GROUP)
        wsg = jnp.where(causal, ws_ref[g], 0.0)
        bias = bst_ref[:, g:g + 1]
        for c in range(tm // GM_CHUNK):
            rows = slice(c * GM_CHUNK, (c + 1) * GM_CHUNK)
            s = _dot3(wsg, v[rows, cols]) + bias
            o_ref[rows, cols] = (u[rows, cols] * s).astype(o_ref.dtype)


def _gmlp(p_gm, ln_g, ln_b, ws, bs):
    m = p_gm.shape[0]
    tm = GM_TM
    fix2 = lambda i: (0, 0)
    return pl.pallas_call(
        _gmlp_kernel,
        grid=(m // tm,),
        in_specs=[
            pl.BlockSpec((tm, GM_COLS), lambda i: (i, 0)),
            pl.BlockSpec((1, GM_WIDTH), fix2),
            pl.BlockSpec((1, GM_WIDTH), fix2),
            pl.BlockSpec((GM_GROUPS, GM_CHUNK, GM_CHUNK), lambda i: (0, 0, 0)),
            pl.BlockSpec((GM_CHUNK, GM_GROUPS), fix2),
        ],
        out_specs=pl.BlockSpec((tm, GM_WIDTH), lambda i: (i, 0)),
        out_shape=jax.ShapeDtypeStruct((m, GM_WIDTH), BF16),
        compiler_params=_cparams(("parallel",)),
        name="gmlp_mix",
    )(p_gm, ln_g.reshape(1, -1), ln_b.reshape(1, -1), ws, bs.T)


def _compress_kernel(c_ref, pos_ref, w1_ref, w2_ref, o_ref):
    c = c_ref[...]
    n = c.shape[0]
    r1 = _dot3(c, w1_ref[0])
    r2 = _dot3(c, w1_ref[1])
    pos = jnp.broadcast_to(pos_ref[...], (8, pos_ref.shape[1]))
    half = w1_ref.shape[1]
    pterm = (_dot3(pos[:, :half], w1_ref[0]) + _dot3(pos[:, half:], w1_ref[1]))[0:1, :]
    h = r1 + pltpu.roll(r2, n - 1, 0) + pterm
    o_ref[...] = _dot3(_gelu_tanh(h), w2_ref[...])


def _compress(z, pos, w1, w2):
    gb, t, d = z.shape
    nb = t // CMP_STRIDE
    c = z.reshape(gb, nb, CMP_STRIDE * d)
    half = CMP_STRIDE * d
    return pl.pallas_call(
        _compress_kernel,
        grid=(gb,),
        in_specs=[
            pl.BlockSpec((None, nb, half), lambda i: (i, 0, 0)),
            pl.BlockSpec((1, 2 * half), lambda i: (0, 0)),
            pl.BlockSpec((2, half, d), lambda i: (0, 0, 0)),
            pl.BlockSpec((d, d), lambda i: (0, 0)),
        ],
        out_specs=pl.BlockSpec((None, nb, d), lambda i: (i, 0, 0)),
        out_shape=jax.ShapeDtypeStruct((gb, nb, d), F32),
        compiler_params=_cparams(("parallel",)),
        name="nsa_compress",
    )(c, pos.reshape(1, 2 * half), w1.reshape(2, half, d), w2)


def _nsa_proj_kernel(x_ref, wq_ref, wkv_ref, wgt_ref,
                     q_ref, kc_ref, vc_ref, ks_ref, vs_ref, kw_ref, vw_ref, gt_ref):
    x = x_ref[...]
    scale = NSA_DK ** -0.5
    rq = _dg(x, wq_ref[...], NN)
    for h in range(NSA_HPG):
        q_ref[h] = (rq[:, h * NSA_DK:(h + 1) * NSA_DK] * scale).astype(q_ref.dtype)
    rkv = _dg(x, wkv_ref[...], NN)
    o = 0
    for o_ref, d in ((vc_ref, NSA_DV), (vs_ref, NSA_DV), (vw_ref, NSA_DV),
                     (kc_ref, NSA_DK), (ks_ref, NSA_DK), (kw_ref, NSA_DK)):
        o_ref[...] = rkv[:, o:o + d].astype(o_ref.dtype)
        o += d
    gt_ref[...] = _dg(wgt_ref[...], x, NT)


def _nsa_proj(xb, wl):
    m, kdim = xb.shape
    G, hp = NSA_KV_GROUPS, NSA_HPG
    tm = min(MM_TM, m)
    o = OFF_NSA

    def take(width, d):
        nonlocal o
        w3 = wl[:, o:o + width].astype(BF16).reshape(kdim, width // d, d)
        o += width
        return w3

    wq = take(NSA_Q_COLS, hp * NSA_DK).reshape(kdim, NSA_Q_COLS)
    kc, vc, ks, vs, kw, vw = (take(NSA_GK, NSA_DK), take(NSA_GV, NSA_DV), take(NSA_GK, NSA_DK),
                              take(NSA_GV, NSA_DV), take(NSA_GK, NSA_DK), take(NSA_GV, NSA_DV))
    wkv = jnp.concatenate([vc, vs, vw, kc, ks, kw], axis=2).transpose(1, 0, 2)
    nkv = wkv.shape[2]
    wgt = take(NSA_G_COLS, 3 * hp).transpose(1, 2, 0)
    wgt = jnp.pad(wgt, ((0, 0), (0, NSA_GROWS - 3 * hp), (0, 0)))
    ospec = lambda d: pl.BlockSpec((None, tm, d), lambda i, g: (g, i, 0))
    oshape = lambda d, dt: jax.ShapeDtypeStruct((G, m, d), dt)
    return pl.pallas_call(
        _nsa_proj_kernel,
        grid=(m // tm, G),
        in_specs=[pl.BlockSpec((tm, kdim), lambda i, g: (i, 0)),
                  pl.BlockSpec((kdim, hp * NSA_DK), lambda i, g: (0, g)),
                  pl.BlockSpec((None, kdim, nkv), lambda i, g: (g, 0, 0)),
                  pl.BlockSpec((None, NSA_GROWS, kdim), lambda i, g: (g, 0, 0))],
        out_specs=[pl.BlockSpec((hp, tm, NSA_DK), lambda i, g: (g, i, 0)),
                   ospec(NSA_DK), ospec(NSA_DV), ospec(NSA_DK), ospec(NSA_DV), ospec(NSA_DK), ospec(NSA_DV),
                   pl.BlockSpec((None, NSA_GROWS, tm), lambda i, g: (g, 0, i))],
        out_shape=[jax.ShapeDtypeStruct((G * hp, m, NSA_DK), BF16),
                   oshape(NSA_DK, F32), oshape(NSA_DV, F32), oshape(NSA_DK, BF16), oshape(NSA_DV, BF16),
                   oshape(NSA_DK, BF16), oshape(NSA_DV, BF16),
                   jax.ShapeDtypeStruct((G, NSA_GROWS, m), F32)],
        compiler_params=_cparams(("parallel", "arbitrary")),
        name="nsa_proj",
    )(xb, wq, wkv, wgt)


def _nsa_kernel(seq_len, q_ref, gt_ref, kc_ref, vc_ref, ks_ref, vs_ref, kw_ref, vw_ref,
                o_ref, bias_ref):
    tq = NSA_TQ
    hp = NSA_HPG
    nr = hp * tq
    tk = min(NSA_TK, seq_len)
    wk = WIN + tq
    n_s = seq_len // SEL_BLK
    k_sel = min(N_SEL, n_s)
    n_cb = kc_ref.shape[0]
    i = pl.program_id(2)
    t0 = i * tq

    qa = q_ref[...].reshape(nr, NSA_DK)
    t_lane = t0 + lax.broadcasted_iota(jnp.int32, (1, tq), 1)
    t_all = t0 + (lax.broadcasted_iota(jnp.int32, (1, nr), 1) & (tq - 1))
    tile_heads = lambda z: jnp.concatenate([z] * hp, axis=1)

    w0 = pl.multiple_of(jnp.maximum(t0 - WIN, 0), tq)
    kw_tile = kw_ref[pl.ds(w0, wk), :]
    vw_tile = vw_ref[pl.ds(w0, wk), :]
    s_w = _dg(kw_tile, qa, NT)
    s_c = _dg(kc_ref[...].astype(BF16), qa, NT)

    n_end = lax.broadcasted_iota(jnp.int32, (n_cb, 1), 0) * CMP_STRIDE + (CMP_BLK - 1)
    m_c = (n_end <= t_all) & (n_end < seq_len)
    s_c = jnp.where(m_c, s_c, NEG)
    e_c = jnp.where(m_c, jnp.exp(s_c - jnp.max(s_c, axis=0, keepdims=True)), 0.0)
    p_c = e_c * (1.0 / jnp.maximum(jnp.sum(e_c, axis=0, keepdims=True), 1e-30))
    o_c = _dg(vc_ref[...].astype(BF16), p_c.astype(BF16), TN)

    p_sum = p_c[:, 0:tq]
    for h in range(1, hp):
        p_sum = p_sum + p_c[:, h * tq:(h + 1) * tq]
    ss = lax.broadcasted_iota(jnp.int32, (n_s, n_cb), 0) * SEL_BLK
    cs = lax.broadcasted_iota(jnp.int32, (n_s, n_cb), 1) * CMP_STRIDE
    overlap_t = ((cs < ss + SEL_BLK) & (cs + (CMP_BLK - 1) >= ss)
                 & (cs + (CMP_BLK - 1) < seq_len)).astype(BF16)
    imp = _dot_exact_lhs(overlap_t, p_sum, NN)

    kpos_w = w0 + lax.broadcasted_iota(jnp.int32, (wk, 1), 0)
    bias_w = jnp.where((kpos_w <= t_lane) & (kpos_w > t_lane - WIN), 0.0, NEG)
    s_w = s_w + tile_heads(bias_w)
    p_w = jnp.exp(s_w - jnp.max(s_w, axis=0, keepdims=True))
    l_w = jnp.sum(p_w, axis=0, keepdims=True)
    o_w = _dg(vw_tile, p_w.astype(BF16), TN) * (1.0 / jnp.maximum(l_w, 1e-30))

    blk = lax.broadcasted_iota(jnp.int32, (n_s, 1), 0)
    cur = t_lane // SEL_BLK
    valid = blk * SEL_BLK <= t_lane
    forced = valid & ((blk == 0) | (blk == cur) | (blk == cur - 1))
    score = jnp.where(forced, FORCED, jnp.where(valid, imp, NEG))
    rank = jnp.zeros((n_s, tq), jnp.int32)
    for s in range(n_s):
        row = score[s:s + 1, :]
        beats = (row > score) | ((row == score) & (blk > s))
        rank = rank + beats.astype(jnp.int32)
    sel_t = ((rank < k_sel) & (score > 0.5 * NEG)).astype(BF16)
    n_kt = (t0 + tq + tk - 1) // tk
    n_all = seq_len // tk
    half = max(n_all // 2, 1)

    def expand(kt0, kt1):
        width = (kt1 - kt0) * tk
        kpos = kt0 * tk + lax.broadcasted_iota(jnp.int32, (width, 1), 0)
        er = (lax.broadcasted_iota(jnp.int32, (width, n_s), 0) + kt0 * tk) // SEL_BLK
        ec = lax.broadcasted_iota(jnp.int32, (width, n_s), 1)
        chosen = _dg((er == ec).astype(BF16), sel_t, NN)
        bias = jnp.where((chosen > 0.5) & (kpos <= t_lane), 0.0, NEG)
        for kt in range(kt0, kt1):
            bias_ref[kt] = bias[(kt - kt0) * tk:(kt - kt0 + 1) * tk]

    expand(0, half)
    if n_all > half:
        @pl.when(n_kt > half)
        def _():
            expand(half, n_all)

    def sel_body(kt, carry):
        m_run, l_run, acc = carry
        k0 = pl.multiple_of(kt * tk, tk)
        s = _dg(ks_ref[pl.ds(k0, tk), :], qa, NT) + tile_heads(bias_ref[kt])
        m_new = jnp.maximum(m_run, jnp.max(s, axis=0, keepdims=True))
        alpha = jnp.exp(m_run - m_new)
        p = jnp.exp(s - m_new)
        l_new = alpha * l_run + jnp.sum(p, axis=0, keepdims=True)
        acc = alpha * acc + _dg(vs_ref[pl.ds(k0, tk), :], p.astype(BF16), TN)
        return m_new, l_new, acc

    init = (jnp.full((1, nr), NEG, F32), jnp.zeros((1, nr), F32), jnp.zeros((NSA_DV, nr), F32))
    _, l_s, acc_s = lax.fori_loop(0, n_kt, sel_body, init)
    o_s = acc_s * (1.0 / jnp.maximum(l_s, 1e-30))

    gates = _sigmoid(gt_ref[...])
    for h in range(hp):
        lanes = slice(h * tq, (h + 1) * tq)
        out_t = (gates[3 * h:3 * h + 1] * o_c[:, lanes] + gates[3 * h + 1:3 * h + 2] * o_s[:, lanes]
                 + gates[3 * h + 2:3 * h + 3] * o_w[:, lanes])
        o_ref[:, h * NSA_DV:(h + 1) * NSA_DV] = out_t.T.astype(o_ref.dtype)


def _nsa_attention(q, p_gt, k_cmp, v_cmp, ks, vs, kw, vw, batch, seq_len):
    m = q.shape[1]
    tq = NSA_TQ
    assert seq_len >= WIN + tq and seq_len % NSA_TK == 0
    nq = seq_len // tq
    n_cb = k_cmp.shape[1]
    tk = min(NSA_TK, seq_len)
    kv = lambda b, g, i: (g * batch + b, 0, 0)
    return pl.pallas_call(
        functools.partial(_nsa_kernel, seq_len),
        grid=(batch, NSA_KV_GROUPS, nq),
        in_specs=[
            pl.BlockSpec((NSA_HPG, tq, NSA_DK), lambda b, g, i: (g, b * nq + i, 0)),
            pl.BlockSpec((None, NSA_GROWS, tq), lambda b, g, i: (g, 0, b * nq + i)),
            pl.BlockSpec((None, n_cb, NSA_DK), kv),
            pl.BlockSpec((None, n_cb, NSA_DV), kv),
            pl.BlockSpec((None, seq_len, NSA_DK), kv),
            pl.BlockSpec((None, seq_len, NSA_DV), kv),
            pl.BlockSpec((None, seq_len, NSA_DK), kv),
            pl.BlockSpec((None, seq_len, NSA_DV), kv),
        ],
        out_specs=pl.BlockSpec((tq, NSA_HPG * NSA_DV), lambda b, g, i: (b * nq + i, g)),
        out_shape=jax.ShapeDtypeStruct((m, NSA_WIDTH), BF16),
        scratch_shapes=[pltpu.VMEM((seq_len // tk, tk, tq), F32)],
        compiler_params=_cparams(("parallel", "parallel", "arbitrary")),
        name="nsa_attention",
    )(q, p_gt, k_cmp, v_cmp, ks, vs, kw, vw)


def _merge_kernel(y0_ref, y1_ref, y2_ref, g0_ref, g1_ref, g2_ref, w_ref, o_ref):
    acc = _sigmoid(g0_ref[...]) * _dg(y0_ref[...], w_ref[0], NN)
    acc = acc + _sigmoid(g1_ref[...]) * _dg(y1_ref[...], w_ref[1], NN)
    acc = acc + _sigmoid(g2_ref[...]) * _dg(y2_ref[...], w_ref[2], NN)
    o_ref[...] = acc.astype(o_ref.dtype)


def _merge(y_rw, y_gm, y_ns, p_gate, w_br):
    m, d = y_rw.shape
    tm = min(MERGE_TM, m)
    tn = MERGE_TN
    nj = d // tn
    ys = pl.BlockSpec((tm, d), lambda i, j: (i, 0))
    gspec = lambda br: pl.BlockSpec((tm, tn), lambda i, j: (i, br * nj + j))
    return pl.pallas_call(
        _merge_kernel,
        grid=(m // tm, nj),
        in_specs=[ys, ys, ys, gspec(0), gspec(1), gspec(2),
                  pl.BlockSpec((N_BRANCH, d, tn), lambda i, j: (0, 0, j))],
        out_specs=pl.BlockSpec((tm, tn), lambda i, j: (i, j)),
        out_shape=jax.ShapeDtypeStruct((m, d), BF16),
        compiler_params=_cparams(("parallel", "arbitrary")),
        name="branch_merge",
    )(y_rw, y_gm, y_ns, p_gate, p_gate, p_gate, w_br)


def _out_ln_kernel(y_ref, w_ref, x_ref, g_ref, b_ref, o_ref):
    y = ALPHA * x_ref[...] + _dg(y_ref[...], w_ref[...], NN)
    o_ref[...] = _layer_norm(y, g_ref[...], b_ref[...])


def _out_ln(merged, w_o, x, g, b):
    m, d = x.shape
    tm = min(WO_TM, m)
    row = lambda i: (i, 0)
    fix = lambda i: (0, 0)
    return pl.pallas_call(
        _out_ln_kernel,
        grid=(m // tm,),
        in_specs=[pl.BlockSpec((tm, d), row), pl.BlockSpec((d, d), fix), pl.BlockSpec((tm, d), row),
                  pl.BlockSpec((1, d), fix), pl.BlockSpec((1, d), fix)],
        out_specs=pl.BlockSpec((tm, d), row),
        out_shape=jax.ShapeDtypeStruct((m, d), F32),
        compiler_params=_cparams(("parallel",)),
        name="out_proj_ln",
    )(merged, w_o, x, g.reshape(1, d), b.reshape(1, d))


def _rwkv_branch(xb, wl, mu, w0, w2, a0, a2, g2, k_k, k_a, r_k, gn_g, gn_b, vres, batch, seq_len):
    p_rkv = _mm(xb, wl[:, :3 * RW_WIDTH].astype(BF16), F32)
    p_lora = _mm(xb, wl[:, 3 * RW_WIDTH:RW_COLS].astype(BF16), F32, tn=RW_LORA_COLS)
    if vres is None:
        return _wkv(p_rkv, p_lora, mu, w0, a0, w2, a2, g2, k_k, k_a, r_k.reshape(-1), gn_g, gn_b,
                    None, batch, seq_len)
    v_first, v0, v1, v2 = vres
    vlo = _rw_vlora(p_rkv, mu[2 * RW_WIDTH:3 * RW_WIDTH], v1, seq_len)
    y = _wkv(p_rkv, p_lora, mu, w0, a0, w2, a2, g2, k_k, k_a, r_k.reshape(-1), gn_g, gn_b,
             (v_first, v0, vlo, v2), batch, seq_len)
    return y, v_first


def _nsa_branch(xb, wl, pos_k, pos_v, phi_k1, phi_k2, phi_v1, phi_v2, batch, seq_len):
    q, kc, vc, ks, vs, kw, vw, p_g = _nsa_proj(xb, wl)
    gb = NSA_KV_GROUPS * batch
    per_seq = lambda z: z.reshape(gb, seq_len, z.shape[-1])
    k_cmp = _compress(per_seq(kc), pos_k, phi_k1, phi_k2)
    v_cmp = _compress(per_seq(vc), pos_v, phi_v1, phi_v2)
    return _nsa_attention(q, p_g, k_cmp, v_cmp, per_seq(ks), per_seq(vs), per_seq(kw), per_seq(vw),
                          batch, seq_len)


def kernel(x, w_in, rw_mu, rw_w0, rw_w2, rw_a0, rw_a2, rw_g2, rw_v0, rw_v1, rw_v2, rw_k_k, rw_k_a, rw_r_k, rw_gn_g, rw_gn_b, gm_ln_g, gm_ln_b, gm_ws, gm_bs, nsa_pos_k, nsa_pos_v, nsa_phi_k1, nsa_phi_k2, nsa_phi_v1, nsa_phi_v2, w_br, w_o, ffn1_wg, ffn1_wu, ffn1_wd, ffn2_wg, ffn2_wu, ffn2_wd, ln_g, ln_b):
    batch, seq_len, d = x.shape
    m = batch * seq_len
    h = x.reshape(m, d)
    v_first = None
    for l in range(DEPTH):
        h = _ffn(h, ffn1_wg[l].astype(BF16), ffn1_wu[l].astype(BF16), ffn1_wd[l].astype(BF16),
                 ln_g[l, 0], ln_b[l, 0])
        hb = h.astype(BF16)
        wl = w_in[l]
        vres = None if l == 0 else (v_first, rw_v0[l - 1], rw_v1[l - 1], rw_v2[l - 1])
        y_rw, v_out = _rwkv_branch(hb, wl, rw_mu[l], rw_w0[l], rw_w2[l], rw_a0[l], rw_a2[l], rw_g2[l],
                                   rw_k_k[l], rw_k_a[l], rw_r_k[l], rw_gn_g[l], rw_gn_b[l], vres,
                                   batch, seq_len)
        if l == 0:
            v_first = v_out
        p_gm = _mm(hb, wl[:, OFF_GM:OFF_NSA].astype(BF16), F32)
        y_gm = _gmlp(p_gm, gm_ln_g[l], gm_ln_b[l], gm_ws[l], gm_bs[l])
        y_ns = _nsa_branch(hb, wl, nsa_pos_k[l], nsa_pos_v[l], nsa_phi_k1[l], nsa_phi_k2[l],
                           nsa_phi_v1[l], nsa_phi_v2[l], batch, seq_len)
        p_gate = _mm(hb, wl[:, OFF_GATE:].astype(BF16), F32)
        merged = _merge(y_rw, y_gm, y_ns, p_gate, w_br[l].astype(BF16))
        h = _out_ln(merged, w_o[l].astype(BF16), h, ln_g[l, 1], ln_b[l, 1])
        h = _ffn(h, ffn2_wg[l].astype(BF16), ffn2_wu[l].astype(BF16), ffn2_wd[l].astype(BF16),
                 ln_g[l, 2], ln_b[l, 2])
    return h.reshape(batch, seq_len, d)
```

```python
import functools

import jax
import jax.numpy as jnp
from jax import lax
from jax.experimental import pallas as pl
from jax.experimental.pallas import tpu as pltpu

F32 = jnp.float32
BF16 = jnp.bfloat16

D_MODEL = 2048
DEPTH = 2
RW_HEAD = 64
RW_WIDTH = D_MODEL
RW_DECAY_LORA = 96
RW_AAA_LORA = 96
RW_MV_LORA = 64
RW_GATE_LORA = 256
RW_LORA_COLS = RW_DECAY_LORA + RW_AAA_LORA + RW_GATE_LORA
RW_GN_EPS = RW_HEAD * 1e-5
GM_WIDTH = D_MODEL
GM_CHUNK = 128
GM_GROUP = 128
GM_GROUPS = GM_WIDTH // GM_GROUP
NSA_HEADS = 16
NSA_KV_GROUPS = 4
NSA_HPG = NSA_HEADS // NSA_KV_GROUPS
NSA_DK = 192
NSA_DV = 128
NSA_WIDTH = NSA_HEADS * NSA_DV
CMP_BLK = 32
CMP_STRIDE = 16
SEL_BLK = 64
N_SEL = 16
WIN = 512
D_FF = 5632
N_BRANCH = 3
ALPHA = (2 * DEPTH) ** 0.25
LN_EPS = 1e-5
NEG = -1e30
FORCED = 1e6

RW_COLS = 3 * RW_WIDTH + RW_LORA_COLS
GM_COLS = 2 * GM_WIDTH
NSA_Q_COLS = NSA_HEADS * NSA_DK
NSA_GK = NSA_KV_GROUPS * NSA_DK
NSA_GV = NSA_KV_GROUPS * NSA_DV
NSA_KV_COLS = 3 * (NSA_GK + NSA_GV)
NSA_G_COLS = 3 * NSA_HEADS
NSA_COLS = NSA_Q_COLS + NSA_KV_COLS + NSA_G_COLS
OFF_GM = RW_COLS
OFF_NSA = OFF_GM + GM_COLS
OFF_GATE = OFF_NSA + NSA_COLS

V7X_VMEM_BYTES = 64 * 1024 * 1024
VMEM_LIMIT = 56 * 1024 * 1024

FFN_TM = 512
FFN_TF = 512
MM_TM = 1024
MM_TN = 512
MERGE_TM = 512
MERGE_TN = 512
WO_TM = 512
PREP_TM = 128
WKV_L = 64
WKV_TC = 1024
WKV_PAIRS = 2
WKV_GROUPS = 2
GM_TM = 256
NSA_TQ = 256
NSA_TK = 1024
NSA_GROWS = 16

NN = (((1,), (0,)), ((), ()))
NT = (((1,), (1,)), ((), ()))
TN = (((0,), (0,)), ((), ()))


def _cparams(sem):
    return pltpu.CompilerParams(dimension_semantics=sem, vmem_limit_bytes=VMEM_LIMIT)


def _dg(a, b, dims):
    return lax.dot_general(a, b, dims, preferred_element_type=F32)


def _split2(x):
    hi = x.astype(BF16)
    lo = (x - hi.astype(F32)).astype(BF16)
    return hi, lo


def _split3(x):
    x1 = x.astype(BF16)
    r1 = x - x1.astype(F32)
    x2 = r1.astype(BF16)
    x3 = (r1 - x2.astype(F32)).astype(BF16)
    return x1, x2, x3


def _dot3(a, b, dims=NN):
    ah, al = _split2(a)
    bh, bl = _split2(b)
    return _dg(ah, bh, dims) + (_dg(ah, bl, dims) + _dg(al, bh, dims))


def _dot_exact_rhs(a, m01, dims=NN):
    a1, a2, a3 = _split3(a)
    return _dg(a1, m01, dims) + (_dg(a2, m01, dims) + _dg(a3, m01, dims))


def _dot_exact_lhs(m01, b, dims=NN):
    b1, b2, b3 = _split3(b)
    return _dg(m01, b1, dims) + (_dg(m01, b2, dims) + _dg(m01, b3, dims))


def _layer_norm(y, g, b):
    mu = jnp.mean(y, axis=-1, keepdims=True)
    d = y - mu
    var = jnp.mean(d * d, axis=-1, keepdims=True)
    return d * lax.rsqrt(var + LN_EPS) * g + b


def _gelu_tanh(x):
    return 0.5 * x * (1.0 + jnp.tanh(0.7978845608028654 * (x + 0.044715 * (x * x * x))))


def _sigmoid(x):
    return 1.0 / (1.0 + jnp.exp(-x))


def _ffn_kernel(x_ref, wg_ref, wu_ref, wd_ref, g_ref, b_ref, o_ref, xb_ref):
    j = pl.program_id(1)

    @pl.when(j == 0)
    def _():
        o_ref[...] = jnp.zeros_like(o_ref)
        xb_ref[...] = x_ref[...].astype(BF16)

    xb = xb_ref[...]
    hg = _dg(xb, wg_ref[...], NN)
    hu = _dg(xb, wu_ref[...], NN)
    h = (hg * _sigmoid(hg) * hu).astype(BF16)
    o_ref[...] += _dg(h, wd_ref[...], NN)

    @pl.when(j == pl.num_programs(1) - 1)
    def _():
        y = ALPHA * x_ref[...] + 0.5 * o_ref[...]
        o_ref[...] = _layer_norm(y, g_ref[...], b_ref[...])


def _ffn(x, wg, wu, wd, g, b):
    m, d = x.shape
    ff = wg.shape[1]
    tm = min(FFN_TM, m)
    return pl.pallas_call(
        _ffn_kernel,
        grid=(m // tm, ff // FFN_TF),
        in_specs=[
            pl.BlockSpec((tm, d), lambda i, j: (i, 0)),
            pl.BlockSpec((d, FFN_TF), lambda i, j: (0, j)),
            pl.BlockSpec((d, FFN_TF), lambda i, j: (0, j)),
            pl.BlockSpec((FFN_TF, d), lambda i, j: (j, 0)),
            pl.BlockSpec((1, d), lambda i, j: (0, 0)),
            pl.BlockSpec((1, d), lambda i, j: (0, 0)),
        ],
        out_specs=pl.BlockSpec((tm, d), lambda i, j: (i, 0)),
        out_shape=jax.ShapeDtypeStruct((m, d), F32),
        scratch_shapes=[pltpu.VMEM((tm, d), BF16)],
        compiler_params=_cparams(("parallel", "arbitrary")),
        name="ffn_swiglu_ln",
    )(x, wg, wu, wd, g.reshape(1, d), b.reshape(1, d))


def _mm_kernel(x_ref, w_ref, o_ref):
    o_ref[...] = _dg(x_ref[...], w_ref[...], NN).astype(o_ref.dtype)


def _mm(x, w, out_dtype, tn=MM_TN):
    m, k = x.shape
    n = w.shape[1]
    tm = min(MM_TM, m)
    tn = min(tn, n)
    return pl.pallas_call(
        _mm_kernel,
        grid=(m // tm, n // tn),
        in_specs=[
            pl.BlockSpec((tm, k), lambda i, j: (i, 0)),
            pl.BlockSpec((k, tn), lambda i, j: (0, j)),
        ],
        out_specs=pl.BlockSpec((tm, tn), lambda i, j: (i, j)),
        out_shape=jax.ShapeDtypeStruct((m, n), out_dtype),
        compiler_params=_cparams(("parallel", "arbitrary")),
        name="proj_mm",
    )(x, w)


def _token_shift(x, prev_row, mu, first):
    rows = lax.broadcasted_iota(jnp.int32, x.shape, 0)
    prev_row = jnp.where(first, jnp.zeros_like(prev_row), prev_row)
    shifted = jnp.where(rows == 0, prev_row, pltpu.roll(x, 1, 0))
    return x + (shifted - x) * mu


def _rw_vlora_kernel(seq_len, p_ref, pp_ref, mu_ref, v1_ref, o_ref):
    first = (pl.program_id(0) * p_ref.shape[0]) % seq_len == 0
    v = _token_shift(p_ref[...], pp_ref[7:8, :], mu_ref[...], first)
    o_ref[...] = _dot3(v, v1_ref[...])


def _rw_vlora(p_rkv, mu_v, v1, seq_len):
    m = p_rkv.shape[0]
    tm = PREP_TM
    wd = RW_WIDTH
    return pl.pallas_call(
        functools.partial(_rw_vlora_kernel, seq_len),
        grid=(m // tm,),
        in_specs=[pl.BlockSpec((tm, wd), lambda i: (i, 2)),
                  pl.BlockSpec((8, wd), lambda i: (jnp.maximum(i * (tm // 8) - 1, 0), 2)),
                  pl.BlockSpec((1, wd), lambda i: (0, 0)),
                  pl.BlockSpec((wd, RW_MV_LORA), lambda i: (0, 0))],
        out_specs=pl.BlockSpec((tm, RW_MV_LORA), lambda i: (i, 0)),
        out_shape=jax.ShapeDtypeStruct((m, RW_MV_LORA), F32),
        compiler_params=_cparams(("parallel",)),
        name="rwkv_vlora",
    )(p_rkv, p_rkv, mu_v.reshape(1, wd), v1)


def _sp(x):
    return _split2(x)


def _d3s(a, b, dims=NN):
    return _dg(a[0], b[0], dims) + (_dg(a[0], b[1], dims) + _dg(a[1], b[0], dims))


def _d2m(a, m01, dims=NN):
    return _dg(a[0], m01, dims) + _dg(a[1], m01, dims)


def _wkv_kernel(has_vres, *refs):
    (pr_ref, pk_ref, pv_ref, ppr_ref, ppk_ref, ppv_ref, lo_ref, plo_ref,
     mur_ref, muk_ref, muv_ref, mul_ref, w0_ref, a0_ref, w2_ref, a2_ref, g2_ref,
     kk_ref, ka_ref, rk_ref, gg_ref, gb_ref) = refs[:22]
    if has_vres:
        vf_ref, v0_ref, vlo_ref, v2_ref, o_ref, s_ref = refs[22:]
    else:
        o_ref, vout_ref, s_ref = refs[22:]
    L = WKV_L
    n_groups = WKV_GROUPS
    rg = pr_ref.shape[0] // n_groups
    n_chunks = rg // L

    @pl.when(pl.program_id(2) == 0)
    def _():
        s_ref[...] = jnp.zeros_like(s_ref)

    ri = lax.broadcasted_iota(jnp.int32, (2 * L, 2 * L), 0)
    ci = lax.broadcasted_iota(jnp.int32, (2 * L, 2 * L), 1)
    bd_f = ((ri // L) == (ci // L)).astype(F32)
    bd_b = bd_f.astype(BF16)
    rl = lax.broadcasted_iota(jnp.int32, (L, L), 0)
    cl = lax.broadcasted_iota(jnp.int32, (L, L), 1)
    tri_b = (cl <= rl).astype(BF16)
    rt_i = lax.broadcasted_iota(jnp.int32, (L, 2 * L), 0)
    cs_i = lax.broadcasted_iota(jnp.int32, (L, 2 * L), 1) % L
    strict = cs_i < rt_i
    incl = cs_i <= rt_i

    bd2_b = jnp.concatenate([bd_b, bd_b], axis=1)
    eye_f = (ri == ci).astype(F32)

    def sm(zb, mask=bd_b):
        return jnp.concatenate([zb, zb], axis=0) * mask

    def b16(x):
        return x.astype(BF16)

    def hsum(x):
        hi, lo = _sp(x)
        s2 = _dg(jnp.concatenate([hi, lo], axis=0), bd_b, NN)
        return s2[:L] + s2[L:]

    def cumsum_rows(x):
        x1, x2, x3 = _split3(x)
        s3 = _dg(tri_b, jnp.concatenate([x1, x2, x3], axis=1), NN)
        w = x.shape[1]
        return s3[:, :w] + (s3[:, w:2 * w] + s3[:, 2 * w:])

    inv_n = 1.0 / RW_HEAD
    n_lvl = L.bit_length() - 1
    n_pairs = pr_ref.shape[1] // (2 * L)

    C = range(n_chunks * n_pairs)
    rows = [slice((c // n_pairs) * L, (c // n_pairs + 1) * L) for c in C]
    lanes = [slice((c % n_pairs) * 2 * L, (c % n_pairs + 1) * 2 * L) for c in C]
    fronts = {}

    def front(g):
        rs = slice(g * rg, (g + 1) * rg)
        if g == 0:
            first = pl.program_id(2) == 0
            prev = lambda p_ref, pp_ref: pp_ref[7:8, :]
        else:
            first = False
            prev = lambda p_ref, pp_ref: p_ref[g * rg - 1:g * rg, :]
        r_all = _token_shift(pr_ref[rs, :], prev(pr_ref, ppr_ref), mur_ref[...], first)
        yield
        k_all = _token_shift(pk_ref[rs, :], prev(pk_ref, ppk_ref), muk_ref[...], first)
        yield
        v_all = _token_shift(pv_ref[rs, :], prev(pv_ref, ppv_ref), muv_ref[...], first)
        yield
        lo = _token_shift(lo_ref[rs, :], prev(lo_ref, plo_ref), mul_ref[...], first)
        wl = lo[:, :RW_DECAY_LORA]
        al = lo[:, RW_DECAY_LORA:RW_DECAY_LORA + RW_AAA_LORA]
        gl = lo[:, RW_DECAY_LORA + RW_AAA_LORA:]
        yield
        z = w0_ref[...] + _dot3(jnp.tanh(wl), w2_ref[...])
        yield
        w_all = -(jnp.maximum(-z, 0.0) + jnp.log1p(jnp.exp(-jnp.abs(z)))) - 0.5
        yield
        a_all = _sigmoid(a0_ref[...] + _dot3(al, a2_ref[...]))
        yield
        g_all = _dot3(_sigmoid(gl), g2_ref[...])
        yield
        if has_vres:
            mix = _sigmoid(v0_ref[...] + _dot3(vlo_ref[rs, :], v2_ref[...]))
            v_all = v_all + (vf_ref[rs, :] - v_all) * mix
        else:
            vout_ref[rs, :] = v_all
        yield
        k_k = [kk_ref[:, lanes[c]] for c in C]
        k_a = [ka_ref[:, lanes[c]] for c in C]
        r_k = [rk_ref[:, lanes[c]] for c in C]
        r = [r_all[rows[c], lanes[c]] for c in C]
        k_raw = [k_all[rows[c], lanes[c]] for c in C]
        v = [v_all[rows[c], lanes[c]] for c in C]
        a = [a_all[rows[c], lanes[c]] for c in C]
        kk = [k_raw[c] * k_k[c] for c in C]
        ssq = [hsum(kk[c] * kk[c]) for c in C]
        yield
        lw = [-jnp.exp(w_all[rows[c], lanes[c]]) for c in C]
        cum = [cumsum_rows(lw[c]) for c in C]
        yield
        kk = [kk[c] * lax.rsqrt(jnp.maximum(ssq[c], 1e-24)) for c in C]
        k = [k_raw[c] * (1.0 + (a[c] - 1.0) * k_a[c]) for c in C]
        bv = [kk[c] * a[c] for c in C]
        yield
        tot = [cum[c][L - 1:L, :] for c in C]
        e_neg = [jnp.exp(-cum[c]) for c in C]
        rt = [r[c] * jnp.exp(cum[c]) for c in C]
        yield
        at = [-kk[c] * jnp.exp(cum[c] - lw[c]) for c in C]
        e_tot = [jnp.exp(tot[c] - cum[c]) for c in C]
        yield
        bh = [b16(bv[c] * e_tot[c]) for c in C]
        kh = [b16(k[c] * e_tot[c]) for c in C]
        vb = [b16(v[c]) for c in C]
        smv = [sm(vb[c]) for c in C]
        yield
        lhs2 = [b16(jnp.concatenate([at[c], rt[c]], axis=0)) for c in C]
        smkb = [jnp.concatenate([sm(b16(k[c] * e_neg[c])), sm(b16(bv[c] * e_neg[c]))], axis=0) for c in C]
        yield
        bonus = [hsum(r[c] * k[c] * r_k[c]) * v[c] for c in C]
        g_out = [g_all[rows[c], lanes[c]] for c in C]
        fronts[g] = (tot, rt, at, bh, kh, vb, smv, lhs2, smkb, bonus, g_out)

    def drain(gen):
        for _ in gen:
            pass

    def run_group(g, state, pump):
        tot, rt, at, bh, kh, vb, smv, lhs2, smkb, bonus, g_out = fronts.pop(g)
        akb = [_dg(lhs2[c], smkb[c], NT) for c in C]
        pump()
        ak = [akb[c][:, :2 * L] for c in C]
        ab = [akb[c][:, 2 * L:] for c in C]
        a_ak = [b16(jnp.where(strict, ak[c][:L], 0.0)) for c in C]
        a_rk = [b16(jnp.where(incl, ak[c][L:], 0.0)) for c in C]
        p = [jnp.where(strict, ab[c][:L], 0.0) for c in C]
        a_rb = [b16(jnp.where(incl, ab[c][L:], 0.0)) for c in C]
        akv = [_dg(a_ak[c], smv[c], NN) for c in C]
        pump()
        z = [jnp.concatenate([at[c], akv[c]], axis=1) for c in C]
        for lvl in range(n_lvl):
            pb = [b16(p[c]) for c in C]
            smz = [sm(b16(z[c]), bd2_b) for c in C]
            z = [z[c] + _dg(pb[c], smz[c], NN) for c in C]
            pump()
            if lvl + 1 < n_lvl:
                smp = [sm(pb[c]) for c in C]
                p = [_dg(pb[c], smp[c], NN) for c in C]
                pump()
        zb = [b16(z[c]) for c in C]
        qy = [_dg(a_rb[c], sm(zb[c], bd2_b), NN) for c in C]
        pump()
        q = [b16(rt[c] + qy[c][:, :2 * L]) for c in C]
        y0 = [_dg(a_rk[c], smv[c], NN) + qy[c][:, 2 * L:] for c in C]
        pump()
        zbh = [_dg(zb[c], bh[c], TN) for c in C]
        gam_s = [_sp(bd_f * zbh[c][:2 * L] + eye_f * jnp.exp(tot[c])) for c in C]
        cc = [bd_f * (_dg(vb[c], kh[c], TN) + zbh[c][2 * L:]) for c in C]
        pump()

        y = []
        for c in C:
            pi = c % n_pairs
            st_s = _sp(state[pi])
            y.append(_dg(q[c], st_s[0], NT) + y0[c])
            state[pi] = _d3s(st_s, gam_s[c]) + cc[c]
        mean = [hsum(y[c]) * inv_n for c in C]
        d = [y[c] - mean[c] for c in C]
        var = [hsum(d[c] * d[c]) * inv_n for c in C]
        for c in C:
            yn = d[c] * lax.rsqrt(var[c] + RW_GN_EPS) * gg_ref[:, lanes[c]] + gb_ref[:, lanes[c]]
            o_ref[pl.ds(g * rg + rows[c].start, L), lanes[c]] = ((yn + bonus[c]) * g_out[c]).astype(o_ref.dtype)
        return state

    state = [s_ref[pi] for pi in range(n_pairs)]
    drain(front(0))
    for g in range(n_groups):
        nxt = front(g + 1) if g + 1 < n_groups else iter(())
        state = run_group(g, state, lambda: next(nxt, None))
        drain(nxt)
    for pi in range(n_pairs):
        s_ref[pi] = state[pi]


def _wkv(p_rkv, p_lora, mu, w0, a0, w2, a2, g2, k_k, k_a, r_k, gn_g, gn_b, vres, batch, seq_len):
    m = p_rkv.shape[0]
    wd = RW_WIDTH
    tc = min(WKV_TC, seq_len)
    nt = seq_len // tc
    pair = 2 * RW_HEAD
    lanes = WKV_PAIRS * pair
    nb = wd // lanes
    row = lambda b, h, t: b * nt + t
    prv = lambda b, h, t: jnp.maximum((b * nt + t) * (tc // 8) - 1, 0)
    act = lambda sec: pl.BlockSpec((tc, lanes), lambda b, h, t: (row(b, h, t), sec * nb + h))
    prev = lambda sec: pl.BlockSpec((8, lanes), lambda b, h, t: (prv(b, h, t), sec * nb + h))
    par = lambda sec=0: pl.BlockSpec((1, lanes), lambda b, h, t: (0, sec * nb + h))
    mat = lambda k: pl.BlockSpec((k, lanes), lambda b, h, t: (0, h))
    lo_w = RW_LORA_COLS
    c3 = 3 * wd
    in_specs = [act(0), act(1), act(2), prev(0), prev(1), prev(2),
                pl.BlockSpec((tc, lo_w), lambda b, h, t: (row(b, h, t), 0)),
                pl.BlockSpec((8, lo_w), lambda b, h, t: (prv(b, h, t), 0)),
                par(0), par(1), par(2), pl.BlockSpec((1, lo_w), lambda b, h, t: (0, 0)),
                par(), par(), mat(RW_DECAY_LORA), mat(RW_AAA_LORA), mat(RW_GATE_LORA)] + [par()] * 5
    mu_rkv = mu[:c3].reshape(1, c3)
    args = [p_rkv, p_rkv, p_rkv, p_rkv, p_rkv, p_rkv, p_lora, p_lora,
            mu_rkv, mu_rkv, mu_rkv, mu[c3:].reshape(1, lo_w),
            w0.reshape(1, wd), a0.reshape(1, wd), w2, a2, g2]
    args += [z.reshape(1, wd) for z in (k_k, k_a, r_k, gn_g, gn_b)]
    out_act = pl.BlockSpec((tc, lanes), lambda b, h, t: (row(b, h, t), h))
    if vres is not None:
        v_first, v0, vlo, v2 = vres
        in_specs += [out_act, par(), pl.BlockSpec((tc, RW_MV_LORA), lambda b, h, t: (row(b, h, t), 0)),
                     mat(RW_MV_LORA)]
        args += [v_first, v0.reshape(1, wd), vlo, v2]
        out_specs = out_act
        out_shape = jax.ShapeDtypeStruct((m, wd), BF16)
    else:
        out_specs = [out_act, out_act]
        out_shape = [jax.ShapeDtypeStruct((m, wd), BF16), jax.ShapeDtypeStruct((m, wd), F32)]
    return pl.pallas_call(
        functools.partial(_wkv_kernel, vres is not None),
        grid=(batch, nb, nt),
        in_specs=in_specs,
        out_specs=out_specs,
        out_shape=out_shape,
        scratch_shapes=[pltpu.VMEM((WKV_PAIRS, pair, pair), F32)],
        compiler_params=_cparams(("parallel", "parallel", "arbitrary")),
        name="rwkv_wkv",
    )(*args)


def _gmlp_kernel(p_ref, lg_ref, lb_ref, ws_ref, bst_ref, o_ref):
    tm = p_ref.shape[0]
    u = _gelu_tanh(p_ref[:, :GM_WIDTH])
    v = _layer_norm(_gelu_tanh(p_ref[:, GM_WIDTH:]), lg_ref[...], lb_ref[...])
    rl = lax.broadcasted_iota(jnp.int32, (GM_CHUNK, GM_CHUNK), 0)
    cl = lax.broadcasted_iota(jnp.int32, (GM_CHUNK, GM_CHUNK), 1)
    causal = cl <= rl
    for g in range(GM_GROUPS):
        cols = slice(g * GM_GROUP, (g + 1) * GM_GROUP)
        wsg = jnp.where(causal, ws_ref[g], 0.0)
        bias = bst_ref[:, g:g + 1]
        for c in range(tm // GM_CHUNK):
            rows = slice(c * GM_CHUNK, (c + 1) * GM_CHUNK)
            s = _dot3(wsg, v[rows, cols]) + bias
            o_ref[rows, cols] = (u[rows, cols] * s).astype(o_ref.dtype)


def _gmlp(p_gm, ln_g, ln_b, ws, bs):
    m = p_gm.shape[0]
    tm = GM_TM
    fix2 = lambda i: (0, 0)
    return pl.pallas_call(
        _gmlp_kernel,
        grid=(m // tm,),
        in_specs=[
            pl.BlockSpec((tm, GM_COLS), lambda i: (i, 0)),
            pl.BlockSpec((1, GM_WIDTH), fix2),
            pl.BlockSpec((1, GM_WIDTH), fix2),
            pl.BlockSpec((GM_GROUPS, GM_CHUNK, GM_CHUNK), lambda i: (0, 0, 0)),
            pl.BlockSpec((GM_CHUNK, GM_GROUPS), fix2),
        ],
        out_specs=pl.BlockSpec((tm, GM_WIDTH), lambda i: (i, 0)),
        out_shape=jax.ShapeDtypeStruct((m, GM_WIDTH), BF16),
        compiler_params=_cparams(("parallel",)),
        name="gmlp_mix",
    )(p_gm, ln_g.reshape(1, -1), ln_b.reshape(1, -1), ws, bs.T)


def _compress_kernel(c_ref, pos_ref, w1_ref, w2_ref, o_ref):
    c = c_ref[...]
    n = c.shape[0]
    r1 = _dot3(c, w1_ref[0])
    r2 = _dot3(c, w1_ref[1])
    pos = jnp.broadcast_to(pos_ref[...], (8, pos_ref.shape[1]))
    half = w1_ref.shape[1]
    pterm = (_dot3(pos[:, :half], w1_ref[0]) + _dot3(pos[:, half:], w1_ref[1]))[0:1, :]
    h = r1 + pltpu.roll(r2, n - 1, 0) + pterm
    o_ref[...] = _dot3(_gelu_tanh(h), w2_ref[...])


def _compress(z, pos, w1, w2):
    gb, t, d = z.shape
    nb = t // CMP_STRIDE
    c = z.reshape(gb, nb, CMP_STRIDE * d)
    half = CMP_STRIDE * d
    return pl.pallas_call(
        _compress_kernel,
        grid=(gb,),
        in_specs=[
            pl.BlockSpec((None, nb, half), lambda i: (i, 0, 0)),
            pl.BlockSpec((1, 2 * half), lambda i: (0, 0)),
            pl.BlockSpec((2, half, d), lambda i: (0, 0, 0)),
            pl.BlockSpec((d, d), lambda i: (0, 0)),
        ],
        out_specs=pl.BlockSpec((None, nb, d), lambda i: (i, 0, 0)),
        out_shape=jax.ShapeDtypeStruct((gb, nb, d), F32),
        compiler_params=_cparams(("parallel",)),
        name="nsa_compress",
    )(c, pos.reshape(1, 2 * half), w1.reshape(2, half, d), w2)


def _nsa_proj_kernel(x_ref, wq_ref, wkv_ref, wgt_ref,
                     q_ref, kc_ref, vc_ref, ks_ref, vs_ref, kw_ref, vw_ref, gt_ref):
    x = x_ref[...]
    scale = NSA_DK ** -0.5
    rq = _dg(x, wq_ref[...], NN)
    for h in range(NSA_HPG):
        q_ref[h] = (rq[:, h * NSA_DK:(h + 1) * NSA_DK] * scale).astype(q_ref.dtype)
    rkv = _dg(x, wkv_ref[...], NN)
    o = 0
    for o_ref, d in ((vc_ref, NSA_DV), (vs_ref, NSA_DV), (vw_ref, NSA_DV),
                     (kc_ref, NSA_DK), (ks_ref, NSA_DK), (kw_ref, NSA_DK)):
        o_ref[...] = rkv[:, o:o + d].astype(o_ref.dtype)
        o += d
    gt_ref[...] = _dg(wgt_ref[...], x, NT)


def _nsa_proj(xb, wl):
    m, kdim = xb.shape
    G, hp = NSA_KV_GROUPS, NSA_HPG
    tm = min(MM_TM, m)
    o = OFF_NSA

    def take(width, d):
        nonlocal o
        w3 = wl[:, o:o + width].astype(BF16).reshape(kdim, width // d, d)
        o += width
        return w3

    wq = take(NSA_Q_COLS, hp * NSA_DK).reshape(kdim, NSA_Q_COLS)
    kc, vc, ks, vs, kw, vw = (take(NSA_GK, NSA_DK), take(NSA_GV, NSA_DV), take(NSA_GK, NSA_DK),
                              take(NSA_GV, NSA_DV), take(NSA_GK, NSA_DK), take(NSA_GV, NSA_DV))
    wkv = jnp.concatenate([vc, vs, vw, kc, ks, kw], axis=2).transpose(1, 0, 2)
    nkv = wkv.shape[2]
    wgt = take(NSA_G_COLS, 3 * hp).transpose(1, 2, 0)
    wgt = jnp.pad(wgt, ((0, 0), (0, NSA_GROWS - 3 * hp), (0, 0)))
    ospec = lambda d: pl.BlockSpec((None, tm, d), lambda i, g: (g, i, 0))
    oshape = lambda d, dt: jax.ShapeDtypeStruct((G, m, d), dt)
    return pl.pallas_call(
        _nsa_proj_kernel,
        grid=(m // tm, G),
        in_specs=[pl.BlockSpec((tm, kdim), lambda i, g: (i, 0)),
                  pl.BlockSpec((kdim, hp * NSA_DK), lambda i, g: (0, g)),
                  pl.BlockSpec((None, kdim, nkv), lambda i, g: (g, 0, 0)),
                  pl.BlockSpec((None, NSA_GROWS, kdim), lambda i, g: (g, 0, 0))],
        out_specs=[pl.BlockSpec((hp, tm, NSA_DK), lambda i, g: (g, i, 0)),
                   ospec(NSA_DK), ospec(NSA_DV), ospec(NSA_DK), ospec(NSA_DV), ospec(NSA_DK), ospec(NSA_DV),
                   pl.BlockSpec((None, NSA_GROWS, tm), lambda i, g: (g, 0, i))],
        out_shape=[jax.ShapeDtypeStruct((G * hp, m, NSA_DK), BF16),
                   oshape(NSA_DK, F32), oshape(NSA_DV, F32), oshape(NSA_DK, BF16), oshape(NSA_DV, BF16),
                   oshape(NSA_DK, BF16), oshape(NSA_DV, BF16),
                   jax.ShapeDtypeStruct((G, NSA_GROWS, m), F32)],
        compiler_params=_cparams(("parallel", "arbitrary")),
        name="nsa_proj",
    )(xb, wq, wkv, wgt)


def _nsa_kernel(seq_len, q_ref, gt_ref, kc_ref, vc_ref, ks_ref, vs_ref, kw_ref, vw_ref,
                o_ref, bias_ref):
    tq = NSA_TQ
    hp = NSA_HPG
    nr = hp * tq
    tk = min(NSA_TK, seq_len)
    wk = WIN + tq
    n_s = seq_len // SEL_BLK
    k_sel = min(N_SEL, n_s)
    n_cb = kc_ref.shape[0]
    i = pl.program_id(2)
    t0 = i * tq

    qa = q_ref[...].reshape(nr, NSA_DK)
    t_lane = t0 + lax.broadcasted_iota(jnp.int32, (1, tq), 1)
    t_all = t0 + (lax.broadcasted_iota(jnp.int32, (1, nr), 1) & (tq - 1))
    tile_heads = lambda z: jnp.concatenate([z] * hp, axis=1)

    w0 = pl.multiple_of(jnp.maximum(t0 - WIN, 0), tq)
    kw_tile = kw_ref[pl.ds(w0, wk), :]
    vw_tile = vw_ref[pl.ds(w0, wk), :]
    s_w = _dg(kw_tile, qa, NT)
    s_c = _dg(kc_ref[...].astype(BF16), qa, NT)

    n_end = lax.broadcasted_iota(jnp.int32, (n_cb, 1), 0) * CMP_STRIDE + (CMP_BLK - 1)
    m_c = (n_end <= t_all) & (n_end < seq_len)
    s_c = jnp.where(m_c, s_c, NEG)
    e_c = jnp.where(m_c, jnp.exp(s_c - jnp.max(s_c, axis=0, keepdims=True)), 0.0)
    p_c = e_c * (1.0 / jnp.maximum(jnp.sum(e_c, axis=0, keepdims=True), 1e-30))
    o_c = _dg(vc_ref[...].astype(BF16), p_c.astype(BF16), TN)

    p_sum = p_c[:, 0:tq]
    for h in range(1, hp):
        p_sum = p_sum + p_c[:, h * tq:(h + 1) * tq]
    ss = lax.broadcasted_iota(jnp.int32, (n_s, n_cb), 0) * SEL_BLK
    cs = lax.broadcasted_iota(jnp.int32, (n_s, n_cb), 1) * CMP_STRIDE
    overlap_t = ((cs < ss + SEL_BLK) & (cs + (CMP_BLK - 1) >= ss)
                 & (cs + (CMP_BLK - 1) < seq_len)).astype(BF16)
    imp = _dot_exact_lhs(overlap_t, p_sum, NN)

    kpos_w = w0 + lax.broadcasted_iota(jnp.int32, (wk, 1), 0)
    bias_w = jnp.where((kpos_w <= t_lane) & (kpos_w > t_lane - WIN), 0.0, NEG)
    s_w = s_w + tile_heads(bias_w)
    p_w = jnp.exp(s_w - jnp.max(s_w, axis=0, keepdims=True))
    l_w = jnp.sum(p_w, axis=0, keepdims=True)
    o_w = _dg(vw_tile, p_w.astype(BF16), TN) * (1.0 / jnp.maximum(l_w, 1e-30))

    blk = lax.broadcasted_iota(jnp.int32, (n_s, 1), 0)
    cur = t_lane // SEL_BLK
    valid = blk * SEL_BLK <= t_lane
    forced = valid & ((blk == 0) | (blk == cur) | (blk == cur - 1))
    score = jnp.where(forced, FORCED, jnp.where(valid, imp, NEG))
    rank = jnp.zeros((n_s, tq), jnp.int32)
    for s in range(n_s):
        row = score[s:s + 1, :]
        beats = (row > score) | ((row == score) & (blk > s))
        rank = rank + beats.astype(jnp.int32)
    sel_t = ((rank < k_sel) & (score > 0.5 * NEG)).astype(BF16)
    n_kt = (t0 + tq + tk - 1) // tk
    n_all = seq_len // tk
    half = max(n_all // 2, 1)

    def expand(kt0, kt1):
        width = (kt1 - kt0) * tk
        kpos = kt0 * tk + lax.broadcasted_iota(jnp.int32, (width, 1), 0)
        er = (lax.broadcasted_iota(jnp.int32, (width, n_s), 0) + kt0 * tk) // SEL_BLK
        ec = lax.broadcasted_iota(jnp.int32, (width, n_s), 1)
        chosen = _dg((er == ec).astype(BF16), sel_t, NN)
        bias = jnp.where((chosen > 0.5) & (kpos <= t_lane), 0.0, NEG)
        for kt in range(kt0, kt1):
            bias_ref[kt] = bias[(kt - kt0) * tk:(kt - kt0 + 1) * tk]

    expand(0, half)
    if n_all > half:
        @pl.when(n_kt > half)
        def _():
            expand(half, n_all)

    def sel_body(kt, carry):
        m_run, l_run, acc = carry
        k0 = pl.multiple_of(kt * tk, tk)
        s = _dg(ks_ref[pl.ds(k0, tk), :], qa, NT) + tile_heads(bias_ref[kt])
        m_new = jnp.maximum(m_run, jnp.max(s, axis=0, keepdims=True))
        alpha = jnp.exp(m_run - m_new)
        p = jnp.exp(s - m_new)
        l_new = alpha * l_run + jnp.sum(p, axis=0, keepdims=True)
        acc = alpha * acc + _dg(vs_ref[pl.ds(k0, tk), :], p.astype(BF16), TN)
        return m_new, l_new, acc

    init = (jnp.full((1, nr), NEG, F32), jnp.zeros((1, nr), F32), jnp.zeros((NSA_DV, nr), F32))
    _, l_s, acc_s = lax.fori_loop(0, n_kt, sel_body, init)
    o_s = acc_s * (1.0 / jnp.maximum(l_s, 1e-30))

    gates = _sigmoid(gt_ref[...])
    for h in range(hp):
        lanes = slice(h * tq, (h + 1) * tq)
        out_t = (gates[3 * h:3 * h + 1] * o_c[:, lanes] + gates[3 * h + 1:3 * h + 2] * o_s[:, lanes]
                 + gates[3 * h + 2:3 * h + 3] * o_w[:, lanes])
        o_ref[:, h * NSA_DV:(h + 1) * NSA_DV] = out_t.T.astype(o_ref.dtype)


def _nsa_attention(q, p_gt, k_cmp, v_cmp, ks, vs, kw, vw, batch, seq_len):
    m = q.shape[1]
    tq = NSA_TQ
    assert seq_len >= WIN + tq and seq_len % NSA_TK == 0
    nq = seq_len // tq
    n_cb = k_cmp.shape[1]
    tk = min(NSA_TK, seq_len)
    kv = lambda b, g, i: (g * batch + b, 0, 0)
    return pl.pallas_call(
        functools.partial(_nsa_kernel, seq_len),
        grid=(batch, NSA_KV_GROUPS, nq),
        in_specs=[
            pl.BlockSpec((NSA_HPG, tq, NSA_DK), lambda b, g, i: (g, b * nq + i, 0)),
            pl.BlockSpec((None, NSA_GROWS, tq), lambda b, g, i: (g, 0, b * nq + i)),
            pl.BlockSpec((None, n_cb, NSA_DK), kv),
            pl.BlockSpec((None, n_cb, NSA_DV), kv),
            pl.BlockSpec((None, seq_len, NSA_DK), kv),
            pl.BlockSpec((None, seq_len, NSA_DV), kv),
            pl.BlockSpec((None, seq_len, NSA_DK), kv),
            pl.BlockSpec((None, seq_len, NSA_DV), kv),
        ],
        out_specs=pl.BlockSpec((tq, NSA_HPG * NSA_DV), lambda b, g, i: (b * nq + i, g)),
        out_shape=jax.ShapeDtypeStruct((m, NSA_WIDTH), BF16),
        scratch_shapes=[pltpu.VMEM((seq_len // tk, tk, tq), F32)],
        compiler_params=_cparams(("parallel", "parallel", "arbitrary")),
        name="nsa_attention",
    )(q, p_gt, k_cmp, v_cmp, ks, vs, kw, vw)


def _merge_kernel(y0_ref, y1_ref, y2_ref, g0_ref, g1_ref, g2_ref, w_ref, o_ref):
    acc = _sigmoid(g0_ref[...]) * _dg(y0_ref[...], w_ref[0], NN)
    acc = acc + _sigmoid(g1_ref[...]) * _dg(y1_ref[...], w_ref[1], NN)
    acc = acc + _sigmoid(g2_ref[...]) * _dg(y2_ref[...], w_ref[2], NN)
    o_ref[...] = acc.astype(o_ref.dtype)


def _merge(y_rw, y_gm, y_ns, p_gate, w_br):
    m, d = y_rw.shape
    tm = min(MERGE_TM, m)
    tn = MERGE_TN
    nj = d // tn
    ys = pl.BlockSpec((tm, d), lambda i, j: (i, 0))
    gspec = lambda br: pl.BlockSpec((tm, tn), lambda i, j: (i, br * nj + j))
    return pl.pallas_call(
        _merge_kernel,
        grid=(m // tm, nj),
        in_specs=[ys, ys, ys, gspec(0), gspec(1), gspec(2),
                  pl.BlockSpec((N_BRANCH, d, tn), lambda i, j: (0, 0, j))],
        out_specs=pl.BlockSpec((tm, tn), lambda i, j: (i, j)),
        out_shape=jax.ShapeDtypeStruct((m, d), BF16),
        compiler_params=_cparams(("parallel", "arbitrary")),
        name="branch_merge",
    )(y_rw, y_gm, y_ns, p_gate, p_gate, p_gate, w_br)


def _out_ln_kernel(y_ref, w_ref, x_ref, g_ref, b_ref, o_ref):
    y = ALPHA * x_ref[...] + _dg(y_ref[...], w_ref[...], NN)
    o_ref[...] = _layer_norm(y, g_ref[...], b_ref[...])


def _out_ln(merged, w_o, x, g, b):
    m, d = x.shape
    tm = min(WO_TM, m)
    row = lambda i: (i, 0)
    fix = lambda i: (0, 0)
    return pl.pallas_call(
        _out_ln_kernel,
        grid=(m // tm,),
        in_specs=[pl.BlockSpec((tm, d), row), pl.BlockSpec((d, d), fix), pl.BlockSpec((tm, d), row),
                  pl.BlockSpec((1, d), fix), pl.BlockSpec((1, d), fix)],
        out_specs=pl.BlockSpec((tm, d), row),
        out_shape=jax.ShapeDtypeStruct((m, d), F32),
        compiler_params=_cparams(("parallel",)),
        name="out_proj_ln",
    )(merged, w_o, x, g.reshape(1, d), b.reshape(1, d))


def _rwkv_branch(xb, wl, mu, w0, w2, a0, a2, g2, k_k, k_a, r_k, gn_g, gn_b, vres, batch, seq_len):
    p_rkv = _mm(xb, wl[:, :3 * RW_WIDTH].astype(BF16), F32)
    p_lora = _mm(xb, wl[:, 3 * RW_WIDTH:RW_COLS].astype(BF16), F32, tn=RW_LORA_COLS)
    if vres is None:
        return _wkv(p_rkv, p_lora, mu, w0, a0, w2, a2, g2, k_k, k_a, r_k.reshape(-1), gn_g, gn_b,
                    None, batch, seq_len)
    v_first, v0, v1, v2 = vres
    vlo = _rw_vlora(p_rkv, mu[2 * RW_WIDTH:3 * RW_WIDTH], v1, seq_len)
    y = _wkv(p_rkv, p_lora, mu, w0, a0, w2, a2, g2, k_k, k_a, r_k.reshape(-1), gn_g, gn_b,
             (v_first, v0, vlo, v2), batch, seq_len)
    return y, v_first


def _nsa_branch(xb, wl, pos_k, pos_v, phi_k1, phi_k2, phi_v1, phi_v2, batch, seq_len):
    q, kc, vc, ks, vs, kw, vw, p_g = _nsa_proj(xb, wl)
    gb = NSA_KV_GROUPS * batch
    per_seq = lambda z: z.reshape(gb, seq_len, z.shape[-1])
    k_cmp = _compress(per_seq(kc), pos_k, phi_k1, phi_k2)
    v_cmp = _compress(per_seq(vc), pos_v, phi_v1, phi_v2)
    return _nsa_attention(q, p_g, k_cmp, v_cmp, per_seq(ks), per_seq(vs), per_seq(kw), per_seq(vw),
                          batch, seq_len)


def kernel(x, w_in, rw_mu, rw_w0, rw_w2, rw_a0, rw_a2, rw_g2, rw_v0, rw_v1, rw_v2, rw_k_k, rw_k_a, rw_r_k, rw_gn_g, rw_gn_b, gm_ln_g, gm_ln_b, gm_ws, gm_bs, nsa_pos_k, nsa_pos_v, nsa_phi_k1, nsa_phi_k2, nsa_phi_v1, nsa_phi_v2, w_br, w_o, ffn1_wg, ffn1_wu, ffn1_wd, ffn2_wg, ffn2_wu, ffn2_wd, ln_g, ln_b):
    batch, seq_len, d = x.shape
    m = batch * seq_len
    h = x.reshape(m, d)
    v_first = None
    for l in range(DEPTH):
        h = _ffn(h, ffn1_wg[l].astype(BF16), ffn1_wu[l].astype(BF16), ffn1_wd[l].astype(BF16),
                 ln_g[l, 0], ln_b[l, 0])
        hb = h.astype(BF16)
        wl = w_in[l]
        vres = None if l == 0 else (v_first, rw_v0[l - 1], rw_v1[l - 1], rw_v2[l - 1])
        y_rw, v_out = _rwkv_branch(hb, wl, rw_mu[l], rw_w0[l], rw_w2[l], rw_a0[l], rw_a2[l], rw_g2[l],
                                   rw_k_k[l], rw_k_a[l], rw_r_k[l], rw_gn_g[l], rw_gn_b[l], vres,
                                   batch, seq_len)
        if l == 0:
            v_first = v_out
        p_gm = _mm(hb, wl[:, OFF_GM:OFF_NSA].astype(BF16), F32)
        y_gm = _gmlp(p_gm, gm_ln_g[l], gm_ln_b[l], gm_ws[l], gm_bs[l])
        y_ns = _nsa_branch(hb, wl, nsa_pos_k[l], nsa_pos_v[l], nsa_phi_k1[l], nsa_phi_k2[l],
                           nsa_phi_v1[l], nsa_phi_v2[l], batch, seq_len)
        p_gate = _mm(hb, wl[:, OFF_GATE:].astype(BF16), F32)
        merged = _merge(y_rw, y_gm, y_ns, p_gate, w_br[l].astype(BF16))
        h = _out_ln(merged, w_o[l].astype(BF16), h, ln_g[l, 1], ln_b[l, 1])
        h = _ffn(h, ffn2_wg[l].astype(BF16), ffn2_wu[l].astype(BF16), ffn2_wd[l].astype(BF16),
                 ln_g[l, 2], ln_b[l, 2])
    return h.reshape(batch, seq_len, d)
```

```python
import functools

import jax
import jax.numpy as jnp
from jax import lax
from jax.experimental import pallas as pl
from jax.experimental.pallas import tpu as pltpu

F32 = jnp.float32
BF16 = jnp.bfloat16

D_MODEL = 2048
DEPTH = 2
RW_HEAD = 64
RW_WIDTH = D_MODEL
RW_DECAY_LORA = 96
RW_AAA_LORA = 96
RW_MV_LORA = 64
RW_GATE_LORA = 256
RW_LORA_COLS = RW_DECAY_LORA + RW_AAA_LORA + RW_GATE_LORA
RW_GN_EPS = RW_HEAD * 1e-5
GM_WIDTH = D_MODEL
GM_CHUNK = 128
GM_GROUP = 128
GM_GROUPS = GM_WIDTH // GM_GROUP
NSA_HEADS = 16
NSA_KV_GROUPS = 4
NSA_HPG = NSA_HEADS // NSA_KV_GROUPS
NSA_DK = 192
NSA_DV = 128
NSA_WIDTH = NSA_HEADS * NSA_DV
CMP_BLK = 32
CMP_STRIDE = 16
SEL_BLK = 64
N_SEL = 16
WIN = 512
N_BRANCH = 3
ALPHA = (2 * DEPTH) ** 0.25
LN_EPS = 1e-5
NEG = -1e30
FORCED = 1e6

RW_COLS = 3 * RW_WIDTH + RW_LORA_COLS
GM_COLS = 2 * GM_WIDTH
NSA_Q_COLS = NSA_HEADS * NSA_DK
NSA_GK = NSA_KV_GROUPS * NSA_DK
NSA_GV = NSA_KV_GROUPS * NSA_DV
NSA_KV_COLS = 3 * (NSA_GK + NSA_GV)
NSA_G_COLS = 3 * NSA_HEADS
NSA_COLS = NSA_Q_COLS + NSA_KV_COLS + NSA_G_COLS
OFF_GM = RW_COLS
OFF_NSA = OFF_GM + GM_COLS
OFF_GATE = OFF_NSA + NSA_COLS

V7X_VMEM_BYTES = 64 * 1024 * 1024
VMEM_LIMIT = V7X_VMEM_BYTES * 7 // 8

FFN_TM = 512
FFN_TF = 512
MM_TM = 1024
MM_TN = 512
MERGE_TM = 512
MERGE_TN = 512
WO_TM = 512
PREP_TM = 128
WKV_L = 64
WKV_TC = 1024
WKV_PAIRS = 2
WKV_GROUPS = 2
GM_TM = 256
NSA_TQ = 256
NSA_TK = 1024
NSA_GROWS = 16

NN = (((1,), (0,)), ((), ()))
NT = (((1,), (1,)), ((), ()))
TN = (((0,), (0,)), ((), ()))


def _cparams(sem):
    return pltpu.CompilerParams(dimension_semantics=sem, vmem_limit_bytes=VMEM_LIMIT)


def _dg(a, b, dims):
    return lax.dot_general(a, b, dims, preferred_element_type=F32)


def _split2(x):
    hi = x.astype(BF16)
    lo = (x - hi.astype(F32)).astype(BF16)
    return hi, lo


def _split3(x):
    x1 = x.astype(BF16)
    r1 = x - x1.astype(F32)
    x2 = r1.astype(BF16)
    x3 = (r1 - x2.astype(F32)).astype(BF16)
    return x1, x2, x3


def _dot3(a, b, dims=NN):
    ah, al = _split2(a)
    bh, bl = _split2(b)
    return _dg(ah, bh, dims) + (_dg(ah, bl, dims) + _dg(al, bh, dims))


def _dot_exact_lhs(m01, b, dims=NN):
    b1, b2, b3 = _split3(b)
    return _dg(m01, b1, dims) + (_dg(m01, b2, dims) + _dg(m01, b3, dims))


def _layer_norm(y, g, b):
    mu = jnp.mean(y, axis=-1, keepdims=True)
    d = y - mu
    var = jnp.mean(d * d, axis=-1, keepdims=True)
    return d * lax.rsqrt(var + LN_EPS) * g + b


def _gelu_tanh(x):
    return 0.5 * x * (1.0 + jnp.tanh(0.7978845608028654 * (x + 0.044715 * (x * x * x))))


def _sigmoid(x):
    return 1.0 / (1.0 + jnp.exp(-x))


def _ffn_kernel(x_ref, wg_ref, wu_ref, wd_ref, g_ref, b_ref, o_ref, xb_ref):
    j = pl.program_id(1)

    @pl.when(j == 0)
    def _():
        o_ref[...] = jnp.zeros_like(o_ref)
        xb_ref[...] = x_ref[...].astype(BF16)

    xb = xb_ref[...]
    hg = _dg(xb, wg_ref[...], NN)
    hu = _dg(xb, wu_ref[...], NN)
    h = (hg * _sigmoid(hg) * hu).astype(BF16)
    o_ref[...] += _dg(h, wd_ref[...], NN)

    @pl.when(j == pl.num_programs(1) - 1)
    def _():
        y = ALPHA * x_ref[...] + 0.5 * o_ref[...]
        o_ref[...] = _layer_norm(y, g_ref[...], b_ref[...])


def _ffn(x, wg, wu, wd, g, b):
    m, d = x.shape
    ff = wg.shape[1]
    tm = min(FFN_TM, m)
    return pl.pallas_call(
        _ffn_kernel,
        grid=(m // tm, ff // FFN_TF),
        in_specs=[
            pl.BlockSpec((tm, d), lambda i, j: (i, 0)),
            pl.BlockSpec((d, FFN_TF), lambda i, j: (0, j)),
            pl.BlockSpec((d, FFN_TF), lambda i, j: (0, j)),
            pl.BlockSpec((FFN_TF, d), lambda i, j: (j, 0)),
            pl.BlockSpec((1, d), lambda i, j: (0, 0)),
            pl.BlockSpec((1, d), lambda i, j: (0, 0)),
        ],
        out_specs=pl.BlockSpec((tm, d), lambda i, j: (i, 0)),
        out_shape=jax.ShapeDtypeStruct((m, d), F32),
        scratch_shapes=[pltpu.VMEM((tm, d), BF16)],
        compiler_params=_cparams(("parallel", "arbitrary")),
        name="ffn_swiglu_ln",
    )(x, wg, wu, wd, g.reshape(1, d), b.reshape(1, d))


def _mm_kernel(x_ref, w_ref, o_ref):
    o_ref[...] = _dg(x_ref[...], w_ref[...], NN).astype(o_ref.dtype)


def _mm(x, w, out_dtype, tn=MM_TN):
    m, k = x.shape
    n = w.shape[1]
    tm = min(MM_TM, m)
    tn = min(tn, n)
    return pl.pallas_call(
        _mm_kernel,
        grid=(m // tm, n // tn),
        in_specs=[
            pl.BlockSpec((tm, k), lambda i, j: (i, 0)),
            pl.BlockSpec((k, tn), lambda i, j: (0, j)),
        ],
        out_specs=pl.BlockSpec((tm, tn), lambda i, j: (i, j)),
        out_shape=jax.ShapeDtypeStruct((m, n), out_dtype),
        compiler_params=_cparams(("parallel", "arbitrary")),
        name="proj_mm",
    )(x, w)


def _token_shift(x, prev_row, mu, first):
    rows = lax.broadcasted_iota(jnp.int32, x.shape, 0)
    prev_row = jnp.where(first, jnp.zeros_like(prev_row), prev_row)
    shifted = jnp.where(rows == 0, prev_row, pltpu.roll(x, 1, 0))
    return x + (shifted - x) * mu


def _rw_vlora_kernel(seq_len, p_ref, pp_ref, mu_ref, v1_ref, o_ref):
    first = (pl.program_id(0) * p_ref.shape[0]) % seq_len == 0
    v = _token_shift(p_ref[...], pp_ref[7:8, :], mu_ref[...], first)
    o_ref[...] = _dot3(v, v1_ref[...])


def _rw_vlora(p_rkv, mu_v, v1, seq_len):
    m = p_rkv.shape[0]
    tm = PREP_TM
    wd = RW_WIDTH
    return pl.pallas_call(
        functools.partial(_rw_vlora_kernel, seq_len),
        grid=(m // tm,),
        in_specs=[pl.BlockSpec((tm, wd), lambda i: (i, 2)),
                  pl.BlockSpec((8, wd), lambda i: (jnp.maximum(i * (tm // 8) - 1, 0), 2)),
                  pl.BlockSpec((1, wd), lambda i: (0, 0)),
                  pl.BlockSpec((wd, RW_MV_LORA), lambda i: (0, 0))],
        out_specs=pl.BlockSpec((tm, RW_MV_LORA), lambda i: (i, 0)),
        out_shape=jax.ShapeDtypeStruct((m, RW_MV_LORA), F32),
        compiler_params=_cparams(("parallel",)),
        name="rwkv_vlora",
    )(p_rkv, p_rkv, mu_v.reshape(1, wd), v1)


def _sp(x):
    return _split2(x)


def _d3s(a, b, dims=NN):
    return _dg(a[0], b[0], dims) + (_dg(a[0], b[1], dims) + _dg(a[1], b[0], dims))


def _wkv_kernel(has_vres, *refs):
    (pr_ref, pk_ref, pv_ref, ppr_ref, ppk_ref, ppv_ref, lo_ref, plo_ref,
     mur_ref, muk_ref, muv_ref, mul_ref, w0_ref, a0_ref, w2_ref, a2_ref, g2_ref,
     kk_ref, ka_ref, rk_ref, gg_ref, gb_ref) = refs[:22]
    if has_vres:
        vf_ref, v0_ref, vlo_ref, v2_ref, o_ref, s_ref = refs[22:]
    else:
        o_ref, vout_ref, s_ref = refs[22:]
    L = WKV_L
    n_groups = WKV_GROUPS
    rg = pr_ref.shape[0] // n_groups
    n_chunks = rg // L

    @pl.when(pl.program_id(2) == 0)
    def _():
        s_ref[...] = jnp.zeros_like(s_ref)

    ri = lax.broadcasted_iota(jnp.int32, (2 * L, 2 * L), 0)
    ci = lax.broadcasted_iota(jnp.int32, (2 * L, 2 * L), 1)
    bd_f = ((ri // L) == (ci // L)).astype(F32)
    bd_b = bd_f.astype(BF16)
    rl = lax.broadcasted_iota(jnp.int32, (L, L), 0)
    cl = lax.broadcasted_iota(jnp.int32, (L, L), 1)
    tri_b = (cl <= rl).astype(BF16)
    rt_i = lax.broadcasted_iota(jnp.int32, (L, 2 * L), 0)
    cs_i = lax.broadcasted_iota(jnp.int32, (L, 2 * L), 1) % L
    strict = cs_i < rt_i
    incl = cs_i <= rt_i

    bd2_b = jnp.concatenate([bd_b, bd_b], axis=1)
    eye_f = (ri == ci).astype(F32)

    def sm(zb, mask=bd_b):
        return jnp.concatenate([zb, zb], axis=0) * mask

    def b16(x):
        return x.astype(BF16)

    def hsum(x):
        hi, lo = _sp(x)
        s2 = _dg(jnp.concatenate([hi, lo], axis=0), bd_b, NN)
        return s2[:L] + s2[L:]

    def cumsum_rows(x):
        x1, x2, x3 = _split3(x)
        s3 = _dg(tri_b, jnp.concatenate([x1, x2, x3], axis=1), NN)
        w = x.shape[1]
        return s3[:, :w] + (s3[:, w:2 * w] + s3[:, 2 * w:])

    inv_n = 1.0 / RW_HEAD
    n_lvl = L.bit_length() - 1
    n_pairs = pr_ref.shape[1] // (2 * L)

    C = range(n_chunks * n_pairs)
    rows = [slice((c // n_pairs) * L, (c // n_pairs + 1) * L) for c in C]
    lanes = [slice((c % n_pairs) * 2 * L, (c % n_pairs + 1) * 2 * L) for c in C]
    fronts = {}

    def front(g):
        rs = slice(g * rg, (g + 1) * rg)
        if g == 0:
            first = pl.program_id(2) == 0
            prev = lambda p_ref, pp_ref: pp_ref[7:8, :]
        else:
            first = False
            prev = lambda p_ref, pp_ref: p_ref[g * rg - 1:g * rg, :]
        r_all = _token_shift(pr_ref[rs, :], prev(pr_ref, ppr_ref), mur_ref[...], first)
        yield
        k_all = _token_shift(pk_ref[rs, :], prev(pk_ref, ppk_ref), muk_ref[...], first)
        yield
        v_all = _token_shift(pv_ref[rs, :], prev(pv_ref, ppv_ref), muv_ref[...], first)
        yield
        lo = _token_shift(lo_ref[rs, :], prev(lo_ref, plo_ref), mul_ref[...], first)
        wl = lo[:, :RW_DECAY_LORA]
        al = lo[:, RW_DECAY_LORA:RW_DECAY_LORA + RW_AAA_LORA]
        gl = lo[:, RW_DECAY_LORA + RW_AAA_LORA:]
        yield
        z = w0_ref[...] + _dot3(jnp.tanh(wl), w2_ref[...])
        yield
        w_all = -(jnp.maximum(-z, 0.0) + jnp.log1p(jnp.exp(-jnp.abs(z)))) - 0.5
        yield
        a_all = _sigmoid(a0_ref[...] + _dot3(al, a2_ref[...]))
        yield
        g_all = _dot3(_sigmoid(gl), g2_ref[...])
        yield
        if has_vres:
            mix = _sigmoid(v0_ref[...] + _dot3(vlo_ref[rs, :], v2_ref[...]))
            v_all = v_all + (vf_ref[rs, :] - v_all) * mix
        else:
            vout_ref[rs, :] = v_all
        yield
        k_k = [kk_ref[:, lanes[c]] for c in C]
        k_a = [ka_ref[:, lanes[c]] for c in C]
        r_k = [rk_ref[:, lanes[c]] for c in C]
        r = [r_all[rows[c], lanes[c]] for c in C]
        k_raw = [k_all[rows[c], lanes[c]] for c in C]
        v = [v_all[rows[c], lanes[c]] for c in C]
        a = [a_all[rows[c], lanes[c]] for c in C]
        kk = [k_raw[c] * k_k[c] for c in C]
        ssq = [hsum(kk[c] * kk[c]) for c in C]
        yield
        lw = [-jnp.exp(w_all[rows[c], lanes[c]]) for c in C]
        cum = [cumsum_rows(lw[c]) for c in C]
        yield
        kk = [kk[c] * lax.rsqrt(jnp.maximum(ssq[c], 1e-24)) for c in C]
        k = [k_raw[c] * (1.0 + (a[c] - 1.0) * k_a[c]) for c in C]
        bv = [kk[c] * a[c] for c in C]
        yield
        tot = [cum[c][L - 1:L, :] for c in C]
        e_neg = [jnp.exp(-cum[c]) for c in C]
        rt = [r[c] * jnp.exp(cum[c]) for c in C]
        yield
        at = [-kk[c] * jnp.exp(cum[c] - lw[c]) for c in C]
        e_tot = [jnp.exp(tot[c] - cum[c]) for c in C]
        yield
        bh = [b16(bv[c] * e_tot[c]) for c in C]
        kh = [b16(k[c] * e_tot[c]) for c in C]
        vb = [b16(v[c]) for c in C]
        smv = [sm(vb[c]) for c in C]
        yield
        lhs2 = [b16(jnp.concatenate([at[c], rt[c]], axis=0)) for c in C]
        smkb = [jnp.concatenate([sm(b16(k[c] * e_neg[c])), sm(b16(bv[c] * e_neg[c]))], axis=0) for c in C]
        yield
        bonus = [hsum(r[c] * k[c] * r_k[c]) * v[c] for c in C]
        g_out = [g_all[rows[c], lanes[c]] for c in C]
        fronts[g] = (tot, rt, at, bh, kh, vb, smv, lhs2, smkb, bonus, g_out)

    def drain(gen):
        for _ in gen:
            pass

    def run_group(g, state, pump):
        tot, rt, at, bh, kh, vb, smv, lhs2, smkb, bonus, g_out = fronts.pop(g)
        akb = [_dg(lhs2[c], smkb[c], NT) for c in C]
        pump()
        ak = [akb[c][:, :2 * L] for c in C]
        ab = [akb[c][:, 2 * L:] for c in C]
        a_ak = [b16(jnp.where(strict, ak[c][:L], 0.0)) for c in C]
        a_rk = [b16(jnp.where(incl, ak[c][L:], 0.0)) for c in C]
        p = [jnp.where(strict, ab[c][:L], 0.0) for c in C]
        a_rb = [b16(jnp.where(incl, ab[c][L:], 0.0)) for c in C]
        akv = [_dg(a_ak[c], smv[c], NN) for c in C]
        pump()
        z = [jnp.concatenate([at[c], akv[c]], axis=1) for c in C]
        for lvl in range(n_lvl):
            pb = [b16(p[c]) for c in C]
            smz = [sm(b16(z[c]), bd2_b) for c in C]
            z = [z[c] + _dg(pb[c], smz[c], NN) for c in C]
            pump()
            if lvl + 1 < n_lvl:
                smp = [sm(pb[c]) for c in C]
                p = [_dg(pb[c], smp[c], NN) for c in C]
                pump()
        zb = [b16(z[c]) for c in C]
        qy = [_dg(a_rb[c], sm(zb[c], bd2_b), NN) for c in C]
        pump()
        q = [b16(rt[c] + qy[c][:, :2 * L]) for c in C]
        y0 = [_dg(a_rk[c], smv[c], NN) + qy[c][:, 2 * L:] for c in C]
        pump()
        zbh = [_dg(zb[c], bh[c], TN) for c in C]
        gam_s = [_sp(bd_f * zbh[c][:2 * L] + eye_f * jnp.exp(tot[c])) for c in C]
        cc = [bd_f * (_dg(vb[c], kh[c], TN) + zbh[c][2 * L:]) for c in C]
        pump()

        y = []
        for c in C:
            pi = c % n_pairs
            st_s = _sp(state[pi])
            y.append(_dg(q[c], st_s[0], NT) + y0[c])
            state[pi] = _d3s(st_s, gam_s[c]) + cc[c]
        mean = [hsum(y[c]) * inv_n for c in C]
        d = [y[c] - mean[c] for c in C]
        var = [hsum(d[c] * d[c]) * inv_n for c in C]
        for c in C:
            yn = d[c] * lax.rsqrt(var[c] + RW_GN_EPS) * gg_ref[:, lanes[c]] + gb_ref[:, lanes[c]]
            o_ref[pl.ds(g * rg + rows[c].start, L), lanes[c]] = ((yn + bonus[c]) * g_out[c]).astype(o_ref.dtype)
        return state

    state = [s_ref[pi] for pi in range(n_pairs)]
    drain(front(0))
    for g in range(n_groups):
        nxt = front(g + 1) if g + 1 < n_groups else iter(())
        state = run_group(g, state, lambda: next(nxt, None))
        drain(nxt)
    for pi in range(n_pairs):
        s_ref[pi] = state[pi]


def _wkv(p_rkv, p_lora, mu, w0, a0, w2, a2, g2, k_k, k_a, r_k, gn_g, gn_b, vres, batch, seq_len):
    m = p_rkv.shape[0]
    wd = RW_WIDTH
    tc = min(WKV_TC, seq_len)
    nt = seq_len // tc
    pair = 2 * RW_HEAD
    lanes = WKV_PAIRS * pair
    nb = wd // lanes
    row = lambda b, h, t: b * nt + t
    prv = lambda b, h, t: jnp.maximum((b * nt + t) * (tc // 8) - 1, 0)
    act = lambda sec: pl.BlockSpec((tc, lanes), lambda b, h, t: (row(b, h, t), sec * nb + h))
    prev = lambda sec: pl.BlockSpec((8, lanes), lambda b, h, t: (prv(b, h, t), sec * nb + h))
    par = lambda sec=0: pl.BlockSpec((1, lanes), lambda b, h, t: (0, sec * nb + h))
    mat = lambda k: pl.BlockSpec((k, lanes), lambda b, h, t: (0, h))
    lo_w = RW_LORA_COLS
    c3 = 3 * wd
    in_specs = [act(0), act(1), act(2), prev(0), prev(1), prev(2),
                pl.BlockSpec((tc, lo_w), lambda b, h, t: (row(b, h, t), 0)),
                pl.BlockSpec((8, lo_w), lambda b, h, t: (prv(b, h, t), 0)),
                par(0), par(1), par(2), pl.BlockSpec((1, lo_w), lambda b, h, t: (0, 0)),
                par(), par(), mat(RW_DECAY_LORA), mat(RW_AAA_LORA), mat(RW_GATE_LORA)] + [par()] * 5
    mu_rkv = mu[:c3].reshape(1, c3)
    args = [p_rkv, p_rkv, p_rkv, p_rkv, p_rkv, p_rkv, p_lora, p_lora,
            mu_rkv, mu_rkv, mu_rkv, mu[c3:].reshape(1, lo_w),
            w0.reshape(1, wd), a0.reshape(1, wd), w2, a2, g2]
    args += [z.reshape(1, wd) for z in (k_k, k_a, r_k, gn_g, gn_b)]
    out_act = pl.BlockSpec((tc, lanes), lambda b, h, t: (row(b, h, t), h))
    if vres is not None:
        v_first, v0, vlo, v2 = vres
        in_specs += [out_act, par(), pl.BlockSpec((tc, RW_MV_LORA), lambda b, h, t: (row(b, h, t), 0)),
                     mat(RW_MV_LORA)]
        args += [v_first, v0.reshape(1, wd), vlo, v2]
        out_specs = out_act
        out_shape = jax.ShapeDtypeStruct((m, wd), BF16)
    else:
        out_specs = [out_act, out_act]
        out_shape = [jax.ShapeDtypeStruct((m, wd), BF16), jax.ShapeDtypeStruct((m, wd), F32)]
    return pl.pallas_call(
        functools.partial(_wkv_kernel, vres is not None),
        grid=(batch, nb, nt),
        in_specs=in_specs,
        out_specs=out_specs,
        out_shape=out_shape,
        scratch_shapes=[pltpu.VMEM((WKV_PAIRS, pair, pair), F32)],
        compiler_params=_cparams(("parallel", "parallel", "arbitrary")),
        name="rwkv_wkv",
    )(*args)


def _gmlp_kernel(p_ref, lg_ref, lb_ref, ws_ref, bst_ref, o_ref):
    tm = p_ref.shape[0]
    u = _gelu_tanh(p_ref[:, :GM_WIDTH])
    v = _layer_norm(_gelu_tanh(p_ref[:, GM_WIDTH:]), lg_ref[...], lb_ref[...])
    rl = lax.broadcasted_iota(jnp.int32, (GM_CHUNK, GM_CHUNK), 0)
    cl = lax.broadcasted_iota(jnp.int32, (GM_CHUNK, GM_CHUNK), 1)
    causal = cl <= rl
    for g in range(GM_GROUPS):
        cols = slice(g * GM_GROUP, (g + 1) * GM_GROUP)
        wsg = jnp.where(causal, ws_ref[g], 0.0)
        bias = bst_ref[:, g:g + 1]
        for c in range(tm // GM_CHUNK):
            rows = slice(c * GM_CHUNK, (c + 1) * GM_CHUNK)
            s = _dot3(wsg, v[rows, cols]) + bias
            o_ref[rows, cols] = (u[rows, cols] * s).astype(o_ref.dtype)


def _gmlp(p_gm, ln_g, ln_b, ws, bs):
    m = p_gm.shape[0]
    tm = GM_TM
    fix2 = lambda i: (0, 0)
    return pl.pallas_call(
        _gmlp_kernel,
        grid=(m // tm,),
        in_specs=[
            pl.BlockSpec((tm, GM_COLS), lambda i: (i, 0)),
            pl.BlockSpec((1, GM_WIDTH), fix2),
            pl.BlockSpec((1, GM_WIDTH), fix2),
            pl.BlockSpec((GM_GROUPS, GM_CHUNK, GM_CHUNK), lambda i: (0, 0, 0)),
            pl.BlockSpec((GM_CHUNK, GM_GROUPS), fix2),
        ],
        out_specs=pl.BlockSpec((tm, GM_WIDTH), lambda i: (i, 0)),
        out_shape=jax.ShapeDtypeStruct((m, GM_WIDTH), BF16),
        compiler_params=_cparams(("parallel",)),
        name="gmlp_mix",
    )(p_gm, ln_g.reshape(1, -1), ln_b.reshape(1, -1), ws, bs.T)


def _compress_kernel(c_ref, pos_ref, w1_ref, w2_ref, o_ref):
    c = c_ref[...]
    n = c.shape[0]
    r1 = _dot3(c, w1_ref[0])
    r2 = _dot3(c, w1_ref[1])
    pos = jnp.broadcast_to(pos_ref[...], (8, pos_ref.shape[1]))
    half = w1_ref.shape[1]
    pterm = (_dot3(pos[:, :half], w1_ref[0]) + _dot3(pos[:, half:], w1_ref[1]))[0:1, :]
    h = r1 + pltpu.roll(r2, n - 1, 0) + pterm
    o_ref[...] = _dot3(_gelu_tanh(h), w2_ref[...])


def _compress(z, pos, w1, w2):
    gb, t, d = z.shape
    nb = t // CMP_STRIDE
    c = z.reshape(gb, nb, CMP_STRIDE * d)
    half = CMP_STRIDE * d
    return pl.pallas_call(
        _compress_kernel,
        grid=(gb,),
        in_specs=[
            pl.BlockSpec((None, nb, half), lambda i: (i, 0, 0)),
            pl.BlockSpec((1, 2 * half), lambda i: (0, 0)),
            pl.BlockSpec((2, half, d), lambda i: (0, 0, 0)),
            pl.BlockSpec((d, d), lambda i: (0, 0)),
        ],
        out_specs=pl.BlockSpec((None, nb, d), lambda i: (i, 0, 0)),
        out_shape=jax.ShapeDtypeStruct((gb, nb, d), F32),
        compiler_params=_cparams(("parallel",)),
        name="nsa_compress",
    )(c, pos.reshape(1, 2 * half), w1.reshape(2, half, d), w2)


def _nsa_proj_kernel(x_ref, wq_ref, wkv_ref, wgt_ref,
                     q_ref, kc_ref, vc_ref, ks_ref, vs_ref, kw_ref, vw_ref, gt_ref):
    x = x_ref[...]
    scale = NSA_DK ** -0.5
    rq = _dg(x, wq_ref[...], NN)
    for h in range(NSA_HPG):
        q_ref[h] = (rq[:, h * NSA_DK:(h + 1) * NSA_DK] * scale).astype(q_ref.dtype)
    rkv = _dg(x, wkv_ref[...], NN)
    o = 0
    for o_ref, d in ((vc_ref, NSA_DV), (vs_ref, NSA_DV), (vw_ref, NSA_DV),
                     (kc_ref, NSA_DK), (ks_ref, NSA_DK), (kw_ref, NSA_DK)):
        o_ref[...] = rkv[:, o:o + d].astype(o_ref.dtype)
        o += d
    gt_ref[...] = _dg(wgt_ref[...], x, NT)


def _nsa_proj(xb, wl):
    m, kdim = xb.shape
    G, hp = NSA_KV_GROUPS, NSA_HPG
    tm = min(MM_TM, m)
    o = OFF_NSA

    def take(width, d):
        nonlocal o
        w3 = wl[:, o:o + width].astype(BF16).reshape(kdim, width // d, d)
        o += width
        return w3

    wq = take(NSA_Q_COLS, hp * NSA_DK).reshape(kdim, NSA_Q_COLS)
    kc, vc, ks, vs, kw, vw = (take(NSA_GK, NSA_DK), take(NSA_GV, NSA_DV), take(NSA_GK, NSA_DK),
                              take(NSA_GV, NSA_DV), take(NSA_GK, NSA_DK), take(NSA_GV, NSA_DV))
    wkv = jnp.concatenate([vc, vs, vw, kc, ks, kw], axis=2).transpose(1, 0, 2)
    nkv = wkv.shape[2]
    wgt = take(NSA_G_COLS, 3 * hp).transpose(1, 2, 0)
    wgt = jnp.pad(wgt, ((0, 0), (0, NSA_GROWS - 3 * hp), (0, 0)))
    ospec = lambda d: pl.BlockSpec((None, tm, d), lambda i, g: (g, i, 0))
    oshape = lambda d, dt: jax.ShapeDtypeStruct((G, m, d), dt)
    return pl.pallas_call(
        _nsa_proj_kernel,
        grid=(m // tm, G),
        in_specs=[pl.BlockSpec((tm, kdim), lambda i, g: (i, 0)),
                  pl.BlockSpec((kdim, hp * NSA_DK), lambda i, g: (0, g)),
                  pl.BlockSpec((None, kdim, nkv), lambda i, g: (g, 0, 0)),
                  pl.BlockSpec((None, NSA_GROWS, kdim), lambda i, g: (g, 0, 0))],
        out_specs=[pl.BlockSpec((hp, tm, NSA_DK), lambda i, g: (g, i, 0)),
                   ospec(NSA_DK), ospec(NSA_DV), ospec(NSA_DK), ospec(NSA_DV), ospec(NSA_DK), ospec(NSA_DV),
                   pl.BlockSpec((None, NSA_GROWS, tm), lambda i, g: (g, 0, i))],
        out_shape=[jax.ShapeDtypeStruct((G * hp, m, NSA_DK), BF16),
                   oshape(NSA_DK, F32), oshape(NSA_DV, F32), oshape(NSA_DK, BF16), oshape(NSA_DV, BF16),
                   oshape(NSA_DK, BF16), oshape(NSA_DV, BF16),
                   jax.ShapeDtypeStruct((G, NSA_GROWS, m), F32)],
        compiler_params=_cparams(("parallel", "arbitrary")),
        name="nsa_proj",
    )(xb, wq, wkv, wgt)


def _nsa_kernel(seq_len, q_ref, gt_ref, kc_ref, vc_ref, ks_ref, vs_ref, kw_ref, vw_ref,
                o_ref, bias_ref):
    tq = NSA_TQ
    hp = NSA_HPG
    nr = hp * tq
    tk = min(NSA_TK, seq_len)
    wk = WIN + tq
    n_s = seq_len // SEL_BLK
    k_sel = min(N_SEL, n_s)
    n_cb = kc_ref.shape[0]
    i = pl.program_id(2)
    t0 = i * tq

    qa = q_ref[...].reshape(nr, NSA_DK)
    t_lane = t0 + lax.broadcasted_iota(jnp.int32, (1, tq), 1)
    t_all = t0 + (lax.broadcasted_iota(jnp.int32, (1, nr), 1) & (tq - 1))
    tile_heads = lambda z: jnp.concatenate([z] * hp, axis=1)

    w0 = pl.multiple_of(jnp.maximum(t0 - WIN, 0), tq)
    kw_tile = kw_ref[pl.ds(w0, wk), :]
    vw_tile = vw_ref[pl.ds(w0, wk), :]
    s_w = _dg(kw_tile, qa, NT)
    s_c = _dg(kc_ref[...].astype(BF16), qa, NT)

    n_end = lax.broadcasted_iota(jnp.int32, (n_cb, 1), 0) * CMP_STRIDE + (CMP_BLK - 1)
    m_c = (n_end <= t_all) & (n_end < seq_len)
    s_c = jnp.where(m_c, s_c, NEG)
    e_c = jnp.where(m_c, jnp.exp(s_c - jnp.max(s_c, axis=0, keepdims=True)), 0.0)
    p_c = e_c * (1.0 / jnp.maximum(jnp.sum(e_c, axis=0, keepdims=True), 1e-30))
    o_c = _dg(vc_ref[...].astype(BF16), p_c.astype(BF16), TN)

    p_sum = p_c[:, 0:tq]
    for h in range(1, hp):
        p_sum = p_sum + p_c[:, h * tq:(h + 1) * tq]
    ss = lax.broadcasted_iota(jnp.int32, (n_s, n_cb), 0) * SEL_BLK
    cs = lax.broadcasted_iota(jnp.int32, (n_s, n_cb), 1) * CMP_STRIDE
    overlap_t = ((cs < ss + SEL_BLK) & (cs + (CMP_BLK - 1) >= ss)
                 & (cs + (CMP_BLK - 1) < seq_len)).astype(BF16)
    imp = _dot_exact_lhs(overlap_t, p_sum, NN)

    kpos_w = w0 + lax.broadcasted_iota(jnp.int32, (wk, 1), 0)
    bias_w = jnp.where((kpos_w <= t_lane) & (kpos_w > t_lane - WIN), 0.0, NEG)
    s_w = s_w + tile_heads(bias_w)
    p_w = jnp.exp(s_w - jnp.max(s_w, axis=0, keepdims=True))
    l_w = jnp.sum(p_w, axis=0, keepdims=True)
    o_w = _dg(vw_tile, p_w.astype(BF16), TN) * (1.0 / jnp.maximum(l_w, 1e-30))

    blk = lax.broadcasted_iota(jnp.int32, (n_s, 1), 0)
    cur = t_lane // SEL_BLK
    valid = blk * SEL_BLK <= t_lane
    forced = valid & ((blk == 0) | (blk == cur) | (blk == cur - 1))
    score = jnp.where(forced, FORCED, jnp.where(valid, imp, NEG))
    rank = jnp.zeros((n_s, tq), jnp.int32)
    for s in range(n_s):
        row = score[s:s + 1, :]
        beats = (row > score) | ((row == score) & (blk > s))
        rank = rank + beats.astype(jnp.int32)
    sel_t = ((rank < k_sel) & (score > 0.5 * NEG)).astype(BF16)
    n_kt = (t0 + tq + tk - 1) // tk
    n_all = seq_len // tk
    half = max(n_all // 2, 1)

    def expand(kt0, kt1):
        width = (kt1 - kt0) * tk
        kpos = kt0 * tk + lax.broadcasted_iota(jnp.int32, (width, 1), 0)
        er = (lax.broadcasted_iota(jnp.int32, (width, n_s), 0) + kt0 * tk) // SEL_BLK
        ec = lax.broadcasted_iota(jnp.int32, (width, n_s), 1)
        chosen = _dg((er == ec).astype(BF16), sel_t, NN)
        bias = jnp.where((chosen > 0.5) & (kpos <= t_lane), 0.0, NEG)
        for kt in range(kt0, kt1):
            bias_ref[kt] = bias[(kt - kt0) * tk:(kt - kt0 + 1) * tk]

    expand(0, half)
    if n_all > half:
        @pl.when(n_kt > half)
        def _():
            expand(half, n_all)

    def sel_body(kt, carry):
        m_run, l_run, acc = carry
        k0 = pl.multiple_of(kt * tk, tk)
        s = _dg(ks_ref[pl.ds(k0, tk), :], qa, NT) + tile_heads(bias_ref[kt])
        m_new = jnp.maximum(m_run, jnp.max(s, axis=0, keepdims=True))
        alpha = jnp.exp(m_run - m_new)
        p = jnp.exp(s - m_new)
        l_new = alpha * l_run + jnp.sum(p, axis=0, keepdims=True)
        acc = alpha * acc + _dg(vs_ref[pl.ds(k0, tk), :], p.astype(BF16), TN)
        return m_new, l_new, acc

    init = (jnp.full((1, nr), NEG, F32), jnp.zeros((1, nr), F32), jnp.zeros((NSA_DV, nr), F32))
    _, l_s, acc_s = lax.fori_loop(0, n_kt, sel_body, init)
    o_s = acc_s * (1.0 / jnp.maximum(l_s, 1e-30))

    gates = _sigmoid(gt_ref[...])
    for h in range(hp):
        lanes = slice(h * tq, (h + 1) * tq)
        out_t = (gates[3 * h:3 * h + 1] * o_c[:, lanes] + gates[3 * h + 1:3 * h + 2] * o_s[:, lanes]
                 + gates[3 * h + 2:3 * h + 3] * o_w[:, lanes])
        o_ref[:, h * NSA_DV:(h + 1) * NSA_DV] = out_t.T.astype(o_ref.dtype)


def _nsa_attention(q, p_gt, k_cmp, v_cmp, ks, vs, kw, vw, batch, seq_len):
    m = q.shape[1]
    tq = NSA_TQ
    assert seq_len >= WIN + tq and seq_len % NSA_TK == 0
    nq = seq_len // tq
    n_cb = k_cmp.shape[1]
    tk = min(NSA_TK, seq_len)
    kv = lambda b, g, i: (g * batch + b, 0, 0)
    return pl.pallas_call(
        functools.partial(_nsa_kernel, seq_len),
        grid=(batch, NSA_KV_GROUPS, nq),
        in_specs=[
            pl.BlockSpec((NSA_HPG, tq, NSA_DK), lambda b, g, i: (g, b * nq + i, 0)),
            pl.BlockSpec((None, NSA_GROWS, tq), lambda b, g, i: (g, 0, b * nq + i)),
            pl.BlockSpec((None, n_cb, NSA_DK), kv),
            pl.BlockSpec((None, n_cb, NSA_DV), kv),
            pl.BlockSpec((None, seq_len, NSA_DK), kv),
            pl.BlockSpec((None, seq_len, NSA_DV), kv),
            pl.BlockSpec((None, seq_len, NSA_DK), kv),
            pl.BlockSpec((None, seq_len, NSA_DV), kv),
        ],
        out_specs=pl.BlockSpec((tq, NSA_HPG * NSA_DV), lambda b, g, i: (b * nq + i, g)),
        out_shape=jax.ShapeDtypeStruct((m, NSA_WIDTH), BF16),
        scratch_shapes=[pltpu.VMEM((seq_len // tk, tk, tq), F32)],
        compiler_params=_cparams(("parallel", "parallel", "arbitrary")),
        name="nsa_attention",
    )(q, p_gt, k_cmp, v_cmp, ks, vs, kw, vw)


def _merge_kernel(y0_ref, y1_ref, y2_ref, g0_ref, g1_ref, g2_ref, w_ref, o_ref):
    acc = _sigmoid(g0_ref[...]) * _dg(y0_ref[...], w_ref[0], NN)
    acc = acc + _sigmoid(g1_ref[...]) * _dg(y1_ref[...], w_ref[1], NN)
    acc = acc + _sigmoid(g2_ref[...]) * _dg(y2_ref[...], w_ref[2], NN)
    o_ref[...] = acc.astype(o_ref.dtype)


def _merge(y_rw, y_gm, y_ns, p_gate, w_br):
    m, d = y_rw.shape
    tm = min(MERGE_TM, m)
    tn = MERGE_TN
    nj = d // tn
    ys = pl.BlockSpec((tm, d), lambda i, j: (i, 0))
    gspec = lambda br: pl.BlockSpec((tm, tn), lambda i, j: (i, br * nj + j))
    return pl.pallas_call(
        _merge_kernel,
        grid=(m // tm, nj),
        in_specs=[ys, ys, ys, gspec(0), gspec(1), gspec(2),
                  pl.BlockSpec((N_BRANCH, d, tn), lambda i, j: (0, 0, j))],
        out_specs=pl.BlockSpec((tm, tn), lambda i, j: (i, j)),
        out_shape=jax.ShapeDtypeStruct((m, d), BF16),
        compiler_params=_cparams(("parallel", "arbitrary")),
        name="branch_merge",
    )(y_rw, y_gm, y_ns, p_gate, p_gate, p_gate, w_br)


def _out_ln_kernel(y_ref, w_ref, x_ref, g_ref, b_ref, o_ref):
    y = ALPHA * x_ref[...] + _dg(y_ref[...], w_ref[...], NN)
    o_ref[...] = _layer_norm(y, g_ref[...], b_ref[...])


def _out_ln(merged, w_o, x, g, b):
    m, d = x.shape
    tm = min(WO_TM, m)
    row = lambda i: (i, 0)
    fix = lambda i: (0, 0)
    return pl.pallas_call(
        _out_ln_kernel,
        grid=(m // tm,),
        in_specs=[pl.BlockSpec((tm, d), row), pl.BlockSpec((d, d), fix), pl.BlockSpec((tm, d), row),
                  pl.BlockSpec((1, d), fix), pl.BlockSpec((1, d), fix)],
        out_specs=pl.BlockSpec((tm, d), row),
        out_shape=jax.ShapeDtypeStruct((m, d), F32),
        compiler_params=_cparams(("parallel",)),
        name="out_proj_ln",
    )(merged, w_o, x, g.reshape(1, d), b.reshape(1, d))


def _rwkv_branch(xb, wl, mu, w0, w2, a0, a2, g2, k_k, k_a, r_k, gn_g, gn_b, vres, batch, seq_len):
    p_rkv = _mm(xb, wl[:, :3 * RW_WIDTH].astype(BF16), F32)
    p_lora = _mm(xb, wl[:, 3 * RW_WIDTH:RW_COLS].astype(BF16), F32, tn=RW_LORA_COLS)
    if vres is None:
        return _wkv(p_rkv, p_lora, mu, w0, a0, w2, a2, g2, k_k, k_a, r_k.reshape(-1), gn_g, gn_b,
                    None, batch, seq_len)
    v_first, v0, v1, v2 = vres
    vlo = _rw_vlora(p_rkv, mu[2 * RW_WIDTH:3 * RW_WIDTH], v1, seq_len)
    y = _wkv(p_rkv, p_lora, mu, w0, a0, w2, a2, g2, k_k, k_a, r_k.reshape(-1), gn_g, gn_b,
             (v_first, v0, vlo, v2), batch, seq_len)
    return y, v_first


def _nsa_branch(xb, wl, pos_k, pos_v, phi_k1, phi_k2, phi_v1, phi_v2, batch, seq_len):
    q, kc, vc, ks, vs, kw, vw, p_g = _nsa_proj(xb, wl)
    gb = NSA_KV_GROUPS * batch
    per_seq = lambda z: z.reshape(gb, seq_len, z.shape[-1])
    k_cmp = _compress(per_seq(kc), pos_k, phi_k1, phi_k2)
    v_cmp = _compress(per_seq(vc), pos_v, phi_v1, phi_v2)
    return _nsa_attention(q, p_g, k_cmp, v_cmp, per_seq(ks), per_seq(vs), per_seq(kw), per_seq(vw),
                          batch, seq_len)


def kernel(x, w_in, rw_mu, rw_w0, rw_w2, rw_a0, rw_a2, rw_g2, rw_v0, rw_v1, rw_v2, rw_k_k, rw_k_a, rw_r_k, rw_gn_g, rw_gn_b, gm_ln_g, gm_ln_b, gm_ws, gm_bs, nsa_pos_k, nsa_pos_v, nsa_phi_k1, nsa_phi_k2, nsa_phi_v1, nsa_phi_v2, w_br, w_o, ffn1_wg, ffn1_wu, ffn1_wd, ffn2_wg, ffn2_wu, ffn2_wd, ln_g, ln_b):
    batch, seq_len, d = x.shape
    m = batch * seq_len
    h = x.reshape(m, d)
    v_first = None
    for l in range(DEPTH):
        h = _ffn(h, ffn1_wg[l].astype(BF16), ffn1_wu[l].astype(BF16), ffn1_wd[l].astype(BF16),
                 ln_g[l, 0], ln_b[l, 0])
        hb = h.astype(BF16)
        wl = w_in[l]
        vres = None if l == 0 else (v_first, rw_v0[l - 1], rw_v1[l - 1], rw_v2[l - 1])
        y_rw, v_out = _rwkv_branch(hb, wl, rw_mu[l], rw_w0[l], rw_w2[l], rw_a0[l], rw_a2[l], rw_g2[l],
                                   rw_k_k[l], rw_k_a[l], rw_r_k[l], rw_gn_g[l], rw_gn_b[l], vres,
                                   batch, seq_len)
        if l == 0:
            v_first = v_out
        p_gm = _mm(hb, wl[:, OFF_GM:OFF_NSA].astype(BF16), F32)
        y_gm = _gmlp(p_gm, gm_ln_g[l], gm_ln_b[l], gm_ws[l], gm_bs[l])
        y_ns = _nsa_branch(hb, wl, nsa_pos_k[l], nsa_pos_v[l], nsa_phi_k1[l], nsa_phi_k2[l],
                           nsa_phi_v1[l], nsa_phi_v2[l], batch, seq_len)
        p_gate = _mm(hb, wl[:, OFF_GATE:].astype(BF16), F32)
        merged = _merge(y_rw, y_gm, y_ns, p_gate, w_br[l].astype(BF16))
        h = _out_ln(merged, w_o[l].astype(BF16), h, ln_g[l, 1], ln_b[l, 1])
        h = _ffn(h, ffn2_wg[l].astype(BF16), ffn2_wu[l].astype(BF16), ffn2_wd[l].astype(BF16),
                 ln_g[l, 2], ln_b[l, 2])
    return h.reshape(batch, seq_len, d)
```

```python
import functools

import jax
import jax.numpy as jnp
from jax import lax
from jax.experimental import pallas as pl
from jax.experimental.pallas import tpu as pltpu

F32 = jnp.float32
BF16 = jnp.bfloat16

D_MODEL = 2048
DEPTH = 2
RW_HEAD = 64
RW_WIDTH = D_MODEL
RW_DECAY_LORA = 96
RW_AAA_LORA = 96
RW_MV_LORA = 64
RW_GATE_LORA = 256
RW_LORA_COLS = RW_DECAY_LORA + RW_AAA_LORA + RW_GATE_LORA
RW_GN_EPS = RW_HEAD * 1e-5
GM_WIDTH = D_MODEL
GM_CHUNK = 128
GM_GROUP = 128
GM_GROUPS = GM_WIDTH // GM_GROUP
NSA_HEADS = 16
NSA_KV_GROUPS = 4
NSA_HPG = NSA_HEADS // NSA_KV_GROUPS
NSA_DK = 192
NSA_DV = 128
NSA_WIDTH = NSA_HEADS * NSA_DV
CMP_BLK = 32
CMP_STRIDE = 16
SEL_BLK = 64
N_SEL = 16
WIN = 512
N_BRANCH = 3
ALPHA = (2 * DEPTH) ** 0.25
LN_EPS = 1e-5
NEG = -1e30
FORCED = 1e6
LOG2_E = 1.4426950408889634

RW_COLS = 3 * RW_WIDTH + RW_LORA_COLS
GM_COLS = 2 * GM_WIDTH
NSA_Q_COLS = NSA_HEADS * NSA_DK
NSA_GK = NSA_KV_GROUPS * NSA_DK
NSA_GV = NSA_KV_GROUPS * NSA_DV
NSA_KV_COLS = 3 * (NSA_GK + NSA_GV)
NSA_G_COLS = 3 * NSA_HEADS
NSA_COLS = NSA_Q_COLS + NSA_KV_COLS + NSA_G_COLS
OFF_GM = RW_COLS
OFF_NSA = OFF_GM + GM_COLS
OFF_GATE = OFF_NSA + NSA_COLS

V7X_VMEM_BYTES = 64 * 1024 * 1024
VMEM_LIMIT = V7X_VMEM_BYTES * 7 // 8

FFN_TM = 512
FFN_TF = 512
MM_TM = 1024
MM_TN = 512
MERGE_TM = 512
MERGE_TN = 512
WO_TM = 512
PREP_TM = 128
WKV_L = 64
WKV_TC = 1024
WKV_PAIRS = 2
WKV_GROUPS = 2
GM_TM = 256
NSA_TQ = 256
NSA_TK = 1024
NSA_GROWS = 16

NN = (((1,), (0,)), ((), ()))
NT = (((1,), (1,)), ((), ()))
TN = (((0,), (0,)), ((), ()))


def _cparams(sem):
    return pltpu.CompilerParams(dimension_semantics=sem, vmem_limit_bytes=VMEM_LIMIT)


def _dg(a, b, dims):
    return lax.dot_general(a, b, dims, preferred_element_type=F32)


def _split2(x):
    hi = x.astype(BF16)
    lo = (x - hi.astype(F32)).astype(BF16)
    return hi, lo


def _split3(x):
    x1 = x.astype(BF16)
    r1 = x - x1.astype(F32)
    x2 = r1.astype(BF16)
    x3 = (r1 - x2.astype(F32)).astype(BF16)
    return x1, x2, x3


def _dot3(a, b, dims=NN):
    ah, al = _split2(a)
    bh, bl = _split2(b)
    return _dg(ah, bh, dims) + (_dg(ah, bl, dims) + _dg(al, bh, dims))


def _dot_exact_lhs(m01, b, dims=NN):
    b1, b2, b3 = _split3(b)
    return _dg(m01, b1, dims) + (_dg(m01, b2, dims) + _dg(m01, b3, dims))


def _layer_norm(y, g, b):
    mu = jnp.mean(y, axis=-1, keepdims=True)
    d = y - mu
    var = jnp.mean(d * d, axis=-1, keepdims=True)
    return d * lax.rsqrt(var + LN_EPS) * g + b


def _gelu_tanh(x):
    return 0.5 * x * (1.0 + jnp.tanh(0.7978845608028654 * (x + 0.044715 * (x * x * x))))


def _sigmoid(x):
    return 1.0 / (1.0 + jnp.exp(-x))


def _ffn_kernel(x_ref, wg_ref, wu_ref, wd_ref, g_ref, b_ref, o_ref, xb_ref):
    j = pl.program_id(1)

    @pl.when(j == 0)
    def _():
        o_ref[...] = jnp.zeros_like(o_ref)
        xb_ref[...] = x_ref[...].astype(BF16)

    xb = xb_ref[...]
    hg = _dg(xb, wg_ref[...], NN)
    hu = _dg(xb, wu_ref[...], NN)
    h = (hg * _sigmoid(hg) * hu).astype(BF16)
    o_ref[...] += _dg(h, wd_ref[...], NN)

    @pl.when(j == pl.num_programs(1) - 1)
    def _():
        y = ALPHA * x_ref[...] + 0.5 * o_ref[...]
        o_ref[...] = _layer_norm(y, g_ref[...], b_ref[...])


def _ffn(x, wg, wu, wd, g, b):
    m, d = x.shape
    ff = wg.shape[1]
    tm = min(FFN_TM, m)
    return pl.pallas_call(
        _ffn_kernel,
        grid=(m // tm, ff // FFN_TF),
        in_specs=[
            pl.BlockSpec((tm, d), lambda i, j: (i, 0)),
            pl.BlockSpec((d, FFN_TF), lambda i, j: (0, j)),
            pl.BlockSpec((d, FFN_TF), lambda i, j: (0, j)),
            pl.BlockSpec((FFN_TF, d), lambda i, j: (j, 0)),
            pl.BlockSpec((1, d), lambda i, j: (0, 0)),
            pl.BlockSpec((1, d), lambda i, j: (0, 0)),
        ],
        out_specs=pl.BlockSpec((tm, d), lambda i, j: (i, 0)),
        out_shape=jax.ShapeDtypeStruct((m, d), F32),
        scratch_shapes=[pltpu.VMEM((tm, d), BF16)],
        compiler_params=_cparams(("parallel", "arbitrary")),
        name="ffn_swiglu_ln",
    )(x, wg, wu, wd, g.reshape(1, d), b.reshape(1, d))


def _mm_kernel(x_ref, w_ref, o_ref):
    o_ref[...] = _dg(x_ref[...], w_ref[...], NN).astype(o_ref.dtype)


def _mm(x, w, out_dtype, tn=MM_TN):
    m, k = x.shape
    n = w.shape[1]
    tm = min(MM_TM, m)
    tn = min(tn, n)
    return pl.pallas_call(
        _mm_kernel,
        grid=(m // tm, n // tn),
        in_specs=[
            pl.BlockSpec((tm, k), lambda i, j: (i, 0)),
            pl.BlockSpec((k, tn), lambda i, j: (0, j)),
        ],
        out_specs=pl.BlockSpec((tm, tn), lambda i, j: (i, j)),
        out_shape=jax.ShapeDtypeStruct((m, n), out_dtype),
        compiler_params=_cparams(("parallel", "arbitrary")),
        name="proj_mm",
    )(x, w)


def _token_shift(x, prev_row, mu, first):
    rows = lax.broadcasted_iota(jnp.int32, x.shape, 0)
    prev_row = jnp.where(first, jnp.zeros_like(prev_row), prev_row)
    shifted = jnp.where(rows == 0, prev_row, pltpu.roll(x, 1, 0))
    return x + (shifted - x) * mu


def _rw_vlora_kernel(seq_len, p_ref, pp_ref, mu_ref, v1_ref, o_ref):
    first = (pl.program_id(0) * p_ref.shape[0]) % seq_len == 0
    v = _token_shift(p_ref[...], pp_ref[7:8, :], mu_ref[...], first)
    o_ref[...] = _dot3(v, v1_ref[...])


def _rw_vlora(p_rkv, mu_v, v1, seq_len):
    m = p_rkv.shape[0]
    tm = PREP_TM
    wd = RW_WIDTH
    return pl.pallas_call(
        functools.partial(_rw_vlora_kernel, seq_len),
        grid=(m // tm,),
        in_specs=[pl.BlockSpec((tm, wd), lambda i: (i, 2)),
                  pl.BlockSpec((8, wd), lambda i: (jnp.maximum(i * (tm // 8) - 1, 0), 2)),
                  pl.BlockSpec((1, wd), lambda i: (0, 0)),
                  pl.BlockSpec((wd, RW_MV_LORA), lambda i: (0, 0))],
        out_specs=pl.BlockSpec((tm, RW_MV_LORA), lambda i: (i, 0)),
        out_shape=jax.ShapeDtypeStruct((m, RW_MV_LORA), F32),
        compiler_params=_cparams(("parallel",)),
        name="rwkv_vlora",
    )(p_rkv, p_rkv, mu_v.reshape(1, wd), v1)


def _sp(x):
    return _split2(x)


def _d3s(a, b, dims=NN):
    return _dg(a[0], b[0], dims) + (_dg(a[0], b[1], dims) + _dg(a[1], b[0], dims))


def _wkv_kernel(has_vres, *refs):
    (pr_ref, pk_ref, pv_ref, ppr_ref, ppk_ref, ppv_ref, lo_ref, plo_ref,
     mur_ref, muk_ref, muv_ref, mul_ref, w0_ref, a0_ref, w2_ref, a2_ref, g2_ref,
     kk_ref, ka_ref, rk_ref, gg_ref, gb_ref) = refs[:22]
    if has_vres:
        vf_ref, v0_ref, vlo_ref, v2_ref, o_ref, s_ref = refs[22:]
    else:
        o_ref, vout_ref, s_ref = refs[22:]
    L = WKV_L
    n_groups = WKV_GROUPS
    rg = pr_ref.shape[0] // n_groups
    n_chunks = rg // L

    @pl.when(pl.program_id(2) == 0)
    def _():
        s_ref[...] = jnp.zeros_like(s_ref)

    ri = lax.broadcasted_iota(jnp.int32, (2 * L, 2 * L), 0)
    ci = lax.broadcasted_iota(jnp.int32, (2 * L, 2 * L), 1)
    bd_f = ((ri // L) == (ci // L)).astype(F32)
    bd_b = bd_f.astype(BF16)
    rl = lax.broadcasted_iota(jnp.int32, (L, L), 0)
    cl = lax.broadcasted_iota(jnp.int32, (L, L), 1)
    tri_b = (cl <= rl).astype(BF16)
    rt_i = lax.broadcasted_iota(jnp.int32, (L, 2 * L), 0)
    cs_i = lax.broadcasted_iota(jnp.int32, (L, 2 * L), 1) % L
    strict = cs_i < rt_i
    incl = cs_i <= rt_i

    bd2_b = jnp.concatenate([bd_b, bd_b], axis=1)
    eye_f = (ri == ci).astype(F32)

    def sm(zb, mask=bd_b):
        return jnp.concatenate([zb, zb], axis=0) * mask

    def b16(x):
        return x.astype(BF16)

    def hsum(x):
        hi, lo = _sp(x)
        s2 = _dg(jnp.concatenate([hi, lo], axis=0), bd_b, NN)
        return s2[:L] + s2[L:]

    def cumsum_rows(x):
        x1, x2, x3 = _split3(x)
        s3 = _dg(tri_b, jnp.concatenate([x1, x2, x3], axis=1), NN)
        w = x.shape[1]
        return s3[:, :w] + (s3[:, w:2 * w] + s3[:, 2 * w:])

    inv_n = 1.0 / RW_HEAD
    n_lvl = L.bit_length() - 1
    n_pairs = pr_ref.shape[1] // (2 * L)

    C = range(n_chunks * n_pairs)
    rows = [slice((c // n_pairs) * L, (c // n_pairs + 1) * L) for c in C]
    lanes = [slice((c % n_pairs) * 2 * L, (c % n_pairs + 1) * 2 * L) for c in C]
    fronts = {}

    def front(g):
        rs = slice(g * rg, (g + 1) * rg)
        if g == 0:
            first = pl.program_id(2) == 0
            prev = lambda p_ref, pp_ref: pp_ref[7:8, :]
        else:
            first = False
            prev = lambda p_ref, pp_ref: p_ref[g * rg - 1:g * rg, :]
        r_all = _token_shift(pr_ref[rs, :], prev(pr_ref, ppr_ref), mur_ref[...], first)
        yield
        k_all = _token_shift(pk_ref[rs, :], prev(pk_ref, ppk_ref), muk_ref[...], first)
        yield
        v_all = _token_shift(pv_ref[rs, :], prev(pv_ref, ppv_ref), muv_ref[...], first)
        yield
        lo = _token_shift(lo_ref[rs, :], prev(lo_ref, plo_ref), mul_ref[...], first)
        wl = lo[:, :RW_DECAY_LORA]
        al = lo[:, RW_DECAY_LORA:RW_DECAY_LORA + RW_AAA_LORA]
        gl = lo[:, RW_DECAY_LORA + RW_AAA_LORA:]
        yield
        z = w0_ref[...] + _dot3(jnp.tanh(wl), w2_ref[...])
        yield
        w_all = -(jnp.maximum(-z, 0.0) + jnp.log1p(jnp.exp(-jnp.abs(z)))) - 0.5
        yield
        a_all = _sigmoid(a0_ref[...] + _dot3(al, a2_ref[...]))
        yield
        g_all = _dot3(_sigmoid(gl), g2_ref[...])
        yield
        if has_vres:
            mix = _sigmoid(v0_ref[...] + _dot3(vlo_ref[rs, :], v2_ref[...]))
            v_all = v_all + (vf_ref[rs, :] - v_all) * mix
        else:
            vout_ref[rs, :] = v_all
        yield
        k_k = [kk_ref[:, lanes[c]] for c in C]
        k_a = [ka_ref[:, lanes[c]] for c in C]
        r_k = [rk_ref[:, lanes[c]] for c in C]
        r = [r_all[rows[c], lanes[c]] for c in C]
        k_raw = [k_all[rows[c], lanes[c]] for c in C]
        v = [v_all[rows[c], lanes[c]] for c in C]
        a = [a_all[rows[c], lanes[c]] for c in C]
        kk = [k_raw[c] * k_k[c] for c in C]
        ssq = [hsum(kk[c] * kk[c]) for c in C]
        yield
        lw = [-jnp.exp(w_all[rows[c], lanes[c]]) for c in C]
        cum = [cumsum_rows(lw[c]) for c in C]
        yield
        kk = [kk[c] * lax.rsqrt(jnp.maximum(ssq[c], 1e-24)) for c in C]
        k = [k_raw[c] * (1.0 + (a[c] - 1.0) * k_a[c]) for c in C]
        bv = [kk[c] * a[c] for c in C]
        yield
        tot = [cum[c][L - 1:L, :] for c in C]
        e_neg = [jnp.exp(-cum[c]) for c in C]
        rt = [r[c] * jnp.exp(cum[c]) for c in C]
        yield
        at = [-kk[c] * jnp.exp(cum[c] - lw[c]) for c in C]
        e_tot = [jnp.exp(tot[c] - cum[c]) for c in C]
        yield
        bh = [b16(bv[c] * e_tot[c]) for c in C]
        kh = [b16(k[c] * e_tot[c]) for c in C]
        vb = [b16(v[c]) for c in C]
        smv = [sm(vb[c]) for c in C]
        yield
        lhs2 = [b16(jnp.concatenate([at[c], rt[c]], axis=0)) for c in C]
        smkb = [jnp.concatenate([sm(b16(k[c] * e_neg[c])), sm(b16(bv[c] * e_neg[c]))], axis=0) for c in C]
        yield
        bonus = [hsum(r[c] * k[c] * r_k[c]) * v[c] for c in C]
        g_out = [g_all[rows[c], lanes[c]] for c in C]
        fronts[g] = (tot, rt, at, bh, kh, vb, smv, lhs2, smkb, bonus, g_out)

    def drain(gen):
        for _ in gen:
            pass

    def run_group(g, state, pump):
        tot, rt, at, bh, kh, vb, smv, lhs2, smkb, bonus, g_out = fronts.pop(g)
        akb = [_dg(lhs2[c], smkb[c], NT) for c in C]
        pump()
        ak = [akb[c][:, :2 * L] for c in C]
        ab = [akb[c][:, 2 * L:] for c in C]
        a_ak = [b16(jnp.where(strict, ak[c][:L], 0.0)) for c in C]
        a_rk = [b16(jnp.where(incl, ak[c][L:], 0.0)) for c in C]
        p = [jnp.where(strict, ab[c][:L], 0.0) for c in C]
        a_rb = [b16(jnp.where(incl, ab[c][L:], 0.0)) for c in C]
        akv = [_dg(a_ak[c], smv[c], NN) for c in C]
        pump()
        z = [jnp.concatenate([at[c], akv[c]], axis=1) for c in C]
        for lvl in range(n_lvl):
            pb = [b16(p[c]) for c in C]
            smz = [sm(b16(z[c]), bd2_b) for c in C]
            z = [z[c] + _dg(pb[c], smz[c], NN) for c in C]
            pump()
            if lvl + 1 < n_lvl:
                smp = [sm(pb[c]) for c in C]
                p = [_dg(pb[c], smp[c], NN) for c in C]
                pump()
        zb = [b16(z[c]) for c in C]
        qy = [_dg(a_rb[c], sm(zb[c], bd2_b), NN) for c in C]
        pump()
        q = [b16(rt[c] + qy[c][:, :2 * L]) for c in C]
        y0 = [_dg(a_rk[c], smv[c], NN) + qy[c][:, 2 * L:] for c in C]
        pump()
        zbh = [_dg(zb[c], bh[c], TN) for c in C]
        gam_s = [_sp(bd_f * zbh[c][:2 * L] + eye_f * jnp.exp(tot[c])) for c in C]
        cc = [bd_f * (_dg(vb[c], kh[c], TN) + zbh[c][2 * L:]) for c in C]
        pump()

        y = []
        for c in C:
            pi = c % n_pairs
            st_s = _sp(state[pi])
            y.append(_dg(q[c], st_s[0], NT) + y0[c])
            state[pi] = _d3s(st_s, gam_s[c]) + cc[c]
        mean = [hsum(y[c]) * inv_n for c in C]
        d = [y[c] - mean[c] for c in C]
        var = [hsum(d[c] * d[c]) * inv_n for c in C]
        for c in C:
            yn = d[c] * lax.rsqrt(var[c] + RW_GN_EPS) * gg_ref[:, lanes[c]] + gb_ref[:, lanes[c]]
            o_ref[pl.ds(g * rg + rows[c].start, L), lanes[c]] = ((yn + bonus[c]) * g_out[c]).astype(o_ref.dtype)
        return state

    state = [s_ref[pi] for pi in range(n_pairs)]
    drain(front(0))
    for g in range(n_groups):
        nxt = front(g + 1) if g + 1 < n_groups else iter(())
        state = run_group(g, state, lambda: next(nxt, None))
        drain(nxt)
    for pi in range(n_pairs):
        s_ref[pi] = state[pi]


def _wkv(p_rkv, p_lora, mu, w0, a0, w2, a2, g2, k_k, k_a, r_k, gn_g, gn_b, vres, batch, seq_len):
    m = p_rkv.shape[0]
    wd = RW_WIDTH
    tc = min(WKV_TC, seq_len)
    nt = seq_len // tc
    pair = 2 * RW_HEAD
    lanes = WKV_PAIRS * pair
    nb = wd // lanes
    row = lambda b, h, t: b * nt + t
    prv = lambda b, h, t: jnp.maximum((b * nt + t) * (tc // 8) - 1, 0)
    act = lambda sec: pl.BlockSpec((tc, lanes), lambda b, h, t: (row(b, h, t), sec * nb + h))
    prev = lambda sec: pl.BlockSpec((8, lanes), lambda b, h, t: (prv(b, h, t), sec * nb + h))
    par = lambda sec=0: pl.BlockSpec((1, lanes), lambda b, h, t: (0, sec * nb + h))
    mat = lambda k: pl.BlockSpec((k, lanes), lambda b, h, t: (0, h))
    lo_w = RW_LORA_COLS
    c3 = 3 * wd
    in_specs = [act(0), act(1), act(2), prev(0), prev(1), prev(2),
                pl.BlockSpec((tc, lo_w), lambda b, h, t: (row(b, h, t), 0)),
                pl.BlockSpec((8, lo_w), lambda b, h, t: (prv(b, h, t), 0)),
                par(0), par(1), par(2), pl.BlockSpec((1, lo_w), lambda b, h, t: (0, 0)),
                par(), par(), mat(RW_DECAY_LORA), mat(RW_AAA_LORA), mat(RW_GATE_LORA)] + [par()] * 5
    mu_rkv = mu[:c3].reshape(1, c3)
    args = [p_rkv, p_rkv, p_rkv, p_rkv, p_rkv, p_rkv, p_lora, p_lora,
            mu_rkv, mu_rkv, mu_rkv, mu[c3:].reshape(1, lo_w),
            w0.reshape(1, wd), a0.reshape(1, wd), w2, a2, g2]
    args += [z.reshape(1, wd) for z in (k_k, k_a, r_k, gn_g, gn_b)]
    out_act = pl.BlockSpec((tc, lanes), lambda b, h, t: (row(b, h, t), h))
    if vres is not None:
        v_first, v0, vlo, v2 = vres
        in_specs += [out_act, par(), pl.BlockSpec((tc, RW_MV_LORA), lambda b, h, t: (row(b, h, t), 0)),
                     mat(RW_MV_LORA)]
        args += [v_first, v0.reshape(1, wd), vlo, v2]
        out_specs = out_act
        out_shape = jax.ShapeDtypeStruct((m, wd), BF16)
    else:
        out_specs = [out_act, out_act]
        out_shape = [jax.ShapeDtypeStruct((m, wd), BF16), jax.ShapeDtypeStruct((m, wd), F32)]
    return pl.pallas_call(
        functools.partial(_wkv_kernel, vres is not None),
        grid=(batch, nb, nt),
        in_specs=in_specs,
        out_specs=out_specs,
        out_shape=out_shape,
        scratch_shapes=[pltpu.VMEM((WKV_PAIRS, pair, pair), F32)],
        compiler_params=_cparams(("parallel", "parallel", "arbitrary")),
        name="rwkv_wkv",
    )(*args)


def _gmlp_kernel(p_ref, lg_ref, lb_ref, ws_ref, bst_ref, o_ref):
    tm = p_ref.shape[0]
    u = _gelu_tanh(p_ref[:, :GM_WIDTH])
    v = _layer_norm(_gelu_tanh(p_ref[:, GM_WIDTH:]), lg_ref[...], lb_ref[...])
    rl = lax.broadcasted_iota(jnp.int32, (GM_CHUNK, GM_CHUNK), 0)
    cl = lax.broadcasted_iota(jnp.int32, (GM_CHUNK, GM_CHUNK), 1)
    causal = cl <= rl
    for g in range(GM_GROUPS):
        cols = slice(g * GM_GROUP, (g + 1) * GM_GROUP)
        wsg = jnp.where(causal, ws_ref[g], 0.0)
        bias = bst_ref[:, g:g + 1]
        for c in range(tm // GM_CHUNK):
            rows = slice(c * GM_CHUNK, (c + 1) * GM_CHUNK)
            s = _dot3(wsg, v[rows, cols]) + bias
            o_ref[rows, cols] = (u[rows, cols] * s).astype(o_ref.dtype)


def _gmlp(p_gm, ln_g, ln_b, ws, bs):
    m = p_gm.shape[0]
    tm = GM_TM
    fix2 = lambda i: (0, 0)
    return pl.pallas_call(
        _gmlp_kernel,
        grid=(m // tm,),
        in_specs=[
            pl.BlockSpec((tm, GM_COLS), lambda i: (i, 0)),
            pl.BlockSpec((1, GM_WIDTH), fix2),
            pl.BlockSpec((1, GM_WIDTH), fix2),
            pl.BlockSpec((GM_GROUPS, GM_CHUNK, GM_CHUNK), lambda i: (0, 0, 0)),
            pl.BlockSpec((GM_CHUNK, GM_GROUPS), fix2),
        ],
        out_specs=pl.BlockSpec((tm, GM_WIDTH), lambda i: (i, 0)),
        out_shape=jax.ShapeDtypeStruct((m, GM_WIDTH), BF16),
        compiler_params=_cparams(("parallel",)),
        name="gmlp_mix",
    )(p_gm, ln_g.reshape(1, -1), ln_b.reshape(1, -1), ws, bs.T)


def _compress_kernel(c_ref, pos_ref, w1_ref, w2_ref, o_ref):
    c = c_ref[...]
    n = c.shape[0]
    r1 = _dot3(c, w1_ref[0])
    r2 = _dot3(c, w1_ref[1])
    pos = jnp.broadcast_to(pos_ref[...], (8, pos_ref.shape[1]))
    half = w1_ref.shape[1]
    pterm = (_dot3(pos[:, :half], w1_ref[0]) + _dot3(pos[:, half:], w1_ref[1]))[0:1, :]
    h = r1 + pltpu.roll(r2, n - 1, 0) + pterm
    o_ref[...] = _dot3(_gelu_tanh(h), w2_ref[...])


def _compress(z, pos, w1, w2):
    gb, t, d = z.shape
    nb = t // CMP_STRIDE
    c = z.reshape(gb, nb, CMP_STRIDE * d)
    half = CMP_STRIDE * d
    return pl.pallas_call(
        _compress_kernel,
        grid=(gb,),
        in_specs=[
            pl.BlockSpec((None, nb, half), lambda i: (i, 0, 0)),
            pl.BlockSpec((1, 2 * half), lambda i: (0, 0)),
            pl.BlockSpec((2, half, d), lambda i: (0, 0, 0)),
            pl.BlockSpec((d, d), lambda i: (0, 0)),
        ],
        out_specs=pl.BlockSpec((None, nb, d), lambda i: (i, 0, 0)),
        out_shape=jax.ShapeDtypeStruct((gb, nb, d), F32),
        compiler_params=_cparams(("parallel",)),
        name="nsa_compress",
    )(c, pos.reshape(1, 2 * half), w1.reshape(2, half, d), w2)


def _nsa_proj_kernel(x_ref, wq_ref, wkv_ref, wgt_ref,
                     q_ref, kc_ref, vc_ref, ks_ref, vs_ref, kw_ref, vw_ref, gt_ref):
    x = x_ref[...]
    scale = NSA_DK ** -0.5 * LOG2_E
    rq = _dg(x, wq_ref[...], NN)
    for h in range(NSA_HPG):
        q_ref[h] = (rq[:, h * NSA_DK:(h + 1) * NSA_DK] * scale).astype(q_ref.dtype)
    rkv = _dg(x, wkv_ref[...], NN)
    o = 0
    for o_ref, d in ((vc_ref, NSA_DV), (vs_ref, NSA_DV), (vw_ref, NSA_DV),
                     (kc_ref, NSA_DK), (ks_ref, NSA_DK), (kw_ref, NSA_DK)):
        o_ref[...] = rkv[:, o:o + d].astype(o_ref.dtype)
        o += d
    gt_ref[...] = _dg(wgt_ref[...], x, NT)


def _nsa_proj(xb, wl):
    m, kdim = xb.shape
    G, hp = NSA_KV_GROUPS, NSA_HPG
    tm = min(MM_TM, m)
    o = OFF_NSA

    def take(width, d):
        nonlocal o
        w3 = wl[:, o:o + width].astype(BF16).reshape(kdim, width // d, d)
        o += width
        return w3

    wq = take(NSA_Q_COLS, hp * NSA_DK).reshape(kdim, NSA_Q_COLS)
    kc, vc, ks, vs, kw, vw = (take(NSA_GK, NSA_DK), take(NSA_GV, NSA_DV), take(NSA_GK, NSA_DK),
                              take(NSA_GV, NSA_DV), take(NSA_GK, NSA_DK), take(NSA_GV, NSA_DV))
    wkv = jnp.concatenate([vc, vs, vw, kc, ks, kw], axis=2).transpose(1, 0, 2)
    nkv = wkv.shape[2]
    wgt = take(NSA_G_COLS, 3 * hp).transpose(1, 2, 0)
    wgt = jnp.pad(wgt, ((0, 0), (0, NSA_GROWS - 3 * hp), (0, 0)))
    ospec = lambda d: pl.BlockSpec((None, tm, d), lambda i, g: (g, i, 0))
    oshape = lambda d, dt: jax.ShapeDtypeStruct((G, m, d), dt)
    return pl.pallas_call(
        _nsa_proj_kernel,
        grid=(m // tm, G),
        in_specs=[pl.BlockSpec((tm, kdim), lambda i, g: (i, 0)),
                  pl.BlockSpec((kdim, hp * NSA_DK), lambda i, g: (0, g)),
                  pl.BlockSpec((None, kdim, nkv), lambda i, g: (g, 0, 0)),
                  pl.BlockSpec((None, NSA_GROWS, kdim), lambda i, g: (g, 0, 0))],
        out_specs=[pl.BlockSpec((hp, tm, NSA_DK), lambda i, g: (g, i, 0)),
                   ospec(NSA_DK), ospec(NSA_DV), ospec(NSA_DK), ospec(NSA_DV), ospec(NSA_DK), ospec(NSA_DV),
                   pl.BlockSpec((None, NSA_GROWS, tm), lambda i, g: (g, 0, i))],
        out_shape=[jax.ShapeDtypeStruct((G * hp, m, NSA_DK), BF16),
                   oshape(NSA_DK, F32), oshape(NSA_DV, F32), oshape(NSA_DK, BF16), oshape(NSA_DV, BF16),
                   oshape(NSA_DK, BF16), oshape(NSA_DV, BF16),
                   jax.ShapeDtypeStruct((G, NSA_GROWS, m), F32)],
        compiler_params=_cparams(("parallel", "arbitrary")),
        name="nsa_proj",
    )(xb, wq, wkv, wgt)


def _nsa_kernel(seq_len, q_ref, gt_ref, kc_ref, vc_ref, ks_ref, vs_ref, kw_ref, vw_ref,
                o_ref, bias_ref):
    tq = NSA_TQ
    hp = NSA_HPG
    nr = hp * tq
    tk = min(NSA_TK, seq_len)
    wk = WIN + tq
    n_s = seq_len // SEL_BLK
    k_sel = min(N_SEL, n_s)
    n_cb = kc_ref.shape[0]
    i = pl.program_id(2)
    t0 = i * tq

    qa = q_ref[...].reshape(nr, NSA_DK)
    t_lane = t0 + lax.broadcasted_iota(jnp.int32, (1, tq), 1)
    t_all = t0 + (lax.broadcasted_iota(jnp.int32, (1, nr), 1) & (tq - 1))
    tile_heads = lambda z: jnp.concatenate([z] * hp, axis=1)

    w0 = pl.multiple_of(jnp.maximum(t0 - WIN, 0), tq)
    kw_tile = kw_ref[pl.ds(w0, wk), :]
    vw_tile = vw_ref[pl.ds(w0, wk), :]
    s_w = _dg(kw_tile, qa, NT)
    s_c = _dg(kc_ref[...].astype(BF16), qa, NT)

    n_end = lax.broadcasted_iota(jnp.int32, (n_cb, 1), 0) * CMP_STRIDE + (CMP_BLK - 1)
    m_c = (n_end <= t_all) & (n_end < seq_len)
    s_c = jnp.where(m_c, s_c, NEG)
    e_c = jnp.where(m_c, jnp.exp2(s_c - jnp.max(s_c, axis=0, keepdims=True)), 0.0)
    p_c = e_c * (1.0 / jnp.maximum(jnp.sum(e_c, axis=0, keepdims=True), 1e-30))
    o_c = _dg(vc_ref[...].astype(BF16), p_c.astype(BF16), TN)

    p_sum = p_c[:, 0:tq]
    for h in range(1, hp):
        p_sum = p_sum + p_c[:, h * tq:(h + 1) * tq]
    ss = lax.broadcasted_iota(jnp.int32, (n_s, n_cb), 0) * SEL_BLK
    cs = lax.broadcasted_iota(jnp.int32, (n_s, n_cb), 1) * CMP_STRIDE
    overlap_t = ((cs < ss + SEL_BLK) & (cs + (CMP_BLK - 1) >= ss)
                 & (cs + (CMP_BLK - 1) < seq_len)).astype(BF16)
    imp = _dot_exact_lhs(overlap_t, p_sum, NN)

    kpos_w = w0 + lax.broadcasted_iota(jnp.int32, (wk, 1), 0)
    bias_w = jnp.where((kpos_w <= t_lane) & (kpos_w > t_lane - WIN), 0.0, NEG)
    s_w = s_w + tile_heads(bias_w)
    p_w = jnp.exp2(s_w - jnp.max(s_w, axis=0, keepdims=True))
    l_w = jnp.sum(p_w, axis=0, keepdims=True)
    o_w = _dg(vw_tile, p_w.astype(BF16), TN) * (1.0 / jnp.maximum(l_w, 1e-30))

    blk = lax.broadcasted_iota(jnp.int32, (n_s, 1), 0)
    cur = t_lane // SEL_BLK
    valid = blk * SEL_BLK <= t_lane
    forced = valid & ((blk == 0) | (blk == cur) | (blk == cur - 1))
    score = jnp.where(forced, FORCED, jnp.where(valid, imp, NEG))
    rank = jnp.zeros((n_s, tq), jnp.int32)
    for s in range(n_s):
        row = score[s:s + 1, :]
        beats = (row > score) | ((row == score) & (blk > s))
        rank = rank + beats.astype(jnp.int32)
    sel_t = ((rank < k_sel) & (score > 0.5 * NEG)).astype(BF16)
    n_kt = (t0 + tq + tk - 1) // tk
    n_all = seq_len // tk
    half = max(n_all // 2, 1)

    def expand(kt0, kt1):
        width = (kt1 - kt0) * tk
        kpos = kt0 * tk + lax.broadcasted_iota(jnp.int32, (width, 1), 0)
        er = (lax.broadcasted_iota(jnp.int32, (width, n_s), 0) + kt0 * tk) // SEL_BLK
        ec = lax.broadcasted_iota(jnp.int32, (width, n_s), 1)
        chosen = _dg((er == ec).astype(BF16), sel_t, NN)
        bias = jnp.where((chosen > 0.5) & (kpos <= t_lane), 0.0, NEG)
        for kt in range(kt0, kt1):
            bias_ref[kt] = bias[(kt - kt0) * tk:(kt - kt0 + 1) * tk]

    expand(0, half)
    if n_all > half:
        @pl.when(n_kt > half)
        def _():
            expand(half, n_all)

    def sel_body(kt, carry):
        m_run, l_run, acc = carry
        k0 = pl.multiple_of(kt * tk, tk)
        s = _dg(ks_ref[pl.ds(k0, tk), :], qa, NT) + tile_heads(bias_ref[kt])
        m_new = jnp.maximum(m_run, jnp.max(s, axis=0, keepdims=True))
        alpha = jnp.exp2(m_run - m_new)
        p = jnp.exp2(s - m_new)
        l_new = alpha * l_run + jnp.sum(p, axis=0, keepdims=True)
        acc = alpha * acc + _dg(vs_ref[pl.ds(k0, tk), :], p.astype(BF16), TN)
        return m_new, l_new, acc

    init = (jnp.full((1, nr), NEG, F32), jnp.zeros((1, nr), F32), jnp.zeros((NSA_DV, nr), F32))
    _, l_s, acc_s = lax.fori_loop(0, n_kt, sel_body, init)
    o_s = acc_s * (1.0 / jnp.maximum(l_s, 1e-30))

    gates = _sigmoid(gt_ref[...])
    for h in range(hp):
        lanes = slice(h * tq, (h + 1) * tq)
        out_t = (gates[3 * h:3 * h + 1] * o_c[:, lanes] + gates[3 * h + 1:3 * h + 2] * o_s[:, lanes]
                 + gates[3 * h + 2:3 * h + 3] * o_w[:, lanes])
        o_ref[:, h * NSA_DV:(h + 1) * NSA_DV] = out_t.T.astype(o_ref.dtype)


def _nsa_attention(q, p_gt, k_cmp, v_cmp, ks, vs, kw, vw, batch, seq_len):
    m = q.shape[1]
    tq = NSA_TQ
    assert seq_len >= WIN + tq and seq_len % NSA_TK == 0
    nq = seq_len // tq
    n_cb = k_cmp.shape[1]
    tk = min(NSA_TK, seq_len)
    kv = lambda b, g, i: (g * batch + b, 0, 0)
    return pl.pallas_call(
        functools.partial(_nsa_kernel, seq_len),
        grid=(batch, NSA_KV_GROUPS, nq),
        in_specs=[
            pl.BlockSpec((NSA_HPG, tq, NSA_DK), lambda b, g, i: (g, b * nq + i, 0)),
            pl.BlockSpec((None, NSA_GROWS, tq), lambda b, g, i: (g, 0, b * nq + i)),
            pl.BlockSpec((None, n_cb, NSA_DK), kv),
            pl.BlockSpec((None, n_cb, NSA_DV), kv),
            pl.BlockSpec((None, seq_len, NSA_DK), kv),
            pl.BlockSpec((None, seq_len, NSA_DV), kv),
            pl.BlockSpec((None, seq_len, NSA_DK), kv),
            pl.BlockSpec((None, seq_len, NSA_DV), kv),
        ],
        out_specs=pl.BlockSpec((tq, NSA_HPG * NSA_DV), lambda b, g, i: (b * nq + i, g)),
        out_shape=jax.ShapeDtypeStruct((m, NSA_WIDTH), BF16),
        scratch_shapes=[pltpu.VMEM((seq_len // tk, tk, tq), F32)],
        compiler_params=_cparams(("parallel", "parallel", "arbitrary")),
        name="nsa_attention",
    )(q, p_gt, k_cmp, v_cmp, ks, vs, kw, vw)


def _merge_kernel(y0_ref, y1_ref, y2_ref, g0_ref, g1_ref, g2_ref, w_ref, o_ref):
    acc = _sigmoid(g0_ref[...]) * _dg(y0_ref[...], w_ref[0], NN)
    acc = acc + _sigmoid(g1_ref[...]) * _dg(y1_ref[...], w_ref[1], NN)
    acc = acc + _sigmoid(g2_ref[...]) * _dg(y2_ref[...], w_ref[2], NN)
    o_ref[...] = acc.astype(o_ref.dtype)


def _merge(y_rw, y_gm, y_ns, p_gate, w_br):
    m, d = y_rw.shape
    tm = min(MERGE_TM, m)
    tn = MERGE_TN
    nj = d // tn
    ys = pl.BlockSpec((tm, d), lambda i, j: (i, 0))
    gspec = lambda br: pl.BlockSpec((tm, tn), lambda i, j: (i, br * nj + j))
    return pl.pallas_call(
        _merge_kernel,
        grid=(m // tm, nj),
        in_specs=[ys, ys, ys, gspec(0), gspec(1), gspec(2),
                  pl.BlockSpec((N_BRANCH, d, tn), lambda i, j: (0, 0, j))],
        out_specs=pl.BlockSpec((tm, tn), lambda i, j: (i, j)),
        out_shape=jax.ShapeDtypeStruct((m, d), BF16),
        compiler_params=_cparams(("parallel", "arbitrary")),
        name="branch_merge",
    )(y_rw, y_gm, y_ns, p_gate, p_gate, p_gate, w_br)


def _out_ln_kernel(y_ref, w_ref, x_ref, g_ref, b_ref, o_ref):
    y = ALPHA * x_ref[...] + _dg(y_ref[...], w_ref[...], NN)
    o_ref[...] = _layer_norm(y, g_ref[...], b_ref[...])


def _out_ln(merged, w_o, x, g, b):
    m, d = x.shape
    tm = min(WO_TM, m)
    row = lambda i: (i, 0)
    fix = lambda i: (0, 0)
    return pl.pallas_call(
        _out_ln_kernel,
        grid=(m // tm,),
        in_specs=[pl.BlockSpec((tm, d), row), pl.BlockSpec((d, d), fix), pl.BlockSpec((tm, d), row),
                  pl.BlockSpec((1, d), fix), pl.BlockSpec((1, d), fix)],
        out_specs=pl.BlockSpec((tm, d), row),
        out_shape=jax.ShapeDtypeStruct((m, d), F32),
        compiler_params=_cparams(("parallel",)),
        name="out_proj_ln",
    )(merged, w_o, x, g.reshape(1, d), b.reshape(1, d))


def _rwkv_branch(xb, wl, mu, w0, w2, a0, a2, g2, k_k, k_a, r_k, gn_g, gn_b, vres, batch, seq_len):
    p_rkv = _mm(xb, wl[:, :3 * RW_WIDTH].astype(BF16), F32)
    p_lora = _mm(xb, wl[:, 3 * RW_WIDTH:RW_COLS].astype(BF16), F32, tn=RW_LORA_COLS)
    if vres is None:
        return _wkv(p_rkv, p_lora, mu, w0, a0, w2, a2, g2, k_k, k_a, r_k.reshape(-1), gn_g, gn_b,
                    None, batch, seq_len)
    v_first, v0, v1, v2 = vres
    vlo = _rw_vlora(p_rkv, mu[2 * RW_WIDTH:3 * RW_WIDTH], v1, seq_len)
    y = _wkv(p_rkv, p_lora, mu, w0, a0, w2, a2, g2, k_k, k_a, r_k.reshape(-1), gn_g, gn_b,
             (v_first, v0, vlo, v2), batch, seq_len)
    return y, v_first


def _nsa_branch(xb, wl, pos_k, pos_v, phi_k1, phi_k2, phi_v1, phi_v2, batch, seq_len):
    q, kc, vc, ks, vs, kw, vw, p_g = _nsa_proj(xb, wl)
    gb = NSA_KV_GROUPS * batch
    per_seq = lambda z: z.reshape(gb, seq_len, z.shape[-1])
    k_cmp = _compress(per_seq(kc), pos_k, phi_k1, phi_k2)
    v_cmp = _compress(per_seq(vc), pos_v, phi_v1, phi_v2)
    return _nsa_attention(q, p_g, k_cmp, v_cmp, per_seq(ks), per_seq(vs), per_seq(kw), per_seq(vw),
                          batch, seq_len)


def kernel(x, w_in, rw_mu, rw_w0, rw_w2, rw_a0, rw_a2, rw_g2, rw_v0, rw_v1, rw_v2, rw_k_k, rw_k_a, rw_r_k, rw_gn_g, rw_gn_b, gm_ln_g, gm_ln_b, gm_ws, gm_bs, nsa_pos_k, nsa_pos_v, nsa_phi_k1, nsa_phi_k2, nsa_phi_v1, nsa_phi_v2, w_br, w_o, ffn1_wg, ffn1_wu, ffn1_wd, ffn2_wg, ffn2_wu, ffn2_wd, ln_g, ln_b):
    batch, seq_len, d = x.shape
    m = batch * seq_len
    h = x.reshape(m, d)
    v_first = None
    for l in range(DEPTH):
        h = _ffn(h, ffn1_wg[l].astype(BF16), ffn1_wu[l].astype(BF16), ffn1_wd[l].astype(BF16),
                 ln_g[l, 0], ln_b[l, 0])
        hb = h.astype(BF16)
        wl = w_in[l]
        vres = None if l == 0 else (v_first, rw_v0[l - 1], rw_v1[l - 1], rw_v2[l - 1])
        y_rw, v_out = _rwkv_branch(hb, wl, rw_mu[l], rw_w0[l], rw_w2[l], rw_a0[l], rw_a2[l], rw_g2[l],
                                   rw_k_k[l], rw_k_a[l], rw_r_k[l], rw_gn_g[l], rw_gn_b[l], vres,
                                   batch, seq_len)
        if l == 0:
            v_first = v_out
        p_gm = _mm(hb, wl[:, OFF_GM:OFF_NSA].astype(BF16), F32)
        y_gm = _gmlp(p_gm, gm_ln_g[l], gm_ln_b[l], gm_ws[l], gm_bs[l])
        y_ns = _nsa_branch(hb, wl, nsa_pos_k[l], nsa_pos_v[l], nsa_phi_k1[l], nsa_phi_k2[l],
                           nsa_phi_v1[l], nsa_phi_v2[l], batch, seq_len)
        p_gate = _mm(hb, wl[:, OFF_GATE:].astype(BF16), F32)
        merged = _merge(y_rw, y_gm, y_ns, p_gate, w_br[l].astype(BF16))
        h = _out_ln(merged, w_o[l].astype(BF16), h, ln_g[l, 1], ln_b[l, 1])
        h = _ffn(h, ffn2_wg[l].astype(BF16), ffn2_wu[l].astype(BF16), ffn2_wd[l].astype(BF16),
                 ln_g[l, 2], ln_b[l, 2])
    return h.reshape(batch, seq_len, d)
```

```python
import functools

import jax
import jax.numpy as jnp
from jax import lax
from jax.experimental import pallas as pl
from jax.experimental.pallas import tpu as pltpu

F32 = jnp.float32
BF16 = jnp.bfloat16

D_MODEL = 2048
DEPTH = 2
RW_HEAD = 64
RW_WIDTH = D_MODEL
RW_DECAY_LORA = 96
RW_AAA_LORA = 96
RW_MV_LORA = 64
RW_GATE_LORA = 256
RW_LORA_COLS = RW_DECAY_LORA + RW_AAA_LORA + RW_GATE_LORA
RW_GN_EPS = RW_HEAD * 1e-5
GM_WIDTH = D_MODEL
GM_CHUNK = 128
GM_GROUP = 128
GM_GROUPS = GM_WIDTH // GM_GROUP
NSA_HEADS = 16
NSA_KV_GROUPS = 4
NSA_HPG = NSA_HEADS // NSA_KV_GROUPS
NSA_DK = 192
NSA_DV = 128
NSA_WIDTH = NSA_HEADS * NSA_DV
CMP_BLK = 32
CMP_STRIDE = 16
SEL_BLK = 64
N_SEL = 16
WIN = 512
N_BRANCH = 3
ALPHA = (2 * DEPTH) ** 0.25
LN_EPS = 1e-5
NEG = -1e30
FORCED = 1e6
LOG2_E = 1.4426950408889634

RW_COLS = 3 * RW_WIDTH + RW_LORA_COLS
GM_COLS = 2 * GM_WIDTH
NSA_Q_COLS = NSA_HEADS * NSA_DK
NSA_GK = NSA_KV_GROUPS * NSA_DK
NSA_GV = NSA_KV_GROUPS * NSA_DV
NSA_KV_COLS = 3 * (NSA_GK + NSA_GV)
NSA_G_COLS = 3 * NSA_HEADS
NSA_COLS = NSA_Q_COLS + NSA_KV_COLS + NSA_G_COLS
OFF_GM = RW_COLS
OFF_NSA = OFF_GM + GM_COLS
OFF_GATE = OFF_NSA + NSA_COLS

V7X_VMEM_BYTES = 64 * 1024 * 1024
VMEM_LIMIT = V7X_VMEM_BYTES * 7 // 8

FFN_TM = 512
FFN_TF = 512
MM_TM = 1024
MM_TN = 512
MERGE_TM = 512
MERGE_TN = 512
WO_TM = 512
PREP_TM = 128
WKV_L = 64
WKV_TC = 512
WKV_PAIRS = 4
WKV_GROUPS = 2
GM_TM = 256
NSA_TQ = 256
NSA_TK = 1024
NSA_GROWS = 16

NN = (((1,), (0,)), ((), ()))
NT = (((1,), (1,)), ((), ()))
TN = (((0,), (0,)), ((), ()))


def _cparams(sem):
    return pltpu.CompilerParams(dimension_semantics=sem, vmem_limit_bytes=VMEM_LIMIT)


def _dg(a, b, dims):
    return lax.dot_general(a, b, dims, preferred_element_type=F32)


def _split2(x):
    hi = x.astype(BF16)
    lo = (x - hi.astype(F32)).astype(BF16)
    return hi, lo


def _split3(x):
    x1 = x.astype(BF16)
    r1 = x - x1.astype(F32)
    x2 = r1.astype(BF16)
    x3 = (r1 - x2.astype(F32)).astype(BF16)
    return x1, x2, x3


def _dot3(a, b, dims=NN):
    ah, al = _split2(a)
    bh, bl = _split2(b)
    return _dg(ah, bh, dims) + (_dg(ah, bl, dims) + _dg(al, bh, dims))


def _dot_exact_lhs(m01, b, dims=NN):
    b1, b2, b3 = _split3(b)
    return _dg(m01, b1, dims) + (_dg(m01, b2, dims) + _dg(m01, b3, dims))


def _layer_norm(y, g, b):
    mu = jnp.mean(y, axis=-1, keepdims=True)
    d = y - mu
    var = jnp.mean(d * d, axis=-1, keepdims=True)
    return d * lax.rsqrt(var + LN_EPS) * g + b


def _gelu_tanh(x):
    return 0.5 * x * (1.0 + jnp.tanh(0.7978845608028654 * (x + 0.044715 * (x * x * x))))


def _sigmoid(x):
    return 1.0 / (1.0 + jnp.exp(-x))


def _ffn_kernel(x_ref, wg_ref, wu_ref, wd_ref, g_ref, b_ref, o_ref, xb_ref):
    j = pl.program_id(1)

    @pl.when(j == 0)
    def _():
        o_ref[...] = jnp.zeros_like(o_ref)
        xb_ref[...] = x_ref[...].astype(BF16)

    xb = xb_ref[...]
    hg = _dg(xb, wg_ref[...], NN)
    hu = _dg(xb, wu_ref[...], NN)
    h = (hg * _sigmoid(hg) * hu).astype(BF16)
    o_ref[...] += _dg(h, wd_ref[...], NN)

    @pl.when(j == pl.num_programs(1) - 1)
    def _():
        y = ALPHA * x_ref[...] + 0.5 * o_ref[...]
        o_ref[...] = _layer_norm(y, g_ref[...], b_ref[...])


def _ffn(x, wg, wu, wd, g, b):
    m, d = x.shape
    ff = wg.shape[1]
    tm = min(FFN_TM, m)
    return pl.pallas_call(
        _ffn_kernel,
        grid=(m // tm, ff // FFN_TF),
        in_specs=[
            pl.BlockSpec((tm, d), lambda i, j: (i, 0)),
            pl.BlockSpec((d, FFN_TF), lambda i, j: (0, j)),
            pl.BlockSpec((d, FFN_TF), lambda i, j: (0, j)),
            pl.BlockSpec((FFN_TF, d), lambda i, j: (j, 0)),
            pl.BlockSpec((1, d), lambda i, j: (0, 0)),
            pl.BlockSpec((1, d), lambda i, j: (0, 0)),
        ],
        out_specs=pl.BlockSpec((tm, d), lambda i, j: (i, 0)),
        out_shape=jax.ShapeDtypeStruct((m, d), F32),
        scratch_shapes=[pltpu.VMEM((tm, d), BF16)],
        compiler_params=_cparams(("parallel", "arbitrary")),
        name="ffn_swiglu_ln",
    )(x, wg, wu, wd, g.reshape(1, d), b.reshape(1, d))


def _mm_kernel(x_ref, w_ref, o_ref):
    o_ref[...] = _dg(x_ref[...], w_ref[...], NN).astype(o_ref.dtype)


def _mm(x, w, out_dtype, tn=MM_TN):
    m, k = x.shape
    n = w.shape[1]
    tm = min(MM_TM, m)
    tn = min(tn, n)
    return pl.pallas_call(
        _mm_kernel,
        grid=(m // tm, n // tn),
        in_specs=[
            pl.BlockSpec((tm, k), lambda i, j: (i, 0)),
            pl.BlockSpec((k, tn), lambda i, j: (0, j)),
        ],
        out_specs=pl.BlockSpec((tm, tn), lambda i, j: (i, j)),
        out_shape=jax.ShapeDtypeStruct((m, n), out_dtype),
        compiler_params=_cparams(("parallel", "arbitrary")),
        name="proj_mm",
    )(x, w)


def _token_shift(x, prev_row, mu, first):
    rows = lax.broadcasted_iota(jnp.int32, x.shape, 0)
    prev_row = jnp.where(first, jnp.zeros_like(prev_row), prev_row)
    shifted = jnp.where(rows == 0, prev_row, pltpu.roll(x, 1, 0))
    return x + (shifted - x) * mu


def _rw_vlora_kernel(seq_len, p_ref, pp_ref, mu_ref, v1_ref, o_ref):
    first = (pl.program_id(0) * p_ref.shape[0]) % seq_len == 0
    v = _token_shift(p_ref[...], pp_ref[7:8, :], mu_ref[...], first)
    o_ref[...] = _dot3(v, v1_ref[...])


def _rw_vlora(p_rkv, mu_v, v1, seq_len):
    m = p_rkv.shape[0]
    tm = PREP_TM
    wd = RW_WIDTH
    return pl.pallas_call(
        functools.partial(_rw_vlora_kernel, seq_len),
        grid=(m // tm,),
        in_specs=[pl.BlockSpec((tm, wd), lambda i: (i, 2)),
                  pl.BlockSpec((8, wd), lambda i: (jnp.maximum(i * (tm // 8) - 1, 0), 2)),
                  pl.BlockSpec((1, wd), lambda i: (0, 0)),
                  pl.BlockSpec((wd, RW_MV_LORA), lambda i: (0, 0))],
        out_specs=pl.BlockSpec((tm, RW_MV_LORA), lambda i: (i, 0)),
        out_shape=jax.ShapeDtypeStruct((m, RW_MV_LORA), F32),
        compiler_params=_cparams(("parallel",)),
        name="rwkv_vlora",
    )(p_rkv, p_rkv, mu_v.reshape(1, wd), v1)


def _sp(x):
    return _split2(x)


def _d3s(a, b, dims=NN):
    return _dg(a[0], b[0], dims) + (_dg(a[0], b[1], dims) + _dg(a[1], b[0], dims))


def _wkv_kernel(has_vres, *refs):
    (pr_ref, pk_ref, pv_ref, ppr_ref, ppk_ref, ppv_ref, lo_ref, plo_ref,
     mur_ref, muk_ref, muv_ref, mul_ref, w0_ref, a0_ref, w2_ref, a2_ref, g2_ref,
     kk_ref, ka_ref, rk_ref, gg_ref, gb_ref) = refs[:22]
    if has_vres:
        vf_ref, v0_ref, vlo_ref, v2_ref, o_ref, s_ref = refs[22:]
    else:
        o_ref, vout_ref, s_ref = refs[22:]
    L = WKV_L
    n_groups = WKV_GROUPS
    rg = pr_ref.shape[0] // n_groups
    n_chunks = rg // L

    @pl.when(pl.program_id(2) == 0)
    def _():
        s_ref[...] = jnp.zeros_like(s_ref)

    ri = lax.broadcasted_iota(jnp.int32, (2 * L, 2 * L), 0)
    ci = lax.broadcasted_iota(jnp.int32, (2 * L, 2 * L), 1)
    bd_f = ((ri // L) == (ci // L)).astype(F32)
    bd_b = bd_f.astype(BF16)
    rl = lax.broadcasted_iota(jnp.int32, (L, L), 0)
    cl = lax.broadcasted_iota(jnp.int32, (L, L), 1)
    tri_b = (cl <= rl).astype(BF16)
    rt_i = lax.broadcasted_iota(jnp.int32, (L, 2 * L), 0)
    cs_i = lax.broadcasted_iota(jnp.int32, (L, 2 * L), 1) % L
    strict = cs_i < rt_i
    incl = cs_i <= rt_i

    bd2_b = jnp.concatenate([bd_b, bd_b], axis=1)
    eye_f = (ri == ci).astype(F32)

    def sm(zb, mask=bd_b):
        return jnp.concatenate([zb, zb], axis=0) * mask

    def b16(x):
        return x.astype(BF16)

    def hsum(x):
        hi, lo = _sp(x)
        s2 = _dg(jnp.concatenate([hi, lo], axis=0), bd_b, NN)
        return s2[:L] + s2[L:]

    def cumsum_rows(x):
        x1, x2, x3 = _split3(x)
        s3 = _dg(tri_b, jnp.concatenate([x1, x2, x3], axis=1), NN)
        w = x.shape[1]
        return s3[:, :w] + (s3[:, w:2 * w] + s3[:, 2 * w:])

    inv_n = 1.0 / RW_HEAD
    n_lvl = L.bit_length() - 1
    n_pairs = pr_ref.shape[1] // (2 * L)

    C = range(n_chunks * n_pairs)
    rows = [slice((c // n_pairs) * L, (c // n_pairs + 1) * L) for c in C]
    lanes = [slice((c % n_pairs) * 2 * L, (c % n_pairs + 1) * 2 * L) for c in C]
    fronts = {}

    def front(g):
        rs = slice(g * rg, (g + 1) * rg)
        if g == 0:
            first = pl.program_id(2) == 0
            prev = lambda p_ref, pp_ref: pp_ref[7:8, :]
        else:
            first = False
            prev = lambda p_ref, pp_ref: p_ref[g * rg - 1:g * rg, :]
        r_all = _token_shift(pr_ref[rs, :], prev(pr_ref, ppr_ref), mur_ref[...], first)
        yield
        k_all = _token_shift(pk_ref[rs, :], prev(pk_ref, ppk_ref), muk_ref[...], first)
        yield
        v_all = _token_shift(pv_ref[rs, :], prev(pv_ref, ppv_ref), muv_ref[...], first)
        yield
        lo = _token_shift(lo_ref[rs, :], prev(lo_ref, plo_ref), mul_ref[...], first)
        wl = lo[:, :RW_DECAY_LORA]
        al = lo[:, RW_DECAY_LORA:RW_DECAY_LORA + RW_AAA_LORA]
        gl = lo[:, RW_DECAY_LORA + RW_AAA_LORA:]
        yield
        z = w0_ref[...] + _dot3(jnp.tanh(wl), w2_ref[...])
        yield
        w_all = -(jnp.maximum(-z, 0.0) + jnp.log1p(jnp.exp(-jnp.abs(z)))) - 0.5
        yield
        a_all = _sigmoid(a0_ref[...] + _dot3(al, a2_ref[...]))
        yield
        g_all = _dot3(_sigmoid(gl), g2_ref[...])
        yield
        if has_vres:
            mix = _sigmoid(v0_ref[...] + _dot3(vlo_ref[rs, :], v2_ref[...]))
            v_all = v_all + (vf_ref[rs, :] - v_all) * mix
        else:
            vout_ref[rs, :] = v_all
        yield
        k_k = [kk_ref[:, lanes[c]] for c in C]
        k_a = [ka_ref[:, lanes[c]] for c in C]
        r_k = [rk_ref[:, lanes[c]] for c in C]
        r = [r_all[rows[c], lanes[c]] for c in C]
        k_raw = [k_all[rows[c], lanes[c]] for c in C]
        v = [v_all[rows[c], lanes[c]] for c in C]
        a = [a_all[rows[c], lanes[c]] for c in C]
        kk = [k_raw[c] * k_k[c] for c in C]
        ssq = [hsum(kk[c] * kk[c]) for c in C]
        yield
        lw = [-jnp.exp(w_all[rows[c], lanes[c]]) for c in C]
        cum = [cumsum_rows(lw[c]) for c in C]
        yield
        kk = [kk[c] * lax.rsqrt(jnp.maximum(ssq[c], 1e-24)) for c in C]
        k = [k_raw[c] * (1.0 + (a[c] - 1.0) * k_a[c]) for c in C]
        bv = [kk[c] * a[c] for c in C]
        yield
        tot = [cum[c][L - 1:L, :] for c in C]
        e_neg = [jnp.exp(-cum[c]) for c in C]
        rt = [r[c] * jnp.exp(cum[c]) for c in C]
        yield
        at = [-kk[c] * jnp.exp(cum[c] - lw[c]) for c in C]
        e_tot = [jnp.exp(tot[c] - cum[c]) for c in C]
        yield
        bh = [b16(bv[c] * e_tot[c]) for c in C]
        kh = [b16(k[c] * e_tot[c]) for c in C]
        vb = [b16(v[c]) for c in C]
        smv = [sm(vb[c]) for c in C]
        yield
        lhs2 = [b16(jnp.concatenate([at[c], rt[c]], axis=0)) for c in C]
        smkb = [jnp.concatenate([sm(b16(k[c] * e_neg[c])), sm(b16(bv[c] * e_neg[c]))], axis=0) for c in C]
        yield
        bonus = [hsum(r[c] * k[c] * r_k[c]) * v[c] for c in C]
        g_out = [g_all[rows[c], lanes[c]] for c in C]
        fronts[g] = (tot, rt, at, bh, kh, vb, smv, lhs2, smkb, bonus, g_out)

    def drain(gen):
        for _ in gen:
            pass

    def run_group(g, state, pump):
        tot, rt, at, bh, kh, vb, smv, lhs2, smkb, bonus, g_out = fronts.pop(g)
        akb = [_dg(lhs2[c], smkb[c], NT) for c in C]
        pump()
        ak = [akb[c][:, :2 * L] for c in C]
        ab = [akb[c][:, 2 * L:] for c in C]
        a_ak = [b16(jnp.where(strict, ak[c][:L], 0.0)) for c in C]
        a_rk = [b16(jnp.where(incl, ak[c][L:], 0.0)) for c in C]
        p = [jnp.where(strict, ab[c][:L], 0.0) for c in C]
        a_rb = [b16(jnp.where(incl, ab[c][L:], 0.0)) for c in C]
        akv = [_dg(a_ak[c], smv[c], NN) for c in C]
        pump()
        z = [jnp.concatenate([at[c], akv[c]], axis=1) for c in C]
        for lvl in range(n_lvl):
            pb = [b16(p[c]) for c in C]
            smz = [sm(b16(z[c]), bd2_b) for c in C]
            z = [z[c] + _dg(pb[c], smz[c], NN) for c in C]
            pump()
            if lvl + 1 < n_lvl:
                smp = [sm(pb[c]) for c in C]
                p = [_dg(pb[c], smp[c], NN) for c in C]
                pump()
        zb = [b16(z[c]) for c in C]
        qy = [_dg(a_rb[c], sm(zb[c], bd2_b), NN) for c in C]
        pump()
        q = [b16(rt[c] + qy[c][:, :2 * L]) for c in C]
        y0 = [_dg(a_rk[c], smv[c], NN) + qy[c][:, 2 * L:] for c in C]
        pump()
        zbh = [_dg(zb[c], bh[c], TN) for c in C]
        gam_s = [_sp(bd_f * zbh[c][:2 * L] + eye_f * jnp.exp(tot[c])) for c in C]
        cc = [bd_f * (_dg(vb[c], kh[c], TN) + zbh[c][2 * L:]) for c in C]
        pump()

        y = []
        for c in C:
            pi = c % n_pairs
            st_s = _sp(state[pi])
            y.append(_dg(q[c], st_s[0], NT) + y0[c])
            state[pi] = _d3s(st_s, gam_s[c]) + cc[c]
        mean = [hsum(y[c]) * inv_n for c in C]
        d = [y[c] - mean[c] for c in C]
        var = [hsum(d[c] * d[c]) * inv_n for c in C]
        for c in C:
            yn = d[c] * lax.rsqrt(var[c] + RW_GN_EPS) * gg_ref[:, lanes[c]] + gb_ref[:, lanes[c]]
            o_ref[pl.ds(g * rg + rows[c].start, L), lanes[c]] = ((yn + bonus[c]) * g_out[c]).astype(o_ref.dtype)
        return state

    state = [s_ref[pi] for pi in range(n_pairs)]
    drain(front(0))
    for g in range(n_groups):
        nxt = front(g + 1) if g + 1 < n_groups else iter(())
        state = run_group(g, state, lambda: next(nxt, None))
        drain(nxt)
    for pi in range(n_pairs):
        s_ref[pi] = state[pi]


def _wkv(p_rkv, p_lora, mu, w0, a0, w2, a2, g2, k_k, k_a, r_k, gn_g, gn_b, vres, batch, seq_len):
    m = p_rkv.shape[0]
    wd = RW_WIDTH
    tc = min(WKV_TC, seq_len)
    nt = seq_len // tc
    pair = 2 * RW_HEAD
    lanes = WKV_PAIRS * pair
    nb = wd // lanes
    row = lambda b, h, t: b * nt + t
    prv = lambda b, h, t: jnp.maximum((b * nt + t) * (tc // 8) - 1, 0)
    act = lambda sec: pl.BlockSpec((tc, lanes), lambda b, h, t: (row(b, h, t), sec * nb + h))
    prev = lambda sec: pl.BlockSpec((8, lanes), lambda b, h, t: (prv(b, h, t), sec * nb + h))
    par = lambda sec=0: pl.BlockSpec((1, lanes), lambda b, h, t: (0, sec * nb + h))
    mat = lambda k: pl.BlockSpec((k, lanes), lambda b, h, t: (0, h))
    lo_w = RW_LORA_COLS
    c3 = 3 * wd
    in_specs = [act(0), act(1), act(2), prev(0), prev(1), prev(2),
                pl.BlockSpec((tc, lo_w), lambda b, h, t: (row(b, h, t), 0)),
                pl.BlockSpec((8, lo_w), lambda b, h, t: (prv(b, h, t), 0)),
                par(0), par(1), par(2), pl.BlockSpec((1, lo_w), lambda b, h, t: (0, 0)),
                par(), par(), mat(RW_DECAY_LORA), mat(RW_AAA_LORA), mat(RW_GATE_LORA)] + [par()] * 5
    mu_rkv = mu[:c3].reshape(1, c3)
    args = [p_rkv, p_rkv, p_rkv, p_rkv, p_rkv, p_rkv, p_lora, p_lora,
            mu_rkv, mu_rkv, mu_rkv, mu[c3:].reshape(1, lo_w),
            w0.reshape(1, wd), a0.reshape(1, wd), w2, a2, g2]
    args += [z.reshape(1, wd) for z in (k_k, k_a, r_k, gn_g, gn_b)]
    out_act = pl.BlockSpec((tc, lanes), lambda b, h, t: (row(b, h, t), h))
    if vres is not None:
        v_first, v0, vlo, v2 = vres
        in_specs += [out_act, par(), pl.BlockSpec((tc, RW_MV_LORA), lambda b, h, t: (row(b, h, t), 0)),
                     mat(RW_MV_LORA)]
        args += [v_first, v0.reshape(1, wd), vlo, v2]
        out_specs = out_act
        out_shape = jax.ShapeDtypeStruct((m, wd), BF16)
    else:
        out_specs = [out_act, out_act]
        out_shape = [jax.ShapeDtypeStruct((m, wd), BF16), jax.ShapeDtypeStruct((m, wd), F32)]
    return pl.pallas_call(
        functools.partial(_wkv_kernel, vres is not None),
        grid=(batch, nb, nt),
        in_specs=in_specs,
        out_specs=out_specs,
        out_shape=out_shape,
        scratch_shapes=[pltpu.VMEM((WKV_PAIRS, pair, pair), F32)],
        compiler_params=_cparams(("parallel", "parallel", "arbitrary")),
        name="rwkv_wkv",
    )(*args)


def _gmlp_kernel(p_ref, lg_ref, lb_ref, ws_ref, bst_ref, o_ref):
    tm = p_ref.shape[0]
    u = _gelu_tanh(p_ref[:, :GM_WIDTH])
    v = _layer_norm(_gelu_tanh(p_ref[:, GM_WIDTH:]), lg_ref[...], lb_ref[...])
    rl = lax.broadcasted_iota(jnp.int32, (GM_CHUNK, GM_CHUNK), 0)
    cl = lax.broadcasted_iota(jnp.int32, (GM_CHUNK, GM_CHUNK), 1)
    causal = cl <= rl
    for g in range(GM_GROUPS):
        cols = slice(g * GM_GROUP, (g + 1) * GM_GROUP)
        wsg = jnp.where(causal, ws_ref[g], 0.0)
        bias = bst_ref[:, g:g + 1]
        for c in range(tm // GM_CHUNK):
            rows = slice(c * GM_CHUNK, (c + 1) * GM_CHUNK)
            s = _dot3(wsg, v[rows, cols]) + bias
            o_ref[rows, cols] = (u[rows, cols] * s).astype(o_ref.dtype)


def _gmlp(p_gm, ln_g, ln_b, ws, bs):
    m = p_gm.shape[0]
    tm = GM_TM
    fix2 = lambda i: (0, 0)
    return pl.pallas_call(
        _gmlp_kernel,
        grid=(m // tm,),
        in_specs=[
            pl.BlockSpec((tm, GM_COLS), lambda i: (i, 0)),
            pl.BlockSpec((1, GM_WIDTH), fix2),
            pl.BlockSpec((1, GM_WIDTH), fix2),
            pl.BlockSpec((GM_GROUPS, GM_CHUNK, GM_CHUNK), lambda i: (0, 0, 0)),
            pl.BlockSpec((GM_CHUNK, GM_GROUPS), fix2),
        ],
        out_specs=pl.BlockSpec((tm, GM_WIDTH), lambda i: (i, 0)),
        out_shape=jax.ShapeDtypeStruct((m, GM_WIDTH), BF16),
        compiler_params=_cparams(("parallel",)),
        name="gmlp_mix",
    )(p_gm, ln_g.reshape(1, -1), ln_b.reshape(1, -1), ws, bs.T)


def _compress_kernel(c_ref, pos_ref, w1_ref, w2_ref, o_ref):
    c = c_ref[...]
    n = c.shape[0]
    r1 = _dot3(c, w1_ref[0])
    r2 = _dot3(c, w1_ref[1])
    pos = jnp.broadcast_to(pos_ref[...], (8, pos_ref.shape[1]))
    half = w1_ref.shape[1]
    pterm = (_dot3(pos[:, :half], w1_ref[0]) + _dot3(pos[:, half:], w1_ref[1]))[0:1, :]
    h = r1 + pltpu.roll(r2, n - 1, 0) + pterm
    o_ref[...] = _dot3(_gelu_tanh(h), w2_ref[...])


def _compress(z, pos, w1, w2):
    gb, t, d = z.shape
    nb = t // CMP_STRIDE
    c = z.reshape(gb, nb, CMP_STRIDE * d)
    half = CMP_STRIDE * d
    return pl.pallas_call(
        _compress_kernel,
        grid=(gb,),
        in_specs=[
            pl.BlockSpec((None, nb, half), lambda i: (i, 0, 0)),
            pl.BlockSpec((1, 2 * half), lambda i: (0, 0)),
            pl.BlockSpec((2, half, d), lambda i: (0, 0, 0)),
            pl.BlockSpec((d, d), lambda i: (0, 0)),
        ],
        out_specs=pl.BlockSpec((None, nb, d), lambda i: (i, 0, 0)),
        out_shape=jax.ShapeDtypeStruct((gb, nb, d), F32),
        compiler_params=_cparams(("parallel",)),
        name="nsa_compress",
    )(c, pos.reshape(1, 2 * half), w1.reshape(2, half, d), w2)


def _nsa_proj_kernel(x_ref, wq_ref, wkv_ref, wgt_ref,
                     q_ref, kc_ref, vc_ref, ks_ref, vs_ref, kw_ref, vw_ref, gt_ref):
    x = x_ref[...]
    scale = NSA_DK ** -0.5 * LOG2_E
    rq = _dg(x, wq_ref[...], NN)
    for h in range(NSA_HPG):
        q_ref[h] = (rq[:, h * NSA_DK:(h + 1) * NSA_DK] * scale).astype(q_ref.dtype)
    rkv = _dg(x, wkv_ref[...], NN)
    o = 0
    for o_ref, d in ((vc_ref, NSA_DV), (vs_ref, NSA_DV), (vw_ref, NSA_DV),
                     (kc_ref, NSA_DK), (ks_ref, NSA_DK), (kw_ref, NSA_DK)):
        o_ref[...] = rkv[:, o:o + d].astype(o_ref.dtype)
        o += d
    gt_ref[...] = _dg(wgt_ref[...], x, NT)


def _nsa_proj(xb, wl):
    m, kdim = xb.shape
    G, hp = NSA_KV_GROUPS, NSA_HPG
    tm = min(MM_TM, m)
    o = OFF_NSA

    def take(width, d):
        nonlocal o
        w3 = wl[:, o:o + width].astype(BF16).reshape(kdim, width // d, d)
        o += width
        return w3

    wq = take(NSA_Q_COLS, hp * NSA_DK).reshape(kdim, NSA_Q_COLS)
    kc, vc, ks, vs, kw, vw = (take(NSA_GK, NSA_DK), take(NSA_GV, NSA_DV), take(NSA_GK, NSA_DK),
                              take(NSA_GV, NSA_DV), take(NSA_GK, NSA_DK), take(NSA_GV, NSA_DV))
    wkv = jnp.concatenate([vc, vs, vw, kc, ks, kw], axis=2).transpose(1, 0, 2)
    nkv = wkv.shape[2]
    wgt = take(NSA_G_COLS, 3 * hp).transpose(1, 2, 0)
    wgt = jnp.pad(wgt, ((0, 0), (0, NSA_GROWS - 3 * hp), (0, 0)))
    ospec = lambda d: pl.BlockSpec((None, tm, d), lambda i, g: (g, i, 0))
    oshape = lambda d, dt: jax.ShapeDtypeStruct((G, m, d), dt)
    return pl.pallas_call(
        _nsa_proj_kernel,
        grid=(m // tm, G),
        in_specs=[pl.BlockSpec((tm, kdim), lambda i, g: (i, 0)),
                  pl.BlockSpec((kdim, hp * NSA_DK), lambda i, g: (0, g)),
                  pl.BlockSpec((None, kdim, nkv), lambda i, g: (g, 0, 0)),
                  pl.BlockSpec((None, NSA_GROWS, kdim), lambda i, g: (g, 0, 0))],
        out_specs=[pl.BlockSpec((hp, tm, NSA_DK), lambda i, g: (g, i, 0)),
                   ospec(NSA_DK), ospec(NSA_DV), ospec(NSA_DK), ospec(NSA_DV), ospec(NSA_DK), ospec(NSA_DV),
                   pl.BlockSpec((None, NSA_GROWS, tm), lambda i, g: (g, 0, i))],
        out_shape=[jax.ShapeDtypeStruct((G * hp, m, NSA_DK), BF16),
                   oshape(NSA_DK, F32), oshape(NSA_DV, F32), oshape(NSA_DK, BF16), oshape(NSA_DV, BF16),
                   oshape(NSA_DK, BF16), oshape(NSA_DV, BF16),
                   jax.ShapeDtypeStruct((G, NSA_GROWS, m), F32)],
        compiler_params=_cparams(("parallel", "arbitrary")),
        name="nsa_proj",
    )(xb, wq, wkv, wgt)


def _nsa_kernel(seq_len, q_ref, gt_ref, kc_ref, vc_ref, ks_ref, vs_ref, kw_ref, vw_ref,
                o_ref, bias_ref):
    tq = NSA_TQ
    hp = NSA_HPG
    nr = hp * tq
    tk = min(NSA_TK, seq_len)
    wk = WIN + tq
    n_s = seq_len // SEL_BLK
    k_sel = min(N_SEL, n_s)
    n_cb = kc_ref.shape[0]
    i = pl.program_id(2)
    t0 = i * tq

    qa = q_ref[...].reshape(nr, NSA_DK)
    t_lane = t0 + lax.broadcasted_iota(jnp.int32, (1, tq), 1)
    t_all = t0 + (lax.broadcasted_iota(jnp.int32, (1, nr), 1) & (tq - 1))
    tile_heads = lambda z: jnp.concatenate([z] * hp, axis=1)

    w0 = pl.multiple_of(jnp.maximum(t0 - WIN, 0), tq)
    kw_tile = kw_ref[pl.ds(w0, wk), :]
    vw_tile = vw_ref[pl.ds(w0, wk), :]
    s_w = _dg(kw_tile, qa, NT)
    s_c = _dg(kc_ref[...].astype(BF16), qa, NT)

    n_end = lax.broadcasted_iota(jnp.int32, (n_cb, 1), 0) * CMP_STRIDE + (CMP_BLK - 1)
    m_c = (n_end <= t_all) & (n_end < seq_len)
    s_c = jnp.where(m_c, s_c, NEG)
    e_c = jnp.where(m_c, jnp.exp2(s_c - jnp.max(s_c, axis=0, keepdims=True)), 0.0)
    p_c = e_c * (1.0 / jnp.maximum(jnp.sum(e_c, axis=0, keepdims=True), 1e-30))
    o_c = _dg(vc_ref[...].astype(BF16), p_c.astype(BF16), TN)

    p_sum = p_c[:, 0:tq]
    for h in range(1, hp):
        p_sum = p_sum + p_c[:, h * tq:(h + 1) * tq]
    ss = lax.broadcasted_iota(jnp.int32, (n_s, n_cb), 0) * SEL_BLK
    cs = lax.broadcasted_iota(jnp.int32, (n_s, n_cb), 1) * CMP_STRIDE
    overlap_t = ((cs < ss + SEL_BLK) & (cs + (CMP_BLK - 1) >= ss)
                 & (cs + (CMP_BLK - 1) < seq_len)).astype(BF16)
    imp = _dot_exact_lhs(overlap_t, p_sum, NN)

    kpos_w = w0 + lax.broadcasted_iota(jnp.int32, (wk, 1), 0)
    bias_w = jnp.where((kpos_w <= t_lane) & (kpos_w > t_lane - WIN), 0.0, NEG)
    s_w = s_w + tile_heads(bias_w)
    p_w = jnp.exp2(s_w - jnp.max(s_w, axis=0, keepdims=True))
    l_w = jnp.sum(p_w, axis=0, keepdims=True)
    o_w = _dg(vw_tile, p_w.astype(BF16), TN) * (1.0 / jnp.maximum(l_w, 1e-30))

    blk = lax.broadcasted_iota(jnp.int32, (n_s, 1), 0)
    cur = t_lane // SEL_BLK
    valid = blk * SEL_BLK <= t_lane
    forced = valid & ((blk == 0) | (blk == cur) | (blk == cur - 1))
    score = jnp.where(forced, FORCED, jnp.where(valid, imp, NEG))
    rank = jnp.zeros((n_s, tq), jnp.int32)
    for s in range(n_s):
        row = score[s:s + 1, :]
        beats = (row > score) | ((row == score) & (blk > s))
        rank = rank + beats.astype(jnp.int32)
    sel_t = ((rank < k_sel) & (score > 0.5 * NEG)).astype(BF16)
    n_kt = (t0 + tq + tk - 1) // tk
    n_all = seq_len // tk
    half = max(n_all // 2, 1)

    def expand(kt0, kt1):
        width = (kt1 - kt0) * tk
        kpos = kt0 * tk + lax.broadcasted_iota(jnp.int32, (width, 1), 0)
        er = (lax.broadcasted_iota(jnp.int32, (width, n_s), 0) + kt0 * tk) // SEL_BLK
        ec = lax.broadcasted_iota(jnp.int32, (width, n_s), 1)
        chosen = _dg((er == ec).astype(BF16), sel_t, NN)
        bias = jnp.where((chosen > 0.5) & (kpos <= t_lane), 0.0, NEG)
        for kt in range(kt0, kt1):
            bias_ref[kt] = bias[(kt - kt0) * tk:(kt - kt0 + 1) * tk]

    expand(0, half)
    if n_all > half:
        @pl.when(n_kt > half)
        def _():
            expand(half, n_all)

    def sel_body(kt, carry):
        m_run, l_run, acc = carry
        k0 = pl.multiple_of(kt * tk, tk)
        s = _dg(ks_ref[pl.ds(k0, tk), :], qa, NT) + tile_heads(bias_ref[kt])
        m_new = jnp.maximum(m_run, jnp.max(s, axis=0, keepdims=True))
        alpha = jnp.exp2(m_run - m_new)
        p = jnp.exp2(s - m_new)
        l_new = alpha * l_run + jnp.sum(p, axis=0, keepdims=True)
        acc = alpha * acc + _dg(vs_ref[pl.ds(k0, tk), :], p.astype(BF16), TN)
        return m_new, l_new, acc

    init = (jnp.full((1, nr), NEG, F32), jnp.zeros((1, nr), F32), jnp.zeros((NSA_DV, nr), F32))
    _, l_s, acc_s = lax.fori_loop(0, n_kt, sel_body, init)
    o_s = acc_s * (1.0 / jnp.maximum(l_s, 1e-30))

    gates = _sigmoid(gt_ref[...])
    for h in range(hp):
        lanes = slice(h * tq, (h + 1) * tq)
        out_t = (gates[3 * h:3 * h + 1] * o_c[:, lanes] + gates[3 * h + 1:3 * h + 2] * o_s[:, lanes]
                 + gates[3 * h + 2:3 * h + 3] * o_w[:, lanes])
        o_ref[:, h * NSA_DV:(h + 1) * NSA_DV] = out_t.T.astype(o_ref.dtype)


def _nsa_attention(q, p_gt, k_cmp, v_cmp, ks, vs, kw, vw, batch, seq_len):
    m = q.shape[1]
    tq = NSA_TQ
    assert seq_len >= WIN + tq and seq_len % NSA_TK == 0
    nq = seq_len // tq
    n_cb = k_cmp.shape[1]
    tk = min(NSA_TK, seq_len)
    kv = lambda b, g, i: (g * batch + b, 0, 0)
    return pl.pallas_call(
        functools.partial(_nsa_kernel, seq_len),
        grid=(batch, NSA_KV_GROUPS, nq),
        in_specs=[
            pl.BlockSpec((NSA_HPG, tq, NSA_DK), lambda b, g, i: (g, b * nq + i, 0)),
            pl.BlockSpec((None, NSA_GROWS, tq), lambda b, g, i: (g, 0, b * nq + i)),
            pl.BlockSpec((None, n_cb, NSA_DK), kv),
            pl.BlockSpec((None, n_cb, NSA_DV), kv),
            pl.BlockSpec((None, seq_len, NSA_DK), kv),
            pl.BlockSpec((None, seq_len, NSA_DV), kv),
            pl.BlockSpec((None, seq_len, NSA_DK), kv),
            pl.BlockSpec((None, seq_len, NSA_DV), kv),
        ],
        out_specs=pl.BlockSpec((tq, NSA_HPG * NSA_DV), lambda b, g, i: (b * nq + i, g)),
        out_shape=jax.ShapeDtypeStruct((m, NSA_WIDTH), BF16),
        scratch_shapes=[pltpu.VMEM((seq_len // tk, tk, tq), F32)],
        compiler_params=_cparams(("parallel", "parallel", "arbitrary")),
        name="nsa_attention",
    )(q, p_gt, k_cmp, v_cmp, ks, vs, kw, vw)


def _merge_kernel(y0_ref, y1_ref, y2_ref, g0_ref, g1_ref, g2_ref, w_ref, o_ref):
    acc = _sigmoid(g0_ref[...]) * _dg(y0_ref[...], w_ref[0], NN)
    acc = acc + _sigmoid(g1_ref[...]) * _dg(y1_ref[...], w_ref[1], NN)
    acc = acc + _sigmoid(g2_ref[...]) * _dg(y2_ref[...], w_ref[2], NN)
    o_ref[...] = acc.astype(o_ref.dtype)


def _merge(y_rw, y_gm, y_ns, p_gate, w_br):
    m, d = y_rw.shape
    tm = min(MERGE_TM, m)
    tn = MERGE_TN
    nj = d // tn
    ys = pl.BlockSpec((tm, d), lambda i, j: (i, 0))
    gspec = lambda br: pl.BlockSpec((tm, tn), lambda i, j: (i, br * nj + j))
    return pl.pallas_call(
        _merge_kernel,
        grid=(m // tm, nj),
        in_specs=[ys, ys, ys, gspec(0), gspec(1), gspec(2),
                  pl.BlockSpec((N_BRANCH, d, tn), lambda i, j: (0, 0, j))],
        out_specs=pl.BlockSpec((tm, tn), lambda i, j: (i, j)),
        out_shape=jax.ShapeDtypeStruct((m, d), BF16),
        compiler_params=_cparams(("parallel", "arbitrary")),
        name="branch_merge",
    )(y_rw, y_gm, y_ns, p_gate, p_gate, p_gate, w_br)


def _out_ln_kernel(y_ref, w_ref, x_ref, g_ref, b_ref, o_ref):
    y = ALPHA * x_ref[...] + _dg(y_ref[...], w_ref[...], NN)
    o_ref[...] = _layer_norm(y, g_ref[...], b_ref[...])


def _out_ln(merged, w_o, x, g, b):
    m, d = x.shape
    tm = min(WO_TM, m)
    row = lambda i: (i, 0)
    fix = lambda i: (0, 0)
    return pl.pallas_call(
        _out_ln_kernel,
        grid=(m // tm,),
        in_specs=[pl.BlockSpec((tm, d), row), pl.BlockSpec((d, d), fix), pl.BlockSpec((tm, d), row),
                  pl.BlockSpec((1, d), fix), pl.BlockSpec((1, d), fix)],
        out_specs=pl.BlockSpec((tm, d), row),
        out_shape=jax.ShapeDtypeStruct((m, d), F32),
        compiler_params=_cparams(("parallel",)),
        name="out_proj_ln",
    )(merged, w_o, x, g.reshape(1, d), b.reshape(1, d))


def _rwkv_branch(xb, wl, mu, w0, w2, a0, a2, g2, k_k, k_a, r_k, gn_g, gn_b, vres, batch, seq_len):
    p_rkv = _mm(xb, wl[:, :3 * RW_WIDTH].astype(BF16), F32)
    p_lora = _mm(xb, wl[:, 3 * RW_WIDTH:RW_COLS].astype(BF16), F32, tn=RW_LORA_COLS)
    if vres is None:
        return _wkv(p_rkv, p_lora, mu, w0, a0, w2, a2, g2, k_k, k_a, r_k.reshape(-1), gn_g, gn_b,
                    None, batch, seq_len)
    v_first, v0, v1, v2 = vres
    vlo = _rw_vlora(p_rkv, mu[2 * RW_WIDTH:3 * RW_WIDTH], v1, seq_len)
    y = _wkv(p_rkv, p_lora, mu, w0, a0, w2, a2, g2, k_k, k_a, r_k.reshape(-1), gn_g, gn_b,
             (v_first, v0, vlo, v2), batch, seq_len)
    return y, v_first


def _nsa_branch(xb, wl, pos_k, pos_v, phi_k1, phi_k2, phi_v1, phi_v2, batch, seq_len):
    q, kc, vc, ks, vs, kw, vw, p_g = _nsa_proj(xb, wl)
    gb = NSA_KV_GROUPS * batch
    per_seq = lambda z: z.reshape(gb, seq_len, z.shape[-1])
    k_cmp = _compress(per_seq(kc), pos_k, phi_k1, phi_k2)
    v_cmp = _compress(per_seq(vc), pos_v, phi_v1, phi_v2)
    return _nsa_attention(q, p_g, k_cmp, v_cmp, per_seq(ks), per_seq(vs), per_seq(kw), per_seq(vw),
                          batch, seq_len)


def kernel(x, w_in, rw_mu, rw_w0, rw_w2, rw_a0, rw_a2, rw_g2, rw_v0, rw_v1, rw_v2, rw_k_k, rw_k_a, rw_r_k, rw_gn_g, rw_gn_b, gm_ln_g, gm_ln_b, gm_ws, gm_bs, nsa_pos_k, nsa_pos_v, nsa_phi_k1, nsa_phi_k2, nsa_phi_v1, nsa_phi_v2, w_br, w_o, ffn1_wg, ffn1_wu, ffn1_wd, ffn2_wg, ffn2_wu, ffn2_wd, ln_g, ln_b):
    batch, seq_len, d = x.shape
    m = batch * seq_len
    h = x.reshape(m, d)
    v_first = None
    for l in range(DEPTH):
        h = _ffn(h, ffn1_wg[l].astype(BF16), ffn1_wu[l].astype(BF16), ffn1_wd[l].astype(BF16),
                 ln_g[l, 0], ln_b[l, 0])
        hb = h.astype(BF16)
        wl = w_in[l]
        vres = None if l == 0 else (v_first, rw_v0[l - 1], rw_v1[l - 1], rw_v2[l - 1])
        y_rw, v_out = _rwkv_branch(hb, wl, rw_mu[l], rw_w0[l], rw_w2[l], rw_a0[l], rw_a2[l], rw_g2[l],
                                   rw_k_k[l], rw_k_a[l], rw_r_k[l], rw_gn_g[l], rw_gn_b[l], vres,
                                   batch, seq_len)
        if l == 0:
            v_first = v_out
        p_gm = _mm(hb, wl[:, OFF_GM:OFF_NSA].astype(BF16), F32)
        y_gm = _gmlp(p_gm, gm_ln_g[l], gm_ln_b[l], gm_ws[l], gm_bs[l])
        y_ns = _nsa_branch(hb, wl, nsa_pos_k[l], nsa_pos_v[l], nsa_phi_k1[l], nsa_phi_k2[l],
                           nsa_phi_v1[l], nsa_phi_v2[l], batch, seq_len)
        p_gate = _mm(hb, wl[:, OFF_GATE:].astype(BF16), F32)
        merged = _merge(y_rw, y_gm, y_ns, p_gate, w_br[l].astype(BF16))
        h = _out_ln(merged, w_o[l].astype(BF16), h, ln_g[l, 1], ln_b[l, 1])
        h = _ffn(h, ffn2_wg[l].astype(BF16), ffn2_wu[l].astype(BF16), ffn2_wd[l].astype(BF16),
                 ln_g[l, 2], ln_b[l, 2])
    return h.reshape(batch, seq_len, d)
```
